```python
import jax
import jax.numpy as jnp
from jax import lax
import numpy as np

D_MODEL = 2048
BATCH = 2
SEQ = 16384
DEPTH = 2

A_HEADS = 16
A_KV_HEADS = 2
A_HEAD_DIM = 64
WINDOW = 128
ATT_BLOCK = 128
ROPE_THETA = 10000.0
B_HEADS = 8
B_KEY_DIM = 128
B_VAL_DIM = 128
GLA_CHUNK = 64
C_WIDTH = D_MODEL
C_GROUP = 16
C_GROUPS = C_WIDTH // C_GROUP
C_STATE = 64
SSM_CHUNK = 128
DT_MIN = 1e-3
DT_MAX = 1e-1
N_EXPERTS = 32
TOP_K = 4
D_FF = D_MODEL // 2
SWIGLU_LIMIT = 7.0
SWIGLU_ALPHA = 1.702
MOE_BLOCK = 128
A_Q = A_HEADS * A_HEAD_DIM
A_KV = A_KV_HEADS * A_HEAD_DIM
B_QK = B_HEADS * B_KEY_DIM
B_V = B_HEADS * B_VAL_DIM
AB_IN = A_Q + 2 * A_KV + 2 * B_QK + 2 * B_V
AB_MIX = A_Q + B_V
N_AB = (DEPTH + 1) // 2
N_C = DEPTH // 2
DN_ALPHA = (2 * DEPTH) ** 0.25
DN_BETA = (8 * DEPTH) ** -0.25
LN_EPS = 1e-5
RMS_EPS = 1e-6

kernel_name = 'hybrid_swa_hgrn2_s5_moe_deepnorm'


def layer_norm(x, g, b):
    xf = x.astype(jnp.float32)
    mu = jnp.mean(xf, axis=-1, keepdims=True)
    var = jnp.mean(jnp.square(xf - mu), axis=-1, keepdims=True)
    return ((xf - mu) * lax.rsqrt(var + LN_EPS) * g.astype(jnp.float32) + b.astype(jnp.float32)).astype(x.dtype)


def rope(x, positions):
    half = x.shape[-1] // 2
    inv_freq = ROPE_THETA ** (-jnp.arange(half, dtype=jnp.float32) / half)
    ang = positions.astype(jnp.float32)[..., None] * inv_freq
    cos = jnp.cos(ang)[:, :, None, :]
    sin = jnp.sin(ang)[:, :, None, :]
    xf = x.astype(jnp.float32)
    x1, x2 = xf[..., :half], xf[..., half:]
    return jnp.concatenate([x1 * cos - x2 * sin, x2 * cos + x1 * sin], axis=-1).astype(x.dtype)


def sliding_window_attention(q, k, v, sinks):
    bsz, seq, hq, dh = q.shape
    hkv = k.shape[2]
    grp = hq // hkv
    nb = seq // ATT_BLOCK
    qb = q.reshape(bsz, nb, ATT_BLOCK, hkv, grp, dh)

    def with_prev(t):
        tb = t.reshape(bsz, nb, ATT_BLOCK, hkv, dh)
        prev = jnp.pad(tb[:, :-1], ((0, 0), (1, 0), (0, 0), (0, 0), (0, 0)))
        return jnp.concatenate([prev, tb], axis=2)

    kb, vb = with_prev(k), with_prev(v)
    s = jnp.einsum('bnqhgd,bnshd->bnhgqs', qb, kb, preferred_element_type=jnp.float32) * (dh ** -0.5)
    qi = jnp.arange(ATT_BLOCK)[:, None]
    kj = jnp.arange(2 * ATT_BLOCK)[None, :] - ATT_BLOCK
    rel = qi - kj
    band = (rel >= 0) & (rel < WINDOW)
    has_prev = (jnp.arange(nb) > 0)[:, None, None]
    mask = band[None] & (has_prev | (kj >= 0)[None])
    s = jnp.where(mask[None, :, None, None], s, -jnp.inf)
    sink = sinks.astype(jnp.float32).reshape(hkv, grp)[None, None, :, :, None, None]
    m = jnp.maximum(jnp.max(s, axis=-1, keepdims=True), sink)
    p = jnp.exp(s - m)
    p = p / (jnp.sum(p, axis=-1, keepdims=True) + jnp.exp(sink - m))
    o = jnp.einsum('bnhgqs,bnshd->bnqhgd', p.astype(v.dtype), vb)
    return o.reshape(bsz, seq, hq * dh)


def hgrn2_recurrence(q, k, v, log_f):
    bsz, seq, nh, dk = q.shape
    dv = v.shape[-1]
    L = GLA_CHUNK
    nc = seq // L

    def chunk(t):
        return t.reshape(bsz, nc, L, nh, t.shape[-1]).transpose(0, 1, 3, 2, 4)

    q, k, v, log_f = chunk(q), chunk(k), chunk(v), chunk(log_f)
    b = jnp.cumsum(log_f, axis=3)
    b_mid = b[:, :, :, L // 2:L // 2 + 1]
    b_last = b[:, :, :, -1:]
    att = jnp.einsum('bnhtd,bnhsd->bnhts', q * jnp.exp(b - b_mid), k * jnp.exp(b_mid - b))
    causal = jnp.tril(jnp.ones((L, L), dtype=bool))
    att = jnp.where(causal, att, 0.0)
    o_intra = jnp.einsum('bnhts,bnhsv->bnhtv', att, v)
    u = jnp.einsum('bnhsd,bnhsv->bnhdv', k * jnp.exp(b_last - b), v)
    decay = jnp.exp(b_last[:, :, :, 0])

    def step(state, inp):
        dec, uc = inp
        return dec[..., None] * state + uc, state

    init = jnp.zeros((bsz, nh, dk, dv), jnp.float32)
    _, s_start = lax.scan(step, init, (decay.transpose(1, 0, 2, 3), u.transpose(1, 0, 2, 3, 4)))
    s_start = s_start.transpose(1, 0, 2, 3, 4)
    o_inter = jnp.einsum('bnhtd,bnhdv->bnhtv', q * jnp.exp(b), s_start)
    o = o_intra + o_inter
    return o.transpose(0, 1, 3, 2, 4).reshape(bsz, seq, nh, dv)


def mixer_ab(h, positions, in_w, in_b, sinks, gnorm_w, out_w, out_b, lower_bound):
    bsz, seq, _ = h.shape
    f32 = jnp.float32
    proj = h @ in_w + in_b
    cuts = [A_Q, A_Q + A_KV, A_Q + 2 * A_KV, A_Q + 2 * A_KV + B_QK,
            A_Q + 2 * A_KV + 2 * B_QK, A_Q + 2 * A_KV + 2 * B_QK + B_V]
    qa, ka, va, qb, fb, ib, gb = jnp.split(proj, cuts, axis=-1)
    qa = rope(qa.reshape(bsz, seq, A_HEADS, A_HEAD_DIM), positions)
    ka = rope(ka.reshape(bsz, seq, A_KV_HEADS, A_HEAD_DIM), positions)
    va = va.reshape(bsz, seq, A_KV_HEADS, A_HEAD_DIM)
    oa = sliding_window_attention(qa, ka, va, sinks)
    lb = lower_bound.astype(f32).reshape(B_HEADS, B_KEY_DIM)
    fgate = lb + (1.0 - lb) * jax.nn.sigmoid(fb.astype(f32).reshape(bsz, seq, B_HEADS, B_KEY_DIM))
    qry = jax.nn.silu(qb.astype(f32).reshape(bsz, seq, B_HEADS, B_KEY_DIM))
    val = ib.astype(f32).reshape(bsz, seq, B_HEADS, B_VAL_DIM)
    ob = hgrn2_recurrence(qry, 1.0 - fgate, val, jnp.log(fgate))
    ob = ob * lax.rsqrt(jnp.mean(jnp.square(ob), axis=-1, keepdims=True) + RMS_EPS) * gnorm_w.astype(f32)
    ob = ob * jax.nn.silu(gb.astype(f32).reshape(bsz, seq, B_HEADS, B_VAL_DIM))
    o = jnp.concatenate([oa, ob.reshape(bsz, seq, B_V).astype(h.dtype)], axis=-1)
    return o @ out_w + out_b


def _linear_combine(e1, e2):
    a1, b1 = e1
    a2, b2 = e2
    return a1 * a2, a2 * b1 + b2


def mixer_ssm(h, in_w, A_re, A_im, log_dt, B_re, B_im, C_re, C_im, D, glu_w, glu_b, out_w):
    bsz, seq, _ = h.shape
    f32 = jnp.float32
    u = (h @ in_w).astype(f32).reshape(bsz, seq, C_GROUPS, C_GROUP)
    lam = lax.complex(A_re.astype(f32), A_im.astype(f32))
    dt = jnp.exp(log_dt.astype(f32))[:, None]
    lam_bar = jnp.exp(lam * dt)
    b_bar = ((lam_bar - 1.0) / lam)[..., None] * lax.complex(B_re.astype(f32), B_im.astype(f32))
    c_mat = lax.complex(C_re.astype(f32), C_im.astype(f32))
    nc = seq // SSM_CHUNK
    u_chunks = u.reshape(bsz, nc, SSM_CHUNK, C_GROUPS, C_GROUP).transpose(1, 0, 2, 3, 4)

    def step(state, uc):
        bu = jnp.einsum('btgc,gpc->btgp', uc.astype(jnp.complex64), b_bar)
        a = jnp.broadcast_to(lam_bar, bu.shape)
        a_cum, x_loc = lax.associative_scan(_linear_combine, (a, bu), axis=1)
        xs = a_cum * state[:, None] + x_loc
        y = jnp.real(jnp.einsum('btgp,gcp->btgc', xs, c_mat))
        return xs[:, -1], y

    init = jnp.zeros((bsz, C_GROUPS, C_STATE), jnp.complex64)
    _, ys = lax.scan(step, init, u_chunks)
    y = ys.transpose(1, 0, 2, 3, 4).reshape(bsz, seq, C_GROUPS, C_GROUP) + D.astype(f32) * u
    y = jax.nn.gelu(y.reshape(bsz, seq, C_WIDTH)).astype(h.dtype)
    y = y * jax.nn.sigmoid(y @ glu_w + glu_b)
    return y @ out_w


def clamped_swiglu(hid):
    gate, lin = jnp.split(hid, 2, axis=-1)
    gate = jnp.minimum(gate, SWIGLU_LIMIT)
    lin = jnp.clip(lin, -SWIGLU_LIMIT, SWIGLU_LIMIT)
    return gate * jax.nn.sigmoid(SWIGLU_ALPHA * gate) * (lin + 1.0)


def moe(h, router_w, router_b, w1, b1, w2, b2):
    bsz, seq, dm = h.shape
    f32 = jnp.float32
    n_tok = bsz * seq
    ht = h.reshape(n_tok, dm)
    logits = (ht @ router_w + router_b).astype(f32)
    top_val, top_idx = lax.top_k(logits, TOP_K)
    gates = jax.nn.softmax(top_val, axis=-1)
    n_assign = n_tok * TOP_K
    flat_e = top_idx.reshape(-1)
    flat_t = jnp.repeat(jnp.arange(n_tok, dtype=jnp.int32), TOP_K)
    flat_g = gates.reshape(-1)
    order = jnp.argsort(flat_e)
    se = flat_e[order]
    counts = jnp.bincount(flat_e, length=N_EXPERTS)
    padded = (counts + MOE_BLOCK - 1) // MOE_BLOCK * MOE_BLOCK
    start_sorted = jnp.cumsum(counts) - counts
    end_padded = jnp.cumsum(padded)
    start_padded = end_padded - padded
    dest = start_padded[se] + jnp.arange(n_assign, dtype=jnp.int32) - start_sorted[se]
    n_rows = -(-(n_assign + N_EXPERTS * (MOE_BLOCK - 1)) // MOE_BLOCK) * MOE_BLOCK
    n_blocks = n_rows // MOE_BLOCK
    row_tok = jnp.zeros((n_rows,), jnp.int32).at[dest].set(flat_t[order])
    row_gate = jnp.zeros((n_rows,), f32).at[dest].set(flat_g[order])
    blk_exp = jnp.minimum(
        jnp.searchsorted(end_padded, jnp.arange(n_blocks, dtype=jnp.int32) * MOE_BLOCK, side='right'),
        N_EXPERTS - 1)

    def block_step(acc, inp):
        tok, gate, e = inp
        hid = ht[tok] @ w1[e] + b1[e]
        y = clamped_swiglu(hid) @ w2[e] + b2[e]
        return acc.at[tok].add(gate[:, None] * y.astype(f32)), None

    acc0 = jnp.zeros((n_tok, dm), f32)
    out, _ = lax.scan(block_step, acc0, (row_tok.reshape(n_blocks, MOE_BLOCK),
                                         row_gate.reshape(n_blocks, MOE_BLOCK), blk_exp))
    return out.reshape(bsz, seq, dm).astype(h.dtype)


def setup_inputs(seed: int = 0) -> dict:
    key = jax.random.key(seed)
    keys = jax.random.split(key, 32)
    f32 = jnp.float32

    def nrm(i, shape, scale):
        return scale * jax.random.normal(keys[i], shape, f32)

    positions = jnp.arange(SEQ, dtype=jnp.int32)[None, :] + jax.random.randint(keys[2], (BATCH, 1), 0, 1024, jnp.int32)
    a_im = jnp.pi * jnp.broadcast_to(jnp.arange(C_STATE, dtype=f32), (N_C, C_GROUPS, C_STATE))
    return {
        'x': nrm(0, (BATCH, SEQ, D_MODEL), 1.0),
        'c': nrm(1, (BATCH, D_MODEL), 1.0),
        'positions': positions,
        'ada_w': nrm(3, (DEPTH, D_MODEL, 6 * D_MODEL), 0.5 * D_MODEL ** -0.5),
        'ada_b': nrm(4, (DEPTH, 6 * D_MODEL), 0.01),
        'ln_g': 1.0 + nrm(5, (DEPTH, 2, D_MODEL), 0.02),
        'ln_b': nrm(6, (DEPTH, 2, D_MODEL), 0.02),
        'ab_in_w': nrm(7, (N_AB, D_MODEL, AB_IN), D_MODEL ** -0.5),
        'ab_in_b': nrm(8, (N_AB, AB_IN), 0.02),
        'ab_sinks': nrm(9, (N_AB, A_HEADS), 0.5),
        'ab_gnorm_w': 1.0 + nrm(10, (N_AB, B_HEADS, B_VAL_DIM), 0.02),
        'ab_out_w': nrm(11, (N_AB, AB_MIX, D_MODEL), DN_BETA * AB_MIX ** -0.5),
        'ab_out_b': nrm(12, (N_AB, D_MODEL), 0.02),
        'hgrn_lb_logits': nrm(13, (DEPTH + 1, B_QK), 0.1),
        'c_in_w': nrm(14, (N_C, D_MODEL, C_WIDTH), D_MODEL ** -0.5),
        'c_A_re': -0.5 + nrm(15, (N_C, C_GROUPS, C_STATE), 0.01),
        'c_A_im': a_im + nrm(16, (N_C, C_GROUPS, C_STATE), 0.01),
        'c_log_dt': jax.random.uniform(keys[17], (N_C, C_GROUPS), f32, float(np.log(DT_MIN)), float(np.log(DT_MAX))),
        'c_B_re': nrm(18, (N_C, C_GROUPS, C_STATE, C_GROUP), (2 * C_GROUP) ** -0.5),
        'c_B_im': nrm(19, (N_C, C_GROUPS, C_STATE, C_GROUP), (2 * C_GROUP) ** -0.5),
        'c_C_re': nrm(20, (N_C, C_GROUPS, C_GROUP, C_STATE), C_STATE ** -0.5),
        'c_C_im': nrm(21, (N_C, C_GROUPS, C_GROUP, C_STATE), C_STATE ** -0.5),
        'c_D': nrm(22, (N_C, C_GROUPS, C_GROUP), 1.0),
        'c_glu_w': nrm(23, (N_C, C_WIDTH, C_WIDTH), C_WIDTH ** -0.5),
        'c_glu_b': nrm(24, (N_C, C_WIDTH), 0.02),
        'c_out_w': nrm(25, (N_C, C_WIDTH, D_MODEL), DN_BETA * C_WIDTH ** -0.5),
        'router_w': nrm(26, (DEPTH, D_MODEL, N_EXPERTS), D_MODEL ** -0.5),
        'router_b': nrm(27, (DEPTH, N_EXPERTS), 0.01),
        'exp_w1': nrm(28, (DEPTH, N_EXPERTS, D_MODEL, 2 * D_FF), D_MODEL ** -0.5),
        'exp_b1': nrm(29, (DEPTH, N_EXPERTS, 2 * D_FF), 0.02),
        'exp_w2': nrm(30, (DEPTH, N_EXPERTS, D_FF, D_MODEL), DN_BETA * D_FF ** -0.5),
        'exp_b2': nrm(31, (DEPTH, N_EXPERTS, D_MODEL), 0.02),
    }


def reference(x, c, positions, ada_w, ada_b, ln_g, ln_b, ab_in_w, ab_in_b, ab_sinks, ab_gnorm_w,
              ab_out_w, ab_out_b, hgrn_lb_logits, c_in_w, c_A_re, c_A_im, c_log_dt, c_B_re, c_B_im,
              c_C_re, c_C_im, c_D, c_glu_w, c_glu_b, c_out_w, router_w, router_b, exp_w1, exp_b1,
              exp_w2, exp_b2):
    cond = jax.nn.silu(c)
    lb_all = jnp.cumsum(jax.nn.softmax(hgrn_lb_logits.astype(jnp.float32), axis=0), axis=0)
    h = x
    for layer in range(DEPTH):
        mod = cond @ ada_w[layer] + ada_b[layer]
        sh_m, sc_m, g_m, sh_f, sc_f, g_f = [m[:, None, :] for m in jnp.split(mod, 6, axis=-1)]
        xin = h * (1.0 + sc_m) + sh_m
        i = layer // 2
        if layer % 2 == 0:
            y = mixer_ab(xin, positions, ab_in_w[i], ab_in_b[i], ab_sinks[i], ab_gnorm_w[i],
                         ab_out_w[i], ab_out_b[i], lb_all[layer])
        else:
            y = mixer_ssm(xin, c_in_w[i], c_A_re[i], c_A_im[i], c_log_dt[i], c_B_re[i], c_B_im[i],
                          c_C_re[i], c_C_im[i], c_D[i], c_glu_w[i], c_glu_b[i], c_out_w[i])
        h = layer_norm(DN_ALPHA * h + (1.0 + g_m) * y, ln_g[layer, 0], ln_b[layer, 0])
        xin = h * (1.0 + sc_f) + sh_f
        y = moe(xin, router_w[layer], router_b[layer], exp_w1[layer], exp_b1[layer],
                exp_w2[layer], exp_b2[layer])
        h = layer_norm(DN_ALPHA * h + (1.0 + g_f) * y, ln_g[layer, 1], ln_b[layer, 1])
    return h
```

```python
import functools

import jax
import jax.numpy as jnp
from jax import lax
from jax.experimental import pallas as pl
from jax.experimental.pallas import tpu as pltpu

F32 = jnp.float32
BF16 = jnp.bfloat16
I32 = jnp.int32
HIGHEST = lax.Precision.HIGHEST

DEPTH = 2
A_HEADS = 16
A_KV_HEADS = 2
A_HEAD_DIM = 64
WINDOW = 128
ROPE_THETA = 10000.0
B_HEADS = 8
B_DIM = 128
GLA_CHUNK = 64
C_GROUP = 16
C_STATE = 64
N_EXPERTS = 32
TOP_K = 4
SWIGLU_LIMIT = 7.0
SWIGLU_ALPHA = 1.702
DN_ALPHA = (2 * DEPTH) ** 0.25
LN_EPS = 1e-5
RMS_EPS = 1e-6

LANES = 128
VMEM_LIMIT = 56 * 1024 * 1024

ATT_BLK = 128
HGRN_T = 512
MOE_BLK = 256
SSM_T = 16
SSM_PACK = LANES // C_GROUP
NEG = -1e30


def _cparams(sem):
    return pltpu.CompilerParams(dimension_semantics=sem, vmem_limit_bytes=VMEM_LIMIT)


def _resident(block_shape, index_map):
    return pl.BlockSpec(block_shape, index_map, pipeline_mode=pl.Buffered(1))


def _layer_norm(z, g, b):
    mu = jnp.mean(z, axis=-1, keepdims=True)
    zc = z - mu
    var = jnp.mean(zc * zc, axis=-1, keepdims=True)
    return zc * lax.rsqrt(var + LN_EPS) * g + b


def _ada_body(ct_ref, w_ref, b_ref, o_ref):
    ct = ct_ref[...]
    cond = ct * jax.nn.sigmoid(ct)
    w = w_ref[0]
    rows = [jnp.sum(w * cond[:, b:b + 1], axis=0, keepdims=True) for b in range(ct.shape[1])]
    o_ref[0] = jnp.concatenate(rows, axis=0) + b_ref[0]


def ada_mod(c, ada_w, ada_b):
    depth, d, n6 = ada_w.shape
    bsz = c.shape[0]
    tn = 1536
    return pl.pallas_call(
        _ada_body,
        grid=(depth, n6 // tn),
        in_specs=[pl.BlockSpec((d, bsz), lambda l, j: (0, 0)),
                  pl.BlockSpec((1, d, tn), lambda l, j: (l, 0, j)),
                  pl.BlockSpec((1, 1, tn), lambda l, j: (l, 0, j))],
        out_specs=pl.BlockSpec((1, bsz, tn), lambda l, j: (l, 0, j)),
        out_shape=jax.ShapeDtypeStruct((depth, bsz, n6), F32),
        compiler_params=_cparams(("arbitrary", "arbitrary")),
        name="ada_mod",
    )(c.T, ada_w, ada_b.reshape(depth, 1, n6))


def _inproj_ab_body(x_ref, sc_ref, sh_ref, w_ref, b_ref, o_ref, xb_ref):
    @pl.when(pl.program_id(1) == 0)
    def _():
        xb_ref[...] = (x_ref[...] * (1.0 + sc_ref[0]) + sh_ref[0]).astype(BF16)

    o_ref[...] = jnp.dot(xb_ref[...], w_ref[...], preferred_element_type=F32) + b_ref[...]


def inproj_ab(h, sc, sh, w_bf, b, seq):
    n, d = h.shape
    nc = w_bf.shape[1]
    tm = min(1024, seq)
    tn = 768
    per_b = seq // tm
    return pl.pallas_call(
        _inproj_ab_body,
        grid=(n // tm, nc // tn),
        in_specs=[pl.BlockSpec((tm, d), lambda i, j: (i, 0)),
                  pl.BlockSpec((1, 1, d), lambda i, j: (i // per_b, 0, 0)),
                  pl.BlockSpec((1, 1, d), lambda i, j: (i // per_b, 0, 0)),
                  pl.BlockSpec((d, tn), lambda i, j: (0, j)),
                  pl.BlockSpec((1, tn), lambda i, j: (0, j))],
        out_specs=pl.BlockSpec((tm, tn), lambda i, j: (i, j)),
        out_shape=jax.ShapeDtypeStruct((n, nc), F32),
        scratch_shapes=[pltpu.VMEM((tm, d), BF16)],
        compiler_params=_cparams(("arbitrary", "arbitrary")),
        name="inproj_ab",
    )(h, sc, sh, w_bf, b.reshape(1, nc))


def _attn_body(sink_ref, q_ref, kc_ref, kp_ref, vc_ref, vp_ref, pc_ref, pp_ref, invf_ref, o_ref):
    blk = ATT_BLK
    half = A_HEAD_DIM // 2
    grp = A_HEADS // A_KV_HEADS
    i = pl.program_id(1)
    lane = lax.broadcasted_iota(I32, (blk, LANES), 1)
    first_half = (lane % A_HEAD_DIM) < half
    low_head = lane < A_HEAD_DIM
    invf = invf_ref[...]

    def cos_sin(pos_col):
        ang = pos_col.astype(F32) * invf
        s = jnp.sin(ang)
        return jnp.cos(ang), jnp.where(first_half, -s, s)

    def rope(x, cs):
        partner = jnp.where(first_half, pltpu.roll(x, LANES - half, 1), pltpu.roll(x, half, 1))
        return x * cs[0] + partner * cs[1]

    cs_c = cos_sin(pc_ref[...])
    cs_p = cos_sin(pp_ref[...])
    k2 = jnp.concatenate([rope(kp_ref[...], cs_p), rope(kc_ref[...], cs_c)], axis=0)
    v2 = jnp.concatenate([vp_ref[...], vc_ref[...]], axis=0)
    lane2 = lax.broadcasted_iota(I32, (2 * blk, LANES), 1)
    k2r = pltpu.roll(k2, A_HEAD_DIM, 1)
    v2r = pltpu.roll(v2, A_HEAD_DIM, 1)

    rows = lax.broadcasted_iota(I32, (grp * blk, 2 * blk), 0) % blk
    cols = lax.broadcasted_iota(I32, (grp * blk, 2 * blk), 1)
    valid = (cols > rows) & (cols <= rows + WINDOW) & ((cols >= blk) | (i > 0))

    q_chunks = [rope(q_ref[:, c * LANES:(c + 1) * LANES], cs_c) * (A_HEAD_DIM ** -0.5)
                for c in range(A_HEADS * A_HEAD_DIM // LANES)]

    for g in range(A_KV_HEADS):
        own = (lane2 < A_HEAD_DIM) if g == 0 else (lane2 >= A_HEAD_DIM)
        kg = jnp.where(own, k2, k2r).astype(BF16)
        vg = jnp.where(own, v2, v2r).astype(BF16)
        pieces = []
        sinks = []
        for r in range(grp):
            hd = g * grp + r
            qc = q_chunks[hd // 2]
            keep = low_head if hd % 2 == 0 else jnp.logical_not(low_head)
            pieces.append(jnp.where(keep, qc, 0.0).astype(BF16))
            sinks.append(jnp.full((blk, 1), sink_ref[hd], F32))
        qs = jnp.concatenate(pieces, axis=0)
        sink = jnp.concatenate(sinks, axis=0)
        s = lax.dot_general(qs, kg, (((1,), (1,)), ((), ())), preferred_element_type=F32)
        s = jnp.where(valid, s, NEG)
        m = jnp.maximum(jnp.max(s, axis=-1, keepdims=True), sink)
        p = jnp.exp(s - m)
        den = jnp.sum(p, axis=-1, keepdims=True) + jnp.exp(sink - m)
        o = jnp.dot(p.astype(BF16), vg, preferred_element_type=F32) / den
        for r2 in range(grp // 2):
            c = (g * grp) // 2 + r2
            ev = o[(2 * r2) * blk:(2 * r2 + 1) * blk]
            od = o[(2 * r2 + 1) * blk:(2 * r2 + 2) * blk]
            o_ref[:, c * LANES:(c + 1) * LANES] = jnp.where(low_head, ev, od).astype(o_ref.dtype)


def swa_attention(proj, pos_col, invf, sinks, bsz, seq):
    n = proj.shape[0]
    blk = ATT_BLK
    nb = seq // blk
    aq = A_HEADS * A_HEAD_DIM
    kcol = aq // LANES
    vcol = kcol + 1

    def cur(b, i, s):
        return b * nb + i

    def prev(b, i, s):
        return b * nb + jnp.maximum(i - 1, 0)

    grid_spec = pltpu.PrefetchScalarGridSpec(
        num_scalar_prefetch=1,
        grid=(bsz, nb),
        in_specs=[pl.BlockSpec((blk, aq), lambda b, i, s: (cur(b, i, s), 0)),
                  pl.BlockSpec((blk, LANES), lambda b, i, s: (cur(b, i, s), kcol)),
                  pl.BlockSpec((blk, LANES), lambda b, i, s: (prev(b, i, s), kcol)),
                  pl.BlockSpec((blk, LANES), lambda b, i, s: (cur(b, i, s), vcol)),
                  pl.BlockSpec((blk, LANES), lambda b, i, s: (prev(b, i, s), vcol)),
                  pl.BlockSpec((blk, 1), lambda b, i, s: (cur(b, i, s), 0)),
                  pl.BlockSpec((blk, 1), lambda b, i, s: (prev(b, i, s), 0)),
                  pl.BlockSpec((1, LANES), lambda b, i, s: (0, 0))],
        out_specs=pl.BlockSpec((blk, aq), lambda b, i, s: (cur(b, i, s), 0)),
    )
    return pl.pallas_call(
        _attn_body,
        grid_spec=grid_spec,
        out_shape=jax.ShapeDtypeStruct((n, aq), BF16),
        compiler_params=_cparams(("arbitrary", "arbitrary")),
        name="swa_attention",
    )(sinks, proj, proj, proj, proj, proj, pos_col, pos_col, invf)


def _hgrn_body(q_ref, f_ref, i_ref, g_ref, lb_ref, gw_ref, o_ref, st_ref):
    L = GLA_CHUNK

    @pl.when(pl.program_id(2) == 0)
    def _():
        st_ref[...] = jnp.zeros_like(st_ref)

    lb = lb_ref[0]
    gw = gw_ref[0]
    r = lax.broadcasted_iota(I32, (L, L), 0)
    c = lax.broadcasted_iota(I32, (L, L), 1)
    causal = r >= c
    tril = causal.astype(F32)
    nt = (((1,), (1,)), ((), ()))

    def chunk(ci, carry):
        sl = pl.ds(pl.multiple_of(ci * L, L), L)
        f = lb + (1.0 - lb) * jax.nn.sigmoid(f_ref[sl, :])
        qb = q_ref[sl, :]
        q = qb * jax.nn.sigmoid(qb)
        v = i_ref[sl, :]
        kk = 1.0 - f
        b = jnp.dot(tril, jnp.log(f), precision=HIGHEST, preferred_element_type=F32)
        b_mid = b[L // 2:L // 2 + 1]
        b_last = b[L - 1:L]
        qa = (q * jnp.exp(b - b_mid)).astype(BF16)
        ka = (kk * jnp.exp(b_mid - b)).astype(BF16)
        att = lax.dot_general(qa, ka, nt, preferred_element_type=F32)
        att = jnp.where(causal, att, 0.0)
        v_bf = v.astype(BF16)
        o = jnp.dot(att.astype(BF16), v_bf, preferred_element_type=F32)
        st = st_ref[...]
        qe = (q * jnp.exp(b)).astype(BF16)
        o = o + lax.dot_general(qe, st.astype(BF16), nt, preferred_element_type=F32)
        kd = (kk * jnp.exp(b_last - b)).astype(BF16)
        u_t = jnp.dot(v.T.astype(BF16), kd, preferred_element_type=F32)
        st_ref[...] = jnp.exp(b_last) * st + u_t
        o = o * lax.rsqrt(jnp.mean(o * o, axis=-1, keepdims=True) + RMS_EPS) * gw
        gb = g_ref[sl, :]
        o_ref[sl, :] = (o * (gb * jax.nn.sigmoid(gb))).astype(o_ref.dtype)
        return carry

    lax.fori_loop(0, q_ref.shape[0] // L, chunk, 0)


def hgrn2(proj, lb, gnorm_w, bsz, seq):
    n = proj.shape[0]
    t = min(HGRN_T, seq)
    nt = seq // t
    base = (A_HEADS + 2 * A_KV_HEADS) * A_HEAD_DIM // LANES
    h = B_HEADS

    def col(k):
        return lambda b, hd, i: (b * nt + i, base + k * h + hd)

    return pl.pallas_call(
        _hgrn_body,
        grid=(bsz, h, nt),
        in_specs=[pl.BlockSpec((t, LANES), col(0)),
                  pl.BlockSpec((t, LANES), col(1)),
                  pl.BlockSpec((t, LANES), col(2)),
                  pl.BlockSpec((t, LANES), col(3)),
                  pl.BlockSpec((1, 1, B_DIM), lambda b, hd, i: (hd, 0, 0)),
                  pl.BlockSpec((1, 1, B_DIM), lambda b, hd, i: (hd, 0, 0))],
        out_specs=pl.BlockSpec((t, LANES), lambda b, hd, i: (b * nt + i, hd)),
        out_shape=jax.ShapeDtypeStruct((n, h * B_DIM), BF16),
        scratch_shapes=[pltpu.VMEM((B_DIM, B_DIM), F32)],
        compiler_params=_cparams(("arbitrary", "arbitrary", "arbitrary")),
        name="hgrn2",
    )(proj, proj, proj, proj, lb.reshape(h, 1, B_DIM), gnorm_w.reshape(h, 1, B_DIM))


def _outproj_ab_body(oa_ref, ob_ref, w_ref, b_ref, h_ref, gm_ref, lg_ref, lb_ref, o_ref):
    ka = oa_ref.shape[1]
    y = jnp.dot(oa_ref[...], w_ref[:ka, :], preferred_element_type=F32)
    y = y + jnp.dot(ob_ref[...], w_ref[ka:, :], preferred_element_type=F32) + b_ref[...]
    z = DN_ALPHA * h_ref[...] + (1.0 + gm_ref[0]) * y
    o_ref[...] = _layer_norm(z, lg_ref[...], lb_ref[...])


def outproj_ab(oa, ob, w_bf, b, h, gm, ln_g, ln_b, seq):
    n, d = h.shape
    tm = min(512, seq)
    per_b = seq // tm
    ka, kb = oa.shape[1], ob.shape[1]
    row = lambda i: (i, 0)
    const = lambda i: (0, 0)
    return pl.pallas_call(
        _outproj_ab_body,
        grid=(n // tm,),
        in_specs=[pl.BlockSpec((tm, ka), row),
                  pl.BlockSpec((tm, kb), row),
                  _resident((ka + kb, d), const),
                  pl.BlockSpec((1, d), const),
                  pl.BlockSpec((tm, d), row),
                  pl.BlockSpec((1, 1, d), lambda i: (i // per_b, 0, 0)),
                  pl.BlockSpec((1, d), const),
                  pl.BlockSpec((1, d), const)],
        out_specs=pl.BlockSpec((tm, d), row),
        out_shape=jax.ShapeDtypeStruct((n, d), F32),
        compiler_params=_cparams(("arbitrary",)),
        name="outproj_ab",
    )(oa, ob, w_bf, b.reshape(1, d), h, gm, ln_g.reshape(1, d), ln_b.reshape(1, d))


def _router_body(h_ref, sc_ref, sh_ref, rw_ref, rb_ref, meta_ref, cnt_ref, carry_ref):
    tm = h_ref.shape[0]

    @pl.when(pl.program_id(0) == 0)
    def _():
        carry_ref[...] = jnp.zeros_like(carry_ref)

    xin = h_ref[...] * (1.0 + sc_ref[0]) + sh_ref[0]
    logits = jnp.dot(xin, rw_ref[...], precision=HIGHEST, preferred_element_type=F32) + rb_ref[...]
    lane = lax.broadcasted_iota(I32, (tm, LANES), 1)
    l = logits
    vals, idxs = [], []
    sel = jnp.zeros((tm, LANES), F32)
    for _ in range(TOP_K):
        m = jnp.max(l, axis=-1, keepdims=True)
        idx = jnp.min(jnp.where(l == m, lane, LANES), axis=-1, keepdims=True)
        hit = lane == idx
        vals.append(m)
        idxs.append(idx)
        sel = jnp.where(hit, 1.0, sel)
        l = jnp.where(hit, -jnp.inf, l)
    es = [jnp.exp(v - vals[0]) for v in vals]
    den = es[0] + es[1] + es[2] + es[3]
    r = lax.broadcasted_iota(I32, (tm, tm), 0)
    c = lax.broadcasted_iota(I32, (tm, tm), 1)
    tril = (r >= c).astype(BF16)
    incl = jnp.dot(tril, sel.astype(BF16), preferred_element_type=F32)
    carry = carry_ref[...]
    excl = incl - sel + carry
    meta = jnp.zeros((tm, LANES), F32)
    for k in range(TOP_K):
        rank = jnp.sum(jnp.where(lane == idxs[k], excl, 0.0), axis=-1, keepdims=True)
        meta = jnp.where(lane == k, idxs[k].astype(F32), meta)
        meta = jnp.where(lane == TOP_K + k, es[k] / den, meta)
        meta = jnp.where(lane == 2 * TOP_K + k, rank, meta)
    meta_ref[...] = meta
    carry = carry + incl[tm - 1:tm, :]
    carry_ref[...] = carry
    cnt_ref[...] = jnp.broadcast_to(carry, cnt_ref.shape)


def moe_router(h, sc, sh, rw_pad, rb_pad, seq):
    n, d = h.shape
    tm = min(512, seq)
    per_b = seq // tm
    return pl.pallas_call(
        _router_body,
        grid=(n // tm,),
        in_specs=[pl.BlockSpec((tm, d), lambda i: (i, 0)),
                  pl.BlockSpec((1, 1, d), lambda i: (i // per_b, 0, 0)),
                  pl.BlockSpec((1, 1, d), lambda i: (i // per_b, 0, 0)),
                  pl.BlockSpec((d, LANES), lambda i: (0, 0)),
                  pl.BlockSpec((1, LANES), lambda i: (0, 0))],
        out_specs=[pl.BlockSpec((tm, LANES), lambda i: (i, 0)),
                   pl.BlockSpec((8, LANES), lambda i: (0, 0))],
        out_shape=[jax.ShapeDtypeStruct((n, LANES), F32),
                   jax.ShapeDtypeStruct((8, LANES), F32)],
        scratch_shapes=[pltpu.VMEM((1, LANES), F32)],
        compiler_params=_cparams(("arbitrary",)),
        name="moe_router",
    )(h, sc, sh, rw_pad, rb_pad)


def _dispatch_body(pad_ref, dest_ref, h_ref, sc_ref, sh_ref, xs_hbm, buf_ref, zero_ref, sem, zsem):
    tm = h_ref.shape[0]
    i = pl.program_id(0)
    last = pl.num_programs(0) - 1
    slot = i % 2
    n_pad = pad_ref.shape[0]

    def wait_slot(s):
        for _ in range(TOP_K):
            pltpu.make_async_copy(buf_ref.at[s], xs_hbm.at[pl.ds(0, tm)], sem.at[s]).wait()

    @pl.when(i == 0)
    def _():
        zero_ref[...] = jnp.zeros_like(zero_ref)

        def zfill(q, carry):
            pltpu.make_async_copy(zero_ref.at[pl.ds(0, 1)], xs_hbm.at[pl.ds(pad_ref[q], 1)], zsem.at[0]).start()
            return carry

        lax.fori_loop(0, n_pad, zfill, 0)

    @pl.when(i >= 2)
    def _():
        wait_slot(slot)

    buf_ref[slot] = h_ref[...] * (1.0 + sc_ref[0]) + sh_ref[0]

    def issue(r, carry):
        for k in range(TOP_K):
            d = dest_ref[0, 0, r * TOP_K + k]
            pltpu.make_async_copy(buf_ref.at[slot, pl.ds(r, 1)], xs_hbm.at[pl.ds(d, 1)], sem.at[slot]).start()
        return carry

    lax.fori_loop(0, tm, issue, 0)

    @pl.when(i == last)
    def _():
        wait_slot(slot)

        @pl.when(i >= 1)
        def _():
            wait_slot(1 - slot)

        for _ in range(n_pad // tm):
            pltpu.make_async_copy(buf_ref.at[0], xs_hbm.at[pl.ds(0, tm)], zsem.at[0]).wait()


def moe_dispatch(h, sc, sh, dest, pad_dest, n_rows, seq):
    n, d = h.shape
    tm = min(256, seq)
    per_b = seq // tm
    assert pad_dest.shape[0] % tm == 0
    grid_spec = pltpu.PrefetchScalarGridSpec(
        num_scalar_prefetch=1,
        grid=(n // tm,),
        in_specs=[pl.BlockSpec((1, 1, tm * TOP_K), lambda i, p: (i, 0, 0), memory_space=pltpu.SMEM),
                  pl.BlockSpec((tm, d), lambda i, p: (i, 0)),
                  pl.BlockSpec((1, 1, d), lambda i, p: (i // per_b, 0, 0)),
                  pl.BlockSpec((1, 1, d), lambda i, p: (i // per_b, 0, 0))],
        out_specs=pl.BlockSpec(memory_space=pl.ANY),
        scratch_shapes=[pltpu.VMEM((2, tm, d), F32), pltpu.VMEM((8, d), F32),
                        pltpu.SemaphoreType.DMA((2,)), pltpu.SemaphoreType.DMA((1,))],
    )
    return pl.pallas_call(
        _dispatch_body,
        grid_spec=grid_spec,
        out_shape=jax.ShapeDtypeStruct((n_rows, d), F32),
        compiler_params=_cparams(("arbitrary",)),
        name="moe_dispatch",
    )(pad_dest, dest.reshape(n // tm, 1, tm * TOP_K), h, sc, sh)


def _expert_body(be_ref, nu_ref, xs_ref, w1_ref, b1_ref, w2_ref, b2_ref, y_ref):
    j = pl.program_id(0)
    ff = w2_ref.shape[1]

    @pl.when(j < nu_ref[0])
    def _():
        x = xs_ref[...].astype(BF16)
        hid = jnp.dot(x, w1_ref[0], preferred_element_type=F32) + b1_ref[0]
        gate = jnp.minimum(hid[:, :ff], SWIGLU_LIMIT)
        lin = jnp.clip(hid[:, ff:], -SWIGLU_LIMIT, SWIGLU_LIMIT)
        act = gate * jax.nn.sigmoid(SWIGLU_ALPHA * gate) * (lin + 1.0)
        y_ref[...] = jnp.dot(act.astype(BF16), w2_ref[0], preferred_element_type=F32) + b2_ref[0]

    @pl.when(j >= nu_ref[0])
    def _():
        y_ref[...] = jnp.zeros_like(y_ref)


def moe_experts(xs, blk_exp, n_used, w1_bf, b1, w2_bf, b2):
    n_rows, d = xs.shape
    ne, _, ff2 = w1_bf.shape
    ff = ff2 // 2
    nblk = n_rows // MOE_BLK
    grid_spec = pltpu.PrefetchScalarGridSpec(
        num_scalar_prefetch=2,
        grid=(nblk,),
        in_specs=[pl.BlockSpec((MOE_BLK, d), lambda j, be, nu: (jnp.minimum(j, nu[0] - 1), 0)),
                  pl.BlockSpec((1, d, ff2), lambda j, be, nu: (be[j], 0, 0)),
                  pl.BlockSpec((1, 1, ff2), lambda j, be, nu: (be[j], 0, 0)),
                  pl.BlockSpec((1, ff, d), lambda j, be, nu: (be[j], 0, 0)),
                  pl.BlockSpec((1, 1, d), lambda j, be, nu: (be[j], 0, 0))],
        out_specs=pl.BlockSpec((MOE_BLK, d), lambda j, be, nu: (j, 0)),
    )
    return pl.pallas_call(
        _expert_body,
        grid_spec=grid_spec,
        out_shape=jax.ShapeDtypeStruct((n_rows, d), F32),
        compiler_params=_cparams(("arbitrary",)),
        name="moe_experts",
    )(blk_exp, n_used, xs, w1_bf, b1.reshape(ne, 1, ff2), w2_bf, b2.reshape(ne, 1, d))


def _combine_body(dc_ref, dn_ref, meta_ref, h_ref, gf_ref, lg_ref, lb_ref, y_hbm, o_ref, buf_ref, sem):
    tm = h_ref.shape[0]
    i = pl.program_id(0)
    n = pl.num_programs(0)
    slot = i % 2

    def issue(dref, s):
        def body(r, carry):
            for k in range(TOP_K):
                d = dref[0, 0, r * TOP_K + k]
                pltpu.make_async_copy(y_hbm.at[pl.ds(d, 1)], buf_ref.at[s, k, pl.ds(r, 1)], sem.at[s]).start()
            return carry
        lax.fori_loop(0, tm, body, 0)

    @pl.when(i == 0)
    def _():
        issue(dc_ref, 0)

    @pl.when(i + 1 < n)
    def _():
        issue(dn_ref, 1 - slot)

    for k in range(TOP_K):
        pltpu.make_async_copy(y_hbm.at[pl.ds(0, tm)], buf_ref.at[slot, k], sem.at[slot]).wait()

    meta = meta_ref[...]
    acc = jnp.zeros(h_ref.shape, F32)
    for k in range(TOP_K):
        acc = acc + meta[:, TOP_K + k:TOP_K + k + 1] * buf_ref[slot, k]
    z = DN_ALPHA * h_ref[...] + (1.0 + gf_ref[0]) * acc
    o_ref[...] = _layer_norm(z, lg_ref[...], lb_ref[...])


def moe_combine(y, dest, meta, h, gf, ln_g, ln_b, seq):
    n, d = h.shape
    tm = min(256, seq)
    per_b = seq // tm
    nblk = n // tm
    dest3 = dest.reshape(nblk, 1, tm * TOP_K)
    row = lambda i: (i, 0)
    const = lambda i: (0, 0)
    return pl.pallas_call(
        _combine_body,
        grid=(nblk,),
        in_specs=[pl.BlockSpec((1, 1, tm * TOP_K), lambda i: (i, 0, 0), memory_space=pltpu.SMEM),
                  pl.BlockSpec((1, 1, tm * TOP_K), lambda i: (jnp.minimum(i + 1, nblk - 1), 0, 0),
                               memory_space=pltpu.SMEM),
                  pl.BlockSpec((tm, LANES), row),
                  pl.BlockSpec((tm, d), row),
                  pl.BlockSpec((1, 1, d), lambda i: (i // per_b, 0, 0)),
                  pl.BlockSpec((1, d), const),
                  pl.BlockSpec((1, d), const),
                  pl.BlockSpec(memory_space=pl.ANY)],
        out_specs=pl.BlockSpec((tm, d), row),
        out_shape=jax.ShapeDtypeStruct((n, d), F32),
        scratch_shapes=[pltpu.VMEM((2, TOP_K, tm, d), F32), pltpu.SemaphoreType.DMA((2,))],
        compiler_params=_cparams(("arbitrary",)),
        name="moe_combine",
    )(dest3, dest3, meta, h, gf, ln_g.reshape(1, d), ln_b.reshape(1, d), y)


def moe_layer(h, sc, sh, gf, ln_g, ln_b, router_w, router_b, w1_bf, b1, w2_bf, b2, seq):
    n, d = h.shape
    ne = router_w.shape[1]
    rw_pad = jnp.pad(router_w, ((0, 0), (0, LANES - ne)))
    rb_pad = jnp.pad(router_b, (0, LANES - ne), constant_values=NEG).reshape(1, LANES)
    meta, cnt = moe_router(h, sc, sh, rw_pad, rb_pad, seq)
    idx4 = meta[:, 0:TOP_K].astype(I32)
    rank4 = meta[:, 2 * TOP_K:3 * TOP_K].astype(I32)
    counts = cnt[0, :ne].astype(I32)
    padded = (counts + MOE_BLK - 1) // MOE_BLK * MOE_BLK
    end = jnp.cumsum(padded)
    start = end - padded
    onehot = idx4[:, :, None] == jnp.arange(ne, dtype=I32)[None, None, :]
    dest = jnp.sum(jnp.where(onehot, start[None, None, :], 0), axis=-1) + rank4
    n_rows = -(-(n * TOP_K + ne * (MOE_BLK - 1)) // MOE_BLK) * MOE_BLK
    nblk = n_rows // MOE_BLK
    blk0 = jnp.arange(nblk, dtype=I32) * MOE_BLK
    n_used = (end[-1] // MOE_BLK).astype(I32)
    blk_exp = jnp.minimum(jnp.sum(end[None, :] <= blk0[:, None], axis=-1), ne - 1).astype(I32)
    last_exp = blk_exp[jnp.maximum(n_used - 1, 0)]
    blk_exp = jnp.where(jnp.arange(nblk) < n_used, blk_exp, last_exp)
    n_pad = n_rows - n * TOP_K
    seg_cnt = jnp.concatenate([padded - counts, (n_rows - end[-1])[None]])
    seg_base = jnp.concatenate([start + counts, end[-1:]])
    seg_end = jnp.cumsum(seg_cnt)
    q = jnp.arange(n_pad, dtype=I32)
    seg = jnp.sum(seg_end[None, :] <= q[:, None], axis=-1)
    pad_dest = (seg_base[seg] + q - (seg_end - seg_cnt)[seg]).astype(I32)
    xs = moe_dispatch(h, sc, sh, dest.reshape(-1), pad_dest, n_rows, seq)
    y = moe_experts(xs, blk_exp, n_used.reshape(1), w1_bf, b1, w2_bf, b2)
    return moe_combine(y, dest.reshape(-1), meta, h, gf, ln_g, ln_b, seq)


def _inproj_c_body(x_ref, sc_ref, sh_ref, w_ref, o_ref, acc_ref):
    tm = x_ref.shape[0]
    xb = (x_ref[...] * (1.0 + sc_ref[0]) + sh_ref[0]).astype(BF16)
    acc = jnp.dot(xb, w_ref[...], preferred_element_type=F32)
    ncol = acc.shape[1] // LANES
    for c in range(ncol):
        acc_ref[c] = acc[:, c * LANES:(c + 1) * LANES]
    for t in range(SSM_T):
        for c in range(ncol):
            o_ref[t, :, c * LANES:(c + 1) * LANES] = acc_ref[c, pl.ds(t, tm // SSM_T, stride=SSM_T), :]


def inproj_c(h, sc, sh, w_bf, seq):
    n, d = h.shape
    width = w_bf.shape[1]
    tm = min(512, seq)
    per_b = seq // tm
    return pl.pallas_call(
        _inproj_c_body,
        grid=(n // tm,),
        in_specs=[pl.BlockSpec((tm, d), lambda i: (i, 0)),
                  pl.BlockSpec((1, 1, d), lambda i: (i // per_b, 0, 0)),
                  pl.BlockSpec((1, 1, d), lambda i: (i // per_b, 0, 0)),
                  _resident((d, width), lambda i: (0, 0))],
        out_specs=pl.BlockSpec((SSM_T, tm // SSM_T, width), lambda i: (0, i, 0)),
        out_shape=jax.ShapeDtypeStruct((SSM_T, n // SSM_T, width), F32),
        scratch_shapes=[pltpu.VMEM((width // LANES, tm, LANES), F32)],
        compiler_params=_cparams(("arbitrary",)),
        name="inproj_c",
    )(h, sc, sh, w_bf)


def _s5_body(u_ref, wt_ref, wa_ref, wc_ref, lam_ref, d_ref, o_ref, st_ref, v_ref, xs_ref):
    cb = u_ref.shape[1]
    ns = st_ref.shape[1] // 2

    @pl.when(pl.program_id(2) == 0)
    def _():
        st_ref[...] = jnp.zeros_like(st_ref)

    xcat = jnp.concatenate([u_ref[t].astype(BF16) for t in range(SSM_T)], axis=1)
    v_ref[...] = jnp.dot(xcat, wa_ref[0], preferred_element_type=F32)
    lam = lam_ref[0]
    a_re, a_im = lam[:, :ns], lam[:, ns:]

    def step(r, st):
        xs_ref[pl.ds(r, 1), :] = st
        v = v_ref[pl.ds(r, 1), :]
        s_re, s_im = st[:, :ns], st[:, ns:]
        n_re = a_re * s_re - a_im * s_im + v[:, :ns]
        n_im = a_re * s_im + a_im * s_re + v[:, ns:]
        return jnp.concatenate([n_re, n_im], axis=1)

    st_ref[...] = lax.fori_loop(0, cb, step, st_ref[...])
    y = jnp.dot(xcat, wt_ref[0], preferred_element_type=F32)
    y = y + jnp.dot(xs_ref[...].astype(BF16), wc_ref[0], preferred_element_type=F32)
    dvec = d_ref[0]
    for t in range(SSM_T):
        yt = y[:, t * LANES:(t + 1) * LANES] + dvec * u_ref[t]
        gel = 0.5 * yt * (1.0 + jnp.tanh(0.7978845608028654 * (yt + 0.044715 * yt * yt * yt)))
        o_ref[t] = gel.astype(o_ref.dtype)


def s5_mixer(u2, wt, wa, wc, lam_t, dpack, bsz, seq):
    _, nchunk, width = u2.shape
    npack = width // LANES
    per_b = seq // SSM_T
    cb = min(256, per_b)
    nblk = per_b // cb
    ns2 = wa.shape[2]
    kdim = SSM_T * LANES
    pk = lambda j, b, i: (j, 0, 0)
    return pl.pallas_call(
        _s5_body,
        grid=(npack, bsz, nblk),
        in_specs=[pl.BlockSpec((SSM_T, cb, LANES), lambda j, b, i: (0, b * nblk + i, j)),
                  _resident((1, kdim, kdim), pk),
                  _resident((1, kdim, ns2), pk),
                  _resident((1, ns2, kdim), pk),
                  pl.BlockSpec((1, 1, ns2), pk),
                  pl.BlockSpec((1, 1, LANES), pk)],
        out_specs=pl.BlockSpec((SSM_T, cb, LANES), lambda j, b, i: (0, b * nblk + i, j)),
        out_shape=jax.ShapeDtypeStruct((SSM_T, nchunk, width), F32),
        scratch_shapes=[pltpu.VMEM((1, ns2), F32), pltpu.VMEM((cb, ns2), F32), pltpu.VMEM((cb, ns2), F32)],
        compiler_params=_cparams(("arbitrary", "arbitrary", "arbitrary")),
        name="s5_mixer",
    )(u2, wt, wa, wc, lam_t, dpack)


def s5_operators(a_re, a_im, log_dt, b_re, b_im, c_re, c_im, dskip):
    g, p = a_re.shape
    cg = b_re.shape[2]
    npack = g // SSM_PACK
    t = SSM_T
    lam = lax.complex(a_re, a_im)
    dt = jnp.exp(log_dt)[:, None]
    lam_bar = jnp.exp(lam * dt)
    b_bar = ((lam_bar - 1.0) / lam)[..., None] * lax.complex(b_re, b_im)
    c_mat = lax.complex(c_re, c_im)
    kpow = jnp.arange(t + 1, dtype=F32)[:, None, None]
    pows = jnp.exp((lam * dt)[None] * kpow)
    eye = jnp.eye(SSM_PACK, dtype=F32)
    kern = jnp.real(jnp.einsum('gcp,kgp,gpd->kgcd', c_mat, pows[:t], b_bar, precision=HIGHEST))
    s_idx = jnp.arange(t)[:, None]
    t_idx = jnp.arange(t)[None, :]
    lag = jnp.clip(t_idx - s_idx, 0, t - 1)
    kst = jnp.where((t_idx >= s_idx)[:, :, None, None, None], kern[lag], 0.0)
    kst = kst.reshape(t, t, npack, SSM_PACK, cg, cg)
    wt = jnp.einsum('stjgcd,gh->jsgdthc', kst, eye).reshape(npack, t * LANES, t * LANES)
    wa_c = pows[t - 1 - jnp.arange(t)][:, :, :, None] * b_bar[None]
    wa_ri = jnp.stack([jnp.real(wa_c), jnp.imag(wa_c)], axis=0)
    wa_ri = wa_ri.reshape(2, t, npack, SSM_PACK, p, cg)
    wa = jnp.einsum('rsjgpd,gh->jsgdrhp', wa_ri, eye).reshape(npack, t * LANES, 2 * SSM_PACK * p)
    wc_c = c_mat[None] * pows[1:t + 1][:, :, None, :]
    wc_ri = jnp.stack([jnp.real(wc_c), -jnp.imag(wc_c)], axis=0)
    wc_ri = wc_ri.reshape(2, t, npack, SSM_PACK, cg, p)
    wc = jnp.einsum('rtjgcp,gh->jrgpthc', wc_ri, eye).reshape(npack, 2 * SSM_PACK * p, t * LANES)
    lam_t = pows[t].reshape(npack, 1, SSM_PACK * p)
    lam_t = jnp.concatenate([jnp.real(lam_t), jnp.imag(lam_t)], axis=-1)
    dpack = dskip.reshape(npack, 1, LANES)
    return wt.astype(BF16), wa.astype(BF16), wc.astype(BF16), lam_t, dpack


def _glu_out_body(y_ref, wg_ref, bg_ref, wo_ref, h_ref, gm_ref, lg_ref, lb_ref, o_ref, ytm_ref):
    cb = y_ref.shape[1]
    ncol = y_ref.shape[2] // LANES
    for t in range(SSM_T):
        for c in range(ncol):
            ytm_ref[c, pl.ds(t, cb, stride=SSM_T), :] = y_ref[t, :, c * LANES:(c + 1) * LANES]
    y = jnp.concatenate([ytm_ref[c] for c in range(ncol)], axis=1)
    y_bf = y.astype(BF16)
    gate = jnp.dot(y_bf, wg_ref[...], preferred_element_type=F32) + bg_ref[...]
    z = (y * jax.nn.sigmoid(gate)).astype(BF16)
    out = jnp.dot(z, wo_ref[...], preferred_element_type=F32)
    zz = DN_ALPHA * h_ref[...] + (1.0 + gm_ref[0]) * out
    o_ref[...] = _layer_norm(zz, lg_ref[...], lb_ref[...])


def glu_out(y2, wg_bf, bg, wo_bf, h, gm, ln_g, ln_b, seq):
    n, d = h.shape
    width = y2.shape[2]
    tm = min(256, seq)
    cb = tm // SSM_T
    per_b = seq // tm
    row = lambda i: (i, 0)
    const = lambda i: (0, 0)
    return pl.pallas_call(
        _glu_out_body,
        grid=(n // tm,),
        in_specs=[pl.BlockSpec((SSM_T, cb, width), lambda i: (0, i, 0)),
                  _resident((width, width), const),
                  pl.BlockSpec((1, width), const),
                  _resident((width, d), const),
                  pl.BlockSpec((tm, d), row),
                  pl.BlockSpec((1, 1, d), lambda i: (i // per_b, 0, 0)),
                  pl.BlockSpec((1, d), const),
                  pl.BlockSpec((1, d), const)],
        out_specs=pl.BlockSpec((tm, d), row),
        out_shape=jax.ShapeDtypeStruct((n, d), F32),
        scratch_shapes=[pltpu.VMEM((width // LANES, tm, LANES), F32)],
        compiler_params=_cparams(("arbitrary",)),
        name="glu_out",
    )(y2, wg_bf, bg.reshape(1, width), wo_bf, h, gm, ln_g.reshape(1, d), ln_b.reshape(1, d))


def kernel(x, c, positions, ada_w, ada_b, ln_g, ln_b, ab_in_w, ab_in_b, ab_sinks, ab_gnorm_w, ab_out_w, ab_out_b, hgrn_lb_logits, c_in_w, c_A_re, c_A_im, c_log_dt, c_B_re, c_B_im, c_C_re, c_C_im, c_D, c_glu_w, c_glu_b, c_out_w, router_w, router_b, exp_w1, exp_b1, exp_w2, exp_b2):
    bsz, seq, d = x.shape
    n = bsz * seq
    depth = ada_w.shape[0]
    mod = ada_mod(c, ada_w, ada_b)
    lb_all = jnp.cumsum(jax.nn.softmax(hgrn_lb_logits.astype(F32), axis=0), axis=0)
    half = A_HEAD_DIM // 2
    inv_freq = ROPE_THETA ** (-jnp.arange(half, dtype=F32) / half)
    invf = jnp.tile(inv_freq, LANES // half).reshape(1, LANES)
    pos_col = positions.reshape(n, 1)
    h = x.reshape(n, d)
    for layer in range(depth):
        m6 = mod[layer].reshape(bsz, 6, 1, d)
        sh_m, sc_m, g_m, sh_f, sc_f, g_f = [m6[:, k] for k in range(6)]
        i = layer // 2
        if layer % 2 == 0:
            proj = inproj_ab(h, sc_m, sh_m, ab_in_w[i].astype(BF16), ab_in_b[i], seq)
            oa = swa_attention(proj, pos_col, invf, ab_sinks[i], bsz, seq)
            ob = hgrn2(proj, lb_all[layer], ab_gnorm_w[i], bsz, seq)
            h = outproj_ab(oa, ob, ab_out_w[i].astype(BF16), ab_out_b[i], h, g_m,
                           ln_g[layer, 0], ln_b[layer, 0], seq)
        else:
            wt, wa, wc, lam_t, dpack = s5_operators(c_A_re[i], c_A_im[i], c_log_dt[i], c_B_re[i], c_B_im[i],
                                                    c_C_re[i], c_C_im[i], c_D[i])
            u2 = inproj_c(h, sc_m, sh_m, c_in_w[i].astype(BF16), seq)
            y2 = s5_mixer(u2, wt, wa, wc, lam_t, dpack, bsz, seq)
            h = glu_out(y2, c_glu_w[i].astype(BF16), c_glu_b[i], c_out_w[i].astype(BF16), h, g_m,
                        ln_g[layer, 0], ln_b[layer, 0], seq)
        h = moe_layer(h, sc_f, sh_f, g_f, ln_g[layer, 1], ln_b[layer, 1], router_w[layer], router_b[layer],
                      exp_w1[layer].astype(BF16), exp_b1[layer], exp_w2[layer].astype(BF16), exp_b2[layer], seq)
    return h.reshape(bsz, seq, d)
```

```python
import functools

import jax
import jax.numpy as jnp
from jax import lax
from jax.experimental import pallas as pl
from jax.experimental.pallas import tpu as pltpu

F32 = jnp.float32
BF16 = jnp.bfloat16
I32 = jnp.int32
HIGHEST = lax.Precision.HIGHEST

DEPTH = 2
A_HEADS = 16
A_KV_HEADS = 2
A_HEAD_DIM = 64
WINDOW = 128
ROPE_THETA = 10000.0
B_HEADS = 8
B_DIM = 128
GLA_CHUNK = 64
C_GROUP = 16
C_STATE = 64
N_EXPERTS = 32
TOP_K = 4
SWIGLU_LIMIT = 7.0
SWIGLU_ALPHA = 1.702
DN_ALPHA = (2 * DEPTH) ** 0.25
LN_EPS = 1e-5
RMS_EPS = 1e-6

LANES = 128
VMEM_LIMIT = 56 * 1024 * 1024

ATT_BLK = 128
HGRN_T = 512
MOE_BLK = 256
SSM_T = 16
SSM_PACK = LANES // C_GROUP
NEG = -1e30


def _cparams(sem):
    return pltpu.CompilerParams(dimension_semantics=sem, vmem_limit_bytes=VMEM_LIMIT)


def _resident(block_shape, index_map):
    return pl.BlockSpec(block_shape, index_map, pipeline_mode=pl.Buffered(1))


def _layer_norm(z, g, b):
    mu = jnp.mean(z, axis=-1, keepdims=True)
    zc = z - mu
    var = jnp.mean(zc * zc, axis=-1, keepdims=True)
    return zc * lax.rsqrt(var + LN_EPS) * g + b


def _ada_body(ct_ref, w_ref, b_ref, o_ref):
    ct = ct_ref[...]
    cond = ct * jax.nn.sigmoid(ct)
    w = w_ref[0]
    rows = [jnp.sum(w * cond[:, b:b + 1], axis=0, keepdims=True) for b in range(ct.shape[1])]
    o_ref[0] = jnp.concatenate(rows, axis=0) + b_ref[0]


def ada_mod(c, ada_w, ada_b):
    depth, d, n6 = ada_w.shape
    bsz = c.shape[0]
    tn = 1536
    return pl.pallas_call(
        _ada_body,
        grid=(depth, n6 // tn),
        in_specs=[pl.BlockSpec((d, bsz), lambda l, j: (0, 0)),
                  pl.BlockSpec((1, d, tn), lambda l, j: (l, 0, j)),
                  pl.BlockSpec((1, 1, tn), lambda l, j: (l, 0, j))],
        out_specs=pl.BlockSpec((1, bsz, tn), lambda l, j: (l, 0, j)),
        out_shape=jax.ShapeDtypeStruct((depth, bsz, n6), F32),
        compiler_params=_cparams(("arbitrary", "arbitrary")),
        name="ada_mod",
    )(c.T, ada_w, ada_b.reshape(depth, 1, n6))


def _inproj_ab_body(x_ref, sc_ref, sh_ref, w_ref, b_ref, o_ref, xb_ref):
    @pl.when(pl.program_id(1) == 0)
    def _():
        xb_ref[...] = (x_ref[...] * (1.0 + sc_ref[0]) + sh_ref[0]).astype(BF16)

    o_ref[...] = jnp.dot(xb_ref[...], w_ref[...], preferred_element_type=F32) + b_ref[...]


def inproj_ab(h, sc, sh, w_bf, b, seq):
    n, d = h.shape
    nc = w_bf.shape[1]
    tm = min(1024, seq)
    tn = 768
    per_b = seq // tm
    return pl.pallas_call(
        _inproj_ab_body,
        grid=(n // tm, nc // tn),
        in_specs=[pl.BlockSpec((tm, d), lambda i, j: (i, 0)),
                  pl.BlockSpec((1, 1, d), lambda i, j: (i // per_b, 0, 0)),
                  pl.BlockSpec((1, 1, d), lambda i, j: (i // per_b, 0, 0)),
                  pl.BlockSpec((d, tn), lambda i, j: (0, j)),
                  pl.BlockSpec((1, tn), lambda i, j: (0, j))],
        out_specs=pl.BlockSpec((tm, tn), lambda i, j: (i, j)),
        out_shape=jax.ShapeDtypeStruct((n, nc), F32),
        scratch_shapes=[pltpu.VMEM((tm, d), BF16)],
        compiler_params=_cparams(("arbitrary", "arbitrary")),
        name="inproj_ab",
    )(h, sc, sh, w_bf, b.reshape(1, nc))


def _attn_body(sink_ref, q_ref, kc_ref, kp_ref, vc_ref, vp_ref, pc_ref, pp_ref, invf_ref, o_ref):
    blk = ATT_BLK
    half = A_HEAD_DIM // 2
    grp = A_HEADS // A_KV_HEADS
    i = pl.program_id(1)
    lane = lax.broadcasted_iota(I32, (blk, LANES), 1)
    first_half = (lane % A_HEAD_DIM) < half
    low_head = lane < A_HEAD_DIM
    invf = invf_ref[...]

    def cos_sin(pos_col):
        ang = pos_col.astype(F32) * invf
        s = jnp.sin(ang)
        return jnp.cos(ang), jnp.where(first_half, -s, s)

    def rope(x, cs):
        partner = jnp.where(first_half, pltpu.roll(x, LANES - half, 1), pltpu.roll(x, half, 1))
        return x * cs[0] + partner * cs[1]

    cs_c = cos_sin(pc_ref[...])
    cs_p = cos_sin(pp_ref[...])
    k2 = jnp.concatenate([rope(kp_ref[...], cs_p), rope(kc_ref[...], cs_c)], axis=0)
    v2 = jnp.concatenate([vp_ref[...], vc_ref[...]], axis=0)
    lane2 = lax.broadcasted_iota(I32, (2 * blk, LANES), 1)
    k2r = pltpu.roll(k2, A_HEAD_DIM, 1)
    v2r = pltpu.roll(v2, A_HEAD_DIM, 1)

    rows = lax.broadcasted_iota(I32, (grp * blk, 2 * blk), 0) % blk
    cols = lax.broadcasted_iota(I32, (grp * blk, 2 * blk), 1)
    valid = (cols > rows) & (cols <= rows + WINDOW) & ((cols >= blk) | (i > 0))

    q_chunks = [rope(q_ref[:, c * LANES:(c + 1) * LANES], cs_c) * (A_HEAD_DIM ** -0.5)
                for c in range(A_HEADS * A_HEAD_DIM // LANES)]

    for g in range(A_KV_HEADS):
        own = (lane2 < A_HEAD_DIM) if g == 0 else (lane2 >= A_HEAD_DIM)
        kg = jnp.where(own, k2, k2r).astype(BF16)
        vg = jnp.where(own, v2, v2r).astype(BF16)
        pieces = []
        sinks = []
        for r in range(grp):
            hd = g * grp + r
            qc = q_chunks[hd // 2]
            keep = low_head if hd % 2 == 0 else jnp.logical_not(low_head)
            pieces.append(jnp.where(keep, qc, 0.0).astype(BF16))
            sinks.append(jnp.full((blk, 1), sink_ref[hd], F32))
        qs = jnp.concatenate(pieces, axis=0)
        sink = jnp.concatenate(sinks, axis=0)
        s = lax.dot_general(qs, kg, (((1,), (1,)), ((), ())), preferred_element_type=F32)
        s = jnp.where(valid, s, NEG)
        m = jnp.maximum(jnp.max(s, axis=-1, keepdims=True), sink)
        p = jnp.exp(s - m)
        den = jnp.sum(p, axis=-1, keepdims=True) + jnp.exp(sink - m)
        o = jnp.dot(p.astype(BF16), vg, preferred_element_type=F32) / den
        for r2 in range(grp // 2):
            c = (g * grp) // 2 + r2
            ev = o[(2 * r2) * blk:(2 * r2 + 1) * blk]
            od = o[(2 * r2 + 1) * blk:(2 * r2 + 2) * blk]
            o_ref[:, c * LANES:(c + 1) * LANES] = jnp.where(low_head, ev, od).astype(o_ref.dtype)


def swa_attention(proj, pos_col, invf, sinks, bsz, seq):
    n = proj.shape[0]
    blk = ATT_BLK
    nb = seq // blk
    aq = A_HEADS * A_HEAD_DIM
    kcol = aq // LANES
    vcol = kcol + 1

    def cur(b, i, s):
        return b * nb + i

    def prev(b, i, s):
        return b * nb + jnp.maximum(i - 1, 0)

    grid_spec = pltpu.PrefetchScalarGridSpec(
        num_scalar_prefetch=1,
        grid=(bsz, nb),
        in_specs=[pl.BlockSpec((blk, aq), lambda b, i, s: (cur(b, i, s), 0)),
                  pl.BlockSpec((blk, LANES), lambda b, i, s: (cur(b, i, s), kcol)),
                  pl.BlockSpec((blk, LANES), lambda b, i, s: (prev(b, i, s), kcol)),
                  pl.BlockSpec((blk, LANES), lambda b, i, s: (cur(b, i, s), vcol)),
                  pl.BlockSpec((blk, LANES), lambda b, i, s: (prev(b, i, s), vcol)),
                  pl.BlockSpec((blk, 1), lambda b, i, s: (cur(b, i, s), 0)),
                  pl.BlockSpec((blk, 1), lambda b, i, s: (prev(b, i, s), 0)),
                  pl.BlockSpec((1, LANES), lambda b, i, s: (0, 0))],
        out_specs=pl.BlockSpec((blk, aq), lambda b, i, s: (cur(b, i, s), 0)),
    )
    return pl.pallas_call(
        _attn_body,
        grid_spec=grid_spec,
        out_shape=jax.ShapeDtypeStruct((n, aq), BF16),
        compiler_params=_cparams(("arbitrary", "arbitrary")),
        name="swa_attention",
    )(sinks, proj, proj, proj, proj, proj, pos_col, pos_col, invf)


def _hgrn_body(q_ref, f_ref, i_ref, g_ref, lb_ref, gw_ref, o_ref, st_ref):
    L = GLA_CHUNK

    @pl.when(pl.program_id(2) == 0)
    def _():
        st_ref[...] = jnp.zeros_like(st_ref)

    T = q_ref.shape[0]
    nch = T // L
    lb = lb_ref[0]
    gw = gw_ref[0]
    nt = (((1,), (1,)), ((), ()))

    f = lb + (1.0 - lb) * jax.nn.sigmoid(f_ref[...])
    kk = 1.0 - f
    qb = q_ref[...]
    q = qb * jax.nn.sigmoid(qb)
    v = i_ref[...]
    pos = lax.broadcasted_iota(I32, (T, B_DIM), 0) % L
    b = jnp.log(f)
    sh = 1
    while sh < L:
        b = b + jnp.where(pos >= sh, pltpu.roll(b, sh, 0), 0.0)
        sh *= 2

    def per_chunk(row):
        return jnp.concatenate(
            [jnp.broadcast_to(b[c * L + row:c * L + row + 1], (L, B_DIM)) for c in range(nch)], axis=0)

    b_mid = per_chunk(L // 2)
    b_last = per_chunk(L - 1)
    e_mid = jnp.exp(b - b_mid)
    qa = (q * e_mid).astype(BF16)
    ka = (kk * jnp.exp(b_mid - b)).astype(BF16)
    qe = (q * jnp.exp(b)).astype(BF16)
    kd = (kk * jnp.exp(b_last - b)).astype(BF16)
    att = lax.dot_general(qa, ka, nt, preferred_element_type=F32)
    r = lax.broadcasted_iota(I32, (T, T), 0)
    c = lax.broadcasted_iota(I32, (T, T), 1)
    att = jnp.where((r >= c) & (r // L == c // L), att, 0.0)
    o_intra = jnp.dot(att.astype(BF16), v.astype(BF16), preferred_element_type=F32)
    vt = v.T.astype(BF16)
    st = st_ref[...]
    o_inter = []
    for ci in range(nch):
        sl = slice(ci * L, (ci + 1) * L)
        o_inter.append(lax.dot_general(qe[sl], st.astype(BF16), nt, preferred_element_type=F32))
        u_t = jnp.dot(vt[:, sl], kd[sl], preferred_element_type=F32)
        st = jnp.exp(b[ci * L + L - 1:ci * L + L]) * st + u_t
    st_ref[...] = st
    o = o_intra + jnp.concatenate(o_inter, axis=0)
    o = o * lax.rsqrt(jnp.mean(o * o, axis=-1, keepdims=True) + RMS_EPS) * gw
    gb = g_ref[...]
    o_ref[...] = (o * (gb * jax.nn.sigmoid(gb))).astype(o_ref.dtype)


def hgrn2(proj, lb, gnorm_w, bsz, seq):
    n = proj.shape[0]
    t = min(HGRN_T, seq)
    nt = seq // t
    base = (A_HEADS + 2 * A_KV_HEADS) * A_HEAD_DIM // LANES
    h = B_HEADS

    def col(k):
        return lambda b, hd, i: (b * nt + i, base + k * h + hd)

    return pl.pallas_call(
        _hgrn_body,
        grid=(bsz, h, nt),
        in_specs=[pl.BlockSpec((t, LANES), col(0)),
                  pl.BlockSpec((t, LANES), col(1)),
                  pl.BlockSpec((t, LANES), col(2)),
                  pl.BlockSpec((t, LANES), col(3)),
                  pl.BlockSpec((1, 1, B_DIM), lambda b, hd, i: (hd, 0, 0)),
                  pl.BlockSpec((1, 1, B_DIM), lambda b, hd, i: (hd, 0, 0))],
        out_specs=pl.BlockSpec((t, LANES), lambda b, hd, i: (b * nt + i, hd)),
        out_shape=jax.ShapeDtypeStruct((n, h * B_DIM), BF16),
        scratch_shapes=[pltpu.VMEM((B_DIM, B_DIM), F32)],
        compiler_params=_cparams(("arbitrary", "arbitrary", "arbitrary")),
        name="hgrn2",
    )(proj, proj, proj, proj, lb.reshape(h, 1, B_DIM), gnorm_w.reshape(h, 1, B_DIM))


def _outproj_ab_body(oa_ref, ob_ref, w_ref, b_ref, h_ref, gm_ref, lg_ref, lb_ref, o_ref):
    ka = oa_ref.shape[1]
    y = jnp.dot(oa_ref[...], w_ref[:ka, :], preferred_element_type=F32)
    y = y + jnp.dot(ob_ref[...], w_ref[ka:, :], preferred_element_type=F32) + b_ref[...]
    z = DN_ALPHA * h_ref[...] + (1.0 + gm_ref[0]) * y
    o_ref[...] = _layer_norm(z, lg_ref[...], lb_ref[...])


def outproj_ab(oa, ob, w_bf, b, h, gm, ln_g, ln_b, seq):
    n, d = h.shape
    tm = min(512, seq)
    per_b = seq // tm
    ka, kb = oa.shape[1], ob.shape[1]
    row = lambda i: (i, 0)
    const = lambda i: (0, 0)
    return pl.pallas_call(
        _outproj_ab_body,
        grid=(n // tm,),
        in_specs=[pl.BlockSpec((tm, ka), row),
                  pl.BlockSpec((tm, kb), row),
                  _resident((ka + kb, d), const),
                  pl.BlockSpec((1, d), const),
                  pl.BlockSpec((tm, d), row),
                  pl.BlockSpec((1, 1, d), lambda i: (i // per_b, 0, 0)),
                  pl.BlockSpec((1, d), const),
                  pl.BlockSpec((1, d), const)],
        out_specs=pl.BlockSpec((tm, d), row),
        out_shape=jax.ShapeDtypeStruct((n, d), F32),
        compiler_params=_cparams(("arbitrary",)),
        name="outproj_ab",
    )(oa, ob, w_bf, b.reshape(1, d), h, gm, ln_g.reshape(1, d), ln_b.reshape(1, d))


def _router_body(h_ref, sc_ref, sh_ref, rw_ref, rb_ref, meta_ref, cnt_ref, carry_ref):
    tm = h_ref.shape[0]

    @pl.when(pl.program_id(0) == 0)
    def _():
        carry_ref[...] = jnp.zeros_like(carry_ref)

    xin = h_ref[...] * (1.0 + sc_ref[0]) + sh_ref[0]
    logits = jnp.dot(xin, rw_ref[...], precision=HIGHEST, preferred_element_type=F32) + rb_ref[...]
    lane = lax.broadcasted_iota(I32, (tm, LANES), 1)
    l = logits
    vals, idxs = [], []
    sel = jnp.zeros((tm, LANES), F32)
    for _ in range(TOP_K):
        m = jnp.max(l, axis=-1, keepdims=True)
        idx = jnp.min(jnp.where(l == m, lane, LANES), axis=-1, keepdims=True)
        hit = lane == idx
        vals.append(m)
        idxs.append(idx)
        sel = jnp.where(hit, 1.0, sel)
        l = jnp.where(hit, -jnp.inf, l)
    es = [jnp.exp(v - vals[0]) for v in vals]
    den = es[0] + es[1] + es[2] + es[3]
    r = lax.broadcasted_iota(I32, (tm, tm), 0)
    c = lax.broadcasted_iota(I32, (tm, tm), 1)
    tril = (r >= c).astype(BF16)
    incl = jnp.dot(tril, sel.astype(BF16), preferred_element_type=F32)
    carry = carry_ref[...]
    excl = incl - sel + carry
    meta = jnp.zeros((tm, LANES), F32)
    for k in range(TOP_K):
        rank = jnp.sum(jnp.where(lane == idxs[k], excl, 0.0), axis=-1, keepdims=True)
        meta = jnp.where(lane == k, idxs[k].astype(F32), meta)
        meta = jnp.where(lane == TOP_K + k, es[k] / den, meta)
        meta = jnp.where(lane == 2 * TOP_K + k, rank, meta)
    meta_ref[...] = meta
    carry = carry + incl[tm - 1:tm, :]
    carry_ref[...] = carry
    cnt_ref[...] = jnp.broadcast_to(carry, cnt_ref.shape)


def moe_router(h, sc, sh, rw_pad, rb_pad, seq):
    n, d = h.shape
    tm = min(512, seq)
    per_b = seq // tm
    return pl.pallas_call(
        _router_body,
        grid=(n // tm,),
        in_specs=[pl.BlockSpec((tm, d), lambda i: (i, 0)),
                  pl.BlockSpec((1, 1, d), lambda i: (i // per_b, 0, 0)),
                  pl.BlockSpec((1, 1, d), lambda i: (i // per_b, 0, 0)),
                  pl.BlockSpec((d, LANES), lambda i: (0, 0)),
                  pl.BlockSpec((1, LANES), lambda i: (0, 0))],
        out_specs=[pl.BlockSpec((tm, LANES), lambda i: (i, 0)),
                   pl.BlockSpec((8, LANES), lambda i: (0, 0))],
        out_shape=[jax.ShapeDtypeStruct((n, LANES), F32),
                   jax.ShapeDtypeStruct((8, LANES), F32)],
        scratch_shapes=[pltpu.VMEM((1, LANES), F32)],
        compiler_params=_cparams(("arbitrary",)),
        name="moe_router",
    )(h, sc, sh, rw_pad, rb_pad)


def _dispatch_body(pad_ref, dest_ref, h_ref, sc_ref, sh_ref, xs_hbm, buf_ref, zero_ref, sem, zsem):
    tm, d = h_ref.shape
    rt = d // LANES
    i = pl.program_id(0)
    last = pl.num_programs(0) - 1
    slot = i % 2
    n_pad = pad_ref.shape[0]

    def wait_slot(s):
        for _ in range(TOP_K):
            pltpu.make_async_copy(buf_ref.at[s], xs_hbm.at[pl.ds(0, tm * rt)], sem.at[s]).wait()

    @pl.when(i == 0)
    def _():
        zero_ref[...] = jnp.zeros_like(zero_ref)

        def zfill(q, carry):
            dst = pl.multiple_of(pad_ref[q] * rt, rt)
            pltpu.make_async_copy(zero_ref, xs_hbm.at[pl.ds(dst, rt)], zsem.at[0]).start()
            return carry

        lax.fori_loop(0, n_pad, zfill, 0)

    @pl.when(i >= 2)
    def _():
        wait_slot(slot)

    xin = h_ref[...] * (1.0 + sc_ref[0]) + sh_ref[0]
    for s in range(rt):
        buf_ref[slot, pl.ds(s, tm, stride=rt), :] = xin[:, s * LANES:(s + 1) * LANES]

    def issue(r, carry):
        src = buf_ref.at[slot, pl.ds(pl.multiple_of(r * rt, rt), rt)]
        for k in range(TOP_K):
            dst = pl.multiple_of(dest_ref[0, 0, r * TOP_K + k] * rt, rt)
            pltpu.make_async_copy(src, xs_hbm.at[pl.ds(dst, rt)], sem.at[slot]).start()
        return carry

    lax.fori_loop(0, tm, issue, 0)

    @pl.when(i == last)
    def _():
        wait_slot(slot)

        @pl.when(i >= 1)
        def _():
            wait_slot(1 - slot)

        for _ in range(n_pad // tm):
            pltpu.make_async_copy(buf_ref.at[0], xs_hbm.at[pl.ds(0, tm * rt)], zsem.at[0]).wait()


def moe_dispatch(h, sc, sh, dest, pad_dest, n_rows, seq):
    n, d = h.shape
    tm = min(256, seq)
    per_b = seq // tm
    rt = d // LANES
    assert pad_dest.shape[0] % tm == 0
    grid_spec = pltpu.PrefetchScalarGridSpec(
        num_scalar_prefetch=1,
        grid=(n // tm,),
        in_specs=[pl.BlockSpec((1, 1, tm * TOP_K), lambda i, p: (i, 0, 0), memory_space=pltpu.SMEM),
                  pl.BlockSpec((tm, d), lambda i, p: (i, 0)),
                  pl.BlockSpec((1, 1, d), lambda i, p: (i // per_b, 0, 0)),
                  pl.BlockSpec((1, 1, d), lambda i, p: (i // per_b, 0, 0))],
        out_specs=pl.BlockSpec(memory_space=pl.ANY),
        scratch_shapes=[pltpu.VMEM((2, tm * rt, LANES), F32), pltpu.VMEM((rt, LANES), F32),
                        pltpu.SemaphoreType.DMA((2,)), pltpu.SemaphoreType.DMA((1,))],
    )
    return pl.pallas_call(
        _dispatch_body,
        grid_spec=grid_spec,
        out_shape=jax.ShapeDtypeStruct((n_rows * rt, LANES), F32),
        compiler_params=_cparams(("arbitrary",)),
        name="moe_dispatch",
    )(pad_dest, dest.reshape(n // tm, 1, tm * TOP_K), h, sc, sh)


def _expert_body(be_ref, nu_ref, xs_ref, w1_ref, b1_ref, w2_ref, b2_ref, y_ref):
    j = pl.program_id(0)
    ff = w2_ref.shape[1]
    rt = w1_ref.shape[1] // LANES
    blk = xs_ref.shape[0] // rt

    @pl.when(j < nu_ref[0])
    def _():
        x = jnp.concatenate([xs_ref[pl.ds(s, blk, stride=rt), :].astype(BF16) for s in range(rt)], axis=1)
        hid = jnp.dot(x, w1_ref[0], preferred_element_type=F32) + b1_ref[0]
        gate = jnp.minimum(hid[:, :ff], SWIGLU_LIMIT)
        lin = jnp.clip(hid[:, ff:], -SWIGLU_LIMIT, SWIGLU_LIMIT)
        act = gate * jax.nn.sigmoid(SWIGLU_ALPHA * gate) * (lin + 1.0)
        y = jnp.dot(act.astype(BF16), w2_ref[0], preferred_element_type=F32) + b2_ref[0]
        for s in range(rt):
            y_ref[pl.ds(s, blk, stride=rt), :] = y[:, s * LANES:(s + 1) * LANES]

    @pl.when(j >= nu_ref[0])
    def _():
        y_ref[...] = jnp.zeros_like(y_ref)


def moe_experts(xs, blk_exp, n_used, w1_bf, b1, w2_bf, b2):
    ne, d, ff2 = w1_bf.shape
    rt = d // LANES
    n_rows = xs.shape[0] // rt
    ff = ff2 // 2
    nblk = n_rows // MOE_BLK
    grid_spec = pltpu.PrefetchScalarGridSpec(
        num_scalar_prefetch=2,
        grid=(nblk,),
        in_specs=[pl.BlockSpec((MOE_BLK * rt, LANES), lambda j, be, nu: (jnp.minimum(j, nu[0] - 1), 0)),
                  pl.BlockSpec((1, d, ff2), lambda j, be, nu: (be[j], 0, 0)),
                  pl.BlockSpec((1, 1, ff2), lambda j, be, nu: (be[j], 0, 0)),
                  pl.BlockSpec((1, ff, d), lambda j, be, nu: (be[j], 0, 0)),
                  pl.BlockSpec((1, 1, d), lambda j, be, nu: (be[j], 0, 0))],
        out_specs=pl.BlockSpec((MOE_BLK * rt, LANES), lambda j, be, nu: (j, 0)),
    )
    return pl.pallas_call(
        _expert_body,
        grid_spec=grid_spec,
        out_shape=jax.ShapeDtypeStruct((n_rows * rt, LANES), F32),
        compiler_params=_cparams(("arbitrary",)),
        name="moe_experts",
    )(blk_exp, n_used, xs, w1_bf, b1.reshape(ne, 1, ff2), w2_bf, b2.reshape(ne, 1, d))


def _combine_body(dc_ref, dn_ref, meta_ref, h_ref, gf_ref, lg_ref, lb_ref, y_hbm, o_ref, buf_ref, sem):
    tm, d = h_ref.shape
    rt = d // LANES
    i = pl.program_id(0)
    n = pl.num_programs(0)
    slot = i % 2

    def issue(dref, s):
        def body(r, carry):
            dst_row = pl.multiple_of(r * rt, rt)
            for k in range(TOP_K):
                src = pl.multiple_of(dref[0, 0, r * TOP_K + k] * rt, rt)
                pltpu.make_async_copy(y_hbm.at[pl.ds(src, rt)], buf_ref.at[s, k, pl.ds(dst_row, rt)],
                                      sem.at[s]).start()
            return carry
        lax.fori_loop(0, tm, body, 0)

    @pl.when(i == 0)
    def _():
        issue(dc_ref, 0)

    @pl.when(i + 1 < n)
    def _():
        issue(dn_ref, 1 - slot)

    for k in range(TOP_K):
        pltpu.make_async_copy(y_hbm.at[pl.ds(0, tm * rt)], buf_ref.at[slot, k], sem.at[slot]).wait()

    meta = meta_ref[...]
    gates = [meta[:, TOP_K + k:TOP_K + k + 1] for k in range(TOP_K)]
    cols = []
    for s in range(rt):
        a = gates[0] * buf_ref[slot, 0, pl.ds(s, tm, stride=rt), :]
        for k in range(1, TOP_K):
            a = a + gates[k] * buf_ref[slot, k, pl.ds(s, tm, stride=rt), :]
        cols.append(a)
    acc = jnp.concatenate(cols, axis=1)
    z = DN_ALPHA * h_ref[...] + (1.0 + gf_ref[0]) * acc
    o_ref[...] = _layer_norm(z, lg_ref[...], lb_ref[...])


def moe_combine(y, dest, meta, h, gf, ln_g, ln_b, seq):
    n, d = h.shape
    tm = min(256, seq)
    per_b = seq // tm
    nblk = n // tm
    dest3 = dest.reshape(nblk, 1, tm * TOP_K)
    row = lambda i: (i, 0)
    const = lambda i: (0, 0)
    return pl.pallas_call(
        _combine_body,
        grid=(nblk,),
        in_specs=[pl.BlockSpec((1, 1, tm * TOP_K), lambda i: (i, 0, 0), memory_space=pltpu.SMEM),
                  pl.BlockSpec((1, 1, tm * TOP_K), lambda i: (jnp.minimum(i + 1, nblk - 1), 0, 0),
                               memory_space=pltpu.SMEM),
                  pl.BlockSpec((tm, LANES), row),
                  pl.BlockSpec((tm, d), row),
                  pl.BlockSpec((1, 1, d), lambda i: (i // per_b, 0, 0)),
                  pl.BlockSpec((1, d), const),
                  pl.BlockSpec((1, d), const),
                  pl.BlockSpec(memory_space=pl.ANY)],
        out_specs=pl.BlockSpec((tm, d), row),
        out_shape=jax.ShapeDtypeStruct((n, d), F32),
        scratch_shapes=[pltpu.VMEM((2, TOP_K, tm * (d // LANES), LANES), F32), pltpu.SemaphoreType.DMA((2,))],
        compiler_params=_cparams(("arbitrary",)),
        name="moe_combine",
    )(dest3, dest3, meta, h, gf, ln_g.reshape(1, d), ln_b.reshape(1, d), y)


def moe_layer(h, sc, sh, gf, ln_g, ln_b, router_w, router_b, w1_bf, b1, w2_bf, b2, seq):
    n, d = h.shape
    ne = router_w.shape[1]
    rw_pad = jnp.pad(router_w, ((0, 0), (0, LANES - ne)))
    rb_pad = jnp.pad(router_b, (0, LANES - ne), constant_values=NEG).reshape(1, LANES)
    meta, cnt = moe_router(h, sc, sh, rw_pad, rb_pad, seq)
    idx4 = meta[:, 0:TOP_K].astype(I32)
    rank4 = meta[:, 2 * TOP_K:3 * TOP_K].astype(I32)
    counts = cnt[0, :ne].astype(I32)
    padded = (counts + MOE_BLK - 1) // MOE_BLK * MOE_BLK
    end = jnp.cumsum(padded)
    start = end - padded
    onehot = idx4[:, :, None] == jnp.arange(ne, dtype=I32)[None, None, :]
    dest = jnp.sum(jnp.where(onehot, start[None, None, :], 0), axis=-1) + rank4
    n_rows = -(-(n * TOP_K + ne * (MOE_BLK - 1)) // MOE_BLK) * MOE_BLK
    nblk = n_rows // MOE_BLK
    blk0 = jnp.arange(nblk, dtype=I32) * MOE_BLK
    n_used = (end[-1] // MOE_BLK).astype(I32)
    blk_exp = jnp.minimum(jnp.sum(end[None, :] <= blk0[:, None], axis=-1), ne - 1).astype(I32)
    last_exp = blk_exp[jnp.maximum(n_used - 1, 0)]
    blk_exp = jnp.where(jnp.arange(nblk) < n_used, blk_exp, last_exp)
    n_pad = n_rows - n * TOP_K
    seg_cnt = jnp.concatenate([padded - counts, (n_rows - end[-1])[None]])
    seg_base = jnp.concatenate([start + counts, end[-1:]])
    seg_end = jnp.cumsum(seg_cnt)
    q = jnp.arange(n_pad, dtype=I32)
    seg = jnp.sum(seg_end[None, :] <= q[:, None], axis=-1)
    pad_dest = (seg_base[seg] + q - (seg_end - seg_cnt)[seg]).astype(I32)
    xs = moe_dispatch(h, sc, sh, dest.reshape(-1), pad_dest, n_rows, seq)
    y = moe_experts(xs, blk_exp, n_used.reshape(1), w1_bf, b1, w2_bf, b2)
    return moe_combine(y, dest.reshape(-1), meta, h, gf, ln_g, ln_b, seq)


def _inproj_c_body(x_ref, sc_ref, sh_ref, w_ref, o_ref, acc_ref):
    tm = x_ref.shape[0]
    xb = (x_ref[...] * (1.0 + sc_ref[0]) + sh_ref[0]).astype(BF16)
    acc = jnp.dot(xb, w_ref[...], preferred_element_type=F32)
    ncol = acc.shape[1] // LANES
    for c in range(ncol):
        acc_ref[c] = acc[:, c * LANES:(c + 1) * LANES]
    for t in range(SSM_T):
        for c in range(ncol):
            o_ref[t, :, c * LANES:(c + 1) * LANES] = acc_ref[c, pl.ds(t, tm // SSM_T, stride=SSM_T), :]


def inproj_c(h, sc, sh, w_bf, seq):
    n, d = h.shape
    width = w_bf.shape[1]
    tm = min(512, seq)
    per_b = seq // tm
    return pl.pallas_call(
        _inproj_c_body,
        grid=(n // tm,),
        in_specs=[pl.BlockSpec((tm, d), lambda i: (i, 0)),
                  pl.BlockSpec((1, 1, d), lambda i: (i // per_b, 0, 0)),
                  pl.BlockSpec((1, 1, d), lambda i: (i // per_b, 0, 0)),
                  _resident((d, width), lambda i: (0, 0))],
        out_specs=pl.BlockSpec((SSM_T, tm // SSM_T, width), lambda i: (0, i, 0)),
        out_shape=jax.ShapeDtypeStruct((SSM_T, n // SSM_T, width), F32),
        scratch_shapes=[pltpu.VMEM((width // LANES, tm, LANES), F32)],
        compiler_params=_cparams(("arbitrary",)),
        name="inproj_c",
    )(h, sc, sh, w_bf)


def _s5_body(u_ref, wt_ref, wa_ref, wc_ref, lam_ref, d_ref, o_ref, st_ref, v_ref, xs_ref):
    cb = u_ref.shape[1]
    ns = st_ref.shape[1] // 2

    @pl.when(pl.program_id(2) == 0)
    def _():
        st_ref[...] = jnp.zeros_like(st_ref)

    xcat = jnp.concatenate([u_ref[t].astype(BF16) for t in range(SSM_T)], axis=1)
    v_ref[...] = jnp.dot(xcat, wa_ref[0], preferred_element_type=F32)
    lam = lam_ref[0]
    a_re, a_im = lam[:, :ns], lam[:, ns:]

    def step(r, st):
        xs_ref[pl.ds(r, 1), :] = st
        v = v_ref[pl.ds(r, 1), :]
        s_re, s_im = st[:, :ns], st[:, ns:]
        n_re = a_re * s_re - a_im * s_im + v[:, :ns]
        n_im = a_re * s_im + a_im * s_re + v[:, ns:]
        return jnp.concatenate([n_re, n_im], axis=1)

    st_ref[...] = lax.fori_loop(0, cb, step, st_ref[...])
    y = jnp.dot(xcat, wt_ref[0], preferred_element_type=F32)
    y = y + jnp.dot(xs_ref[...].astype(BF16), wc_ref[0], preferred_element_type=F32)
    dvec = d_ref[0]
    for t in range(SSM_T):
        yt = y[:, t * LANES:(t + 1) * LANES] + dvec * u_ref[t]
        gel = 0.5 * yt * (1.0 + jnp.tanh(0.7978845608028654 * (yt + 0.044715 * yt * yt * yt)))
        o_ref[t] = gel.astype(o_ref.dtype)


def s5_mixer(u2, wt, wa, wc, lam_t, dpack, bsz, seq):
    _, nchunk, width = u2.shape
    npack = width // LANES
    per_b = seq // SSM_T
    cb = min(256, per_b)
    nblk = per_b // cb
    ns2 = wa.shape[2]
    kdim = SSM_T * LANES
    pk = lambda j, b, i: (j, 0, 0)
    return pl.pallas_call(
        _s5_body,
        grid=(npack, bsz, nblk),
        in_specs=[pl.BlockSpec((SSM_T, cb, LANES), lambda j, b, i: (0, b * nblk + i, j)),
                  _resident((1, kdim, kdim), pk),
                  _resident((1, kdim, ns2), pk),
                  _resident((1, ns2, kdim), pk),
                  pl.BlockSpec((1, 1, ns2), pk),
                  pl.BlockSpec((1, 1, LANES), pk)],
        out_specs=pl.BlockSpec((SSM_T, cb, LANES), lambda j, b, i: (0, b * nblk + i, j)),
        out_shape=jax.ShapeDtypeStruct((SSM_T, nchunk, width), F32),
        scratch_shapes=[pltpu.VMEM((1, ns2), F32), pltpu.VMEM((cb, ns2), F32), pltpu.VMEM((cb, ns2), F32)],
        compiler_params=_cparams(("arbitrary", "arbitrary", "arbitrary")),
        name="s5_mixer",
    )(u2, wt, wa, wc, lam_t, dpack)


def s5_operators(a_re, a_im, log_dt, b_re, b_im, c_re, c_im, dskip):
    g, p = a_re.shape
    cg = b_re.shape[2]
    npack = g // SSM_PACK
    ns2 = SSM_PACK * 2 * p
    t = SSM_T
    lam = lax.complex(a_re, a_im)
    dt = jnp.exp(log_dt)[:, None]
    lam_bar = jnp.exp(lam * dt)
    b_bar = ((lam_bar - 1.0) / lam)[..., None] * lax.complex(b_re, b_im)
    c_mat = lax.complex(c_re, c_im)
    kpow = jnp.arange(t + 1, dtype=F32)[:, None, None]
    pows = jnp.exp((lam * dt)[None] * kpow)
    lane_c = jnp.arange(LANES) % cg
    lane_g = jnp.arange(LANES) // cg
    tile = (lane_c[None, :] == jnp.arange(cg)[:, None]).astype(F32)
    diag = lane_g[:, None] == lane_g[None, :]
    kern = jnp.real(jnp.einsum('gcp,kgp,gpd->kgdc', c_mat, pows[:t], b_bar, precision=HIGHEST))
    kfull = jnp.where(diag, jnp.dot(kern.reshape(t, npack, LANES, cg), tile, precision=HIGHEST), 0.0)
    kft = kfull.transpose(1, 2, 0, 3)
    wt = jnp.stack([jnp.pad(kft[:, :, :t - s].reshape(npack, LANES, (t - s) * LANES),
                            ((0, 0), (0, 0), (s * LANES, 0))) for s in range(t)], axis=1)
    wt = wt.reshape(npack, t * LANES, t * LANES)
    ns = SSM_PACK * p
    col_g = jnp.arange(ns) // p
    tile_p = (jnp.arange(ns)[None, :] % p == jnp.arange(p)[:, None]).astype(F32)
    wa_c = pows[t - 1 - jnp.arange(t)][:, :, :, None] * b_bar[None]
    wa_c = wa_c.transpose(0, 1, 3, 2).reshape(t, npack, LANES, p)
    own_a = lane_g[:, None] == col_g[None, :]
    wa = jnp.concatenate([jnp.where(own_a, jnp.dot(part, tile_p, precision=HIGHEST), 0.0)
                          for part in (jnp.real(wa_c), jnp.imag(wa_c))], axis=-1)
    wa = wa.transpose(1, 0, 2, 3).reshape(npack, t * LANES, ns2)
    wc_c = c_mat[None] * pows[1:t + 1][:, :, None, :]
    wc_c = wc_c.transpose(0, 1, 3, 2).reshape(t, npack, ns, cg)
    cc = jnp.concatenate([jnp.real(wc_c), -jnp.imag(wc_c)], axis=2)
    own_c = (jnp.arange(ns2) % ns // p)[:, None] == lane_g[None, :]
    wcf = jnp.where(own_c, jnp.dot(cc, tile, precision=HIGHEST), 0.0)
    wc = wcf.transpose(1, 2, 0, 3).reshape(npack, ns2, t * LANES)
    lam_t = pows[t].reshape(npack, 1, ns)
    lam_t = jnp.concatenate([jnp.real(lam_t), jnp.imag(lam_t)], axis=-1)
    dpack = dskip.reshape(npack, 1, LANES)
    return wt.astype(BF16), wa.astype(BF16), wc.astype(BF16), lam_t, dpack


def _glu_out_body(y_ref, wg_ref, bg_ref, wo_ref, h_ref, gm_ref, lg_ref, lb_ref, o_ref, ytm_ref):
    cb = y_ref.shape[1]
    ncol = y_ref.shape[2] // LANES
    for t in range(SSM_T):
        for c in range(ncol):
            ytm_ref[c, pl.ds(t, cb, stride=SSM_T), :] = y_ref[t, :, c * LANES:(c + 1) * LANES]
    y = jnp.concatenate([ytm_ref[c] for c in range(ncol)], axis=1)
    y_bf = y.astype(BF16)
    gate = jnp.dot(y_bf, wg_ref[...], preferred_element_type=F32) + bg_ref[...]
    z = (y * jax.nn.sigmoid(gate)).astype(BF16)
    out = jnp.dot(z, wo_ref[...], preferred_element_type=F32)
    zz = DN_ALPHA * h_ref[...] + (1.0 + gm_ref[0]) * out
    o_ref[...] = _layer_norm(zz, lg_ref[...], lb_ref[...])


def glu_out(y2, wg_bf, bg, wo_bf, h, gm, ln_g, ln_b, seq):
    n, d = h.shape
    width = y2.shape[2]
    tm = min(256, seq)
    cb = tm // SSM_T
    per_b = seq // tm
    row = lambda i: (i, 0)
    const = lambda i: (0, 0)
    return pl.pallas_call(
        _glu_out_body,
        grid=(n // tm,),
        in_specs=[pl.BlockSpec((SSM_T, cb, width), lambda i: (0, i, 0)),
                  _resident((width, width), const),
                  pl.BlockSpec((1, width), const),
                  _resident((width, d), const),
                  pl.BlockSpec((tm, d), row),
                  pl.BlockSpec((1, 1, d), lambda i: (i // per_b, 0, 0)),
                  pl.BlockSpec((1, d), const),
                  pl.BlockSpec((1, d), const)],
        out_specs=pl.BlockSpec((tm, d), row),
        out_shape=jax.ShapeDtypeStruct((n, d), F32),
        scratch_shapes=[pltpu.VMEM((width // LANES, tm, LANES), F32)],
        compiler_params=_cparams(("arbitrary",)),
        name="glu_out",
    )(y2, wg_bf, bg.reshape(1, width), wo_bf, h, gm, ln_g.reshape(1, d), ln_b.reshape(1, d))


def kernel(x, c, positions, ada_w, ada_b, ln_g, ln_b, ab_in_w, ab_in_b, ab_sinks, ab_gnorm_w, ab_out_w, ab_out_b, hgrn_lb_logits, c_in_w, c_A_re, c_A_im, c_log_dt, c_B_re, c_B_im, c_C_re, c_C_im, c_D, c_glu_w, c_glu_b, c_out_w, router_w, router_b, exp_w1, exp_b1, exp_w2, exp_b2):
    bsz, seq, d = x.shape
    n = bsz * seq
    depth = ada_w.shape[0]
    mod = ada_mod(c, ada_w, ada_b)
    lb_all = jnp.cumsum(jax.nn.softmax(hgrn_lb_logits.astype(F32), axis=0), axis=0)
    half = A_HEAD_DIM // 2
    inv_freq = ROPE_THETA ** (-jnp.arange(half, dtype=F32) / half)
    invf = jnp.tile(inv_freq, LANES // half).reshape(1, LANES)
    pos_col = positions.reshape(n, 1)
    h = x.reshape(n, d)
    for layer in range(depth):
        m6 = mod[layer].reshape(bsz, 6, 1, d)
        sh_m, sc_m, g_m, sh_f, sc_f, g_f = [m6[:, k] for k in range(6)]
        i = layer // 2
        if layer % 2 == 0:
            proj = inproj_ab(h, sc_m, sh_m, ab_in_w[i].astype(BF16), ab_in_b[i], seq)
            oa = swa_attention(proj, pos_col, invf, ab_sinks[i], bsz, seq)
            ob = hgrn2(proj, lb_all[layer], ab_gnorm_w[i], bsz, seq)
            h = outproj_ab(oa, ob, ab_out_w[i].astype(BF16), ab_out_b[i], h, g_m,
                           ln_g[layer, 0], ln_b[layer, 0], seq)
        else:
            wt, wa, wc, lam_t, dpack = s5_operators(c_A_re[i], c_A_im[i], c_log_dt[i], c_B_re[i], c_B_im[i],
                                                    c_C_re[i], c_C_im[i], c_D[i])
            u2 = inproj_c(h, sc_m, sh_m, c_in_w[i].astype(BF16), seq)
            y2 = s5_mixer(u2, wt, wa, wc, lam_t, dpack, bsz, seq)
            h = glu_out(y2, c_glu_w[i].astype(BF16), c_glu_b[i], c_out_w[i].astype(BF16), h, g_m,
                        ln_g[layer, 0], ln_b[layer, 0], seq)
        h = moe_layer(h, sc_f, sh_f, g_f, ln_g[layer, 1], ln_b[layer, 1], router_w[layer], router_b[layer],
                      exp_w1[layer].astype(BF16), exp_b1[layer], exp_w2[layer].astype(BF16), exp_b2[layer], seq)
    return h.reshape(bsz, seq, d)
```

```python
import functools

import jax
import jax.numpy as jnp
from jax import lax
from jax.experimental import pallas as pl
from jax.experimental.pallas import tpu as pltpu

F32 = jnp.float32
BF16 = jnp.bfloat16
I32 = jnp.int32
HIGHEST = lax.Precision.HIGHEST

DEPTH = 2
A_HEADS = 16
A_KV_HEADS = 2
A_HEAD_DIM = 64
WINDOW = 128
ROPE_THETA = 10000.0
B_HEADS = 8
B_DIM = 128
GLA_CHUNK = 64
C_GROUP = 16
C_STATE = 64
N_EXPERTS = 32
TOP_K = 4
SWIGLU_LIMIT = 7.0
SWIGLU_ALPHA = 1.702
DN_ALPHA = (2 * DEPTH) ** 0.25
LN_EPS = 1e-5
RMS_EPS = 1e-6

LANES = 128
VMEM_LIMIT = 56 * 1024 * 1024

ATT_BLK = 128
HGRN_T = 512
MOE_BLK = 256
SSM_T = 16
SSM_PACK = LANES // C_GROUP
S5_COL_BLOCKS = 4
DMA_GROUP = 8
COMBINE_GROUP = 32
NEG = -1e30


def _row_pitch(d):
    return d // LANES + 1


def _cparams(sem):
    return pltpu.CompilerParams(dimension_semantics=sem, vmem_limit_bytes=VMEM_LIMIT)


def _resident(block_shape, index_map):
    return pl.BlockSpec(block_shape, index_map, pipeline_mode=pl.Buffered(1))


def _layer_norm(z, g, b):
    mu = jnp.mean(z, axis=-1, keepdims=True)
    zc = z - mu
    var = jnp.mean(zc * zc, axis=-1, keepdims=True)
    return zc * lax.rsqrt(var + LN_EPS) * g + b


def _ada_body(ct_ref, w_ref, b_ref, o_ref):
    ct = ct_ref[...]
    cond = ct * jax.nn.sigmoid(ct)
    w = w_ref[0]
    rows = [jnp.sum(w * cond[:, b:b + 1], axis=0, keepdims=True) for b in range(ct.shape[1])]
    o_ref[0] = jnp.concatenate(rows, axis=0) + b_ref[0]


def ada_mod(c, ada_w, ada_b):
    depth, d, n6 = ada_w.shape
    bsz = c.shape[0]
    tn = 1536
    return pl.pallas_call(
        _ada_body,
        grid=(depth, n6 // tn),
        in_specs=[pl.BlockSpec((d, bsz), lambda l, j: (0, 0)),
                  pl.BlockSpec((1, d, tn), lambda l, j: (l, 0, j)),
                  pl.BlockSpec((1, 1, tn), lambda l, j: (l, 0, j))],
        out_specs=pl.BlockSpec((1, bsz, tn), lambda l, j: (l, 0, j)),
        out_shape=jax.ShapeDtypeStruct((depth, bsz, n6), F32),
        compiler_params=_cparams(("arbitrary", "arbitrary")),
        name="ada_mod",
    )(c.T, ada_w, ada_b.reshape(depth, 1, n6))


def _inproj_ab_body(x_ref, sc_ref, sh_ref, w_ref, b_ref, o_ref, xb_ref):
    @pl.when(pl.program_id(1) == 0)
    def _():
        xb_ref[...] = (x_ref[...] * (1.0 + sc_ref[0]) + sh_ref[0]).astype(BF16)

    o_ref[...] = jnp.dot(xb_ref[...], w_ref[...], preferred_element_type=F32) + b_ref[...]


def inproj_ab(h, sc, sh, w_bf, b, seq):
    n, d = h.shape
    nc = w_bf.shape[1]
    tm = min(1024, seq)
    tn = 768
    per_b = seq // tm
    return pl.pallas_call(
        _inproj_ab_body,
        grid=(n // tm, nc // tn),
        in_specs=[pl.BlockSpec((tm, d), lambda i, j: (i, 0)),
                  pl.BlockSpec((1, 1, d), lambda i, j: (i // per_b, 0, 0)),
                  pl.BlockSpec((1, 1, d), lambda i, j: (i // per_b, 0, 0)),
                  pl.BlockSpec((d, tn), lambda i, j: (0, j)),
                  pl.BlockSpec((1, tn), lambda i, j: (0, j))],
        out_specs=pl.BlockSpec((tm, tn), lambda i, j: (i, j)),
        out_shape=jax.ShapeDtypeStruct((n, nc), F32),
        scratch_shapes=[pltpu.VMEM((tm, d), BF16)],
        compiler_params=_cparams(("arbitrary", "arbitrary")),
        name="inproj_ab",
    )(h, sc, sh, w_bf, b.reshape(1, nc))


def _attn_body(sink_ref, q_ref, kc_ref, kp_ref, vc_ref, vp_ref, pc_ref, pp_ref, invf_ref, o_ref):
    blk = ATT_BLK
    half = A_HEAD_DIM // 2
    grp = A_HEADS // A_KV_HEADS
    i = pl.program_id(1)
    lane = lax.broadcasted_iota(I32, (blk, LANES), 1)
    first_half = (lane % A_HEAD_DIM) < half
    low_head = lane < A_HEAD_DIM
    invf = invf_ref[...]

    def cos_sin(pos_col):
        ang = pos_col.astype(F32) * invf
        s = jnp.sin(ang)
        return jnp.cos(ang), jnp.where(first_half, -s, s)

    def rope(x, cs):
        partner = jnp.where(first_half, pltpu.roll(x, LANES - half, 1), pltpu.roll(x, half, 1))
        return x * cs[0] + partner * cs[1]

    cs_c = cos_sin(pc_ref[...])
    cs_p = cos_sin(pp_ref[...])
    k2 = jnp.concatenate([rope(kp_ref[...], cs_p), rope(kc_ref[...], cs_c)], axis=0)
    v2 = jnp.concatenate([vp_ref[...], vc_ref[...]], axis=0)
    lane2 = lax.broadcasted_iota(I32, (2 * blk, LANES), 1)
    k2r = pltpu.roll(k2, A_HEAD_DIM, 1)
    v2r = pltpu.roll(v2, A_HEAD_DIM, 1)

    rows = lax.broadcasted_iota(I32, (grp * blk, 2 * blk), 0) % blk
    cols = lax.broadcasted_iota(I32, (grp * blk, 2 * blk), 1)
    valid = (cols > rows) & (cols <= rows + WINDOW) & ((cols >= blk) | (i > 0))

    q_chunks = [rope(q_ref[:, c * LANES:(c + 1) * LANES], cs_c) * (A_HEAD_DIM ** -0.5)
                for c in range(A_HEADS * A_HEAD_DIM // LANES)]

    for g in range(A_KV_HEADS):
        own = (lane2 < A_HEAD_DIM) if g == 0 else (lane2 >= A_HEAD_DIM)
        kg = jnp.where(own, k2, k2r).astype(BF16)
        vg = jnp.where(own, v2, v2r).astype(BF16)
        pieces = []
        sinks = []
        for r in range(grp):
            hd = g * grp + r
            qc = q_chunks[hd // 2]
            keep = low_head if hd % 2 == 0 else jnp.logical_not(low_head)
            pieces.append(jnp.where(keep, qc, 0.0).astype(BF16))
            sinks.append(jnp.full((blk, 1), sink_ref[hd], F32))
        qs = jnp.concatenate(pieces, axis=0)
        sink = jnp.concatenate(sinks, axis=0)
        s = lax.dot_general(qs, kg, (((1,), (1,)), ((), ())), preferred_element_type=F32)
        s = jnp.where(valid, s, NEG)
        m = jnp.maximum(jnp.max(s, axis=-1, keepdims=True), sink)
        p = jnp.exp(s - m)
        den = jnp.sum(p, axis=-1, keepdims=True) + jnp.exp(sink - m)
        o = jnp.dot(p.astype(BF16), vg, preferred_element_type=F32) / den
        for r2 in range(grp // 2):
            c = (g * grp) // 2 + r2
            ev = o[(2 * r2) * blk:(2 * r2 + 1) * blk]
            od = o[(2 * r2 + 1) * blk:(2 * r2 + 2) * blk]
            o_ref[:, c * LANES:(c + 1) * LANES] = jnp.where(low_head, ev, od).astype(o_ref.dtype)


def swa_attention(proj, pos_col, invf, sinks, bsz, seq):
    n = proj.shape[0]
    blk = ATT_BLK
    nb = seq // blk
    aq = A_HEADS * A_HEAD_DIM
    kcol = aq // LANES
    vcol = kcol + 1

    def cur(b, i, s):
        return b * nb + i

    def prev(b, i, s):
        return b * nb + jnp.maximum(i - 1, 0)

    grid_spec = pltpu.PrefetchScalarGridSpec(
        num_scalar_prefetch=1,
        grid=(bsz, nb),
        in_specs=[pl.BlockSpec((blk, aq), lambda b, i, s: (cur(b, i, s), 0)),
                  pl.BlockSpec((blk, LANES), lambda b, i, s: (cur(b, i, s), kcol)),
                  pl.BlockSpec((blk, LANES), lambda b, i, s: (prev(b, i, s), kcol)),
                  pl.BlockSpec((blk, LANES), lambda b, i, s: (cur(b, i, s), vcol)),
                  pl.BlockSpec((blk, LANES), lambda b, i, s: (prev(b, i, s), vcol)),
                  pl.BlockSpec((blk, 1), lambda b, i, s: (cur(b, i, s), 0)),
                  pl.BlockSpec((blk, 1), lambda b, i, s: (prev(b, i, s), 0)),
                  pl.BlockSpec((1, LANES), lambda b, i, s: (0, 0))],
        out_specs=pl.BlockSpec((blk, aq), lambda b, i, s: (cur(b, i, s), 0)),
    )
    return pl.pallas_call(
        _attn_body,
        grid_spec=grid_spec,
        out_shape=jax.ShapeDtypeStruct((n, aq), BF16),
        compiler_params=_cparams(("arbitrary", "arbitrary")),
        name="swa_attention",
    )(sinks, proj, proj, proj, proj, proj, pos_col, pos_col, invf)


def _hgrn_body(q_ref, f_ref, i_ref, g_ref, lb_ref, gw_ref, o_ref, st_ref):
    L = GLA_CHUNK

    @pl.when(pl.program_id(2) == 0)
    def _():
        st_ref[...] = jnp.zeros_like(st_ref)

    T = q_ref.shape[0]
    nch = T // L
    lb = lb_ref[0]
    gw = gw_ref[0]
    nt = (((1,), (1,)), ((), ()))

    f = lb + (1.0 - lb) * jax.nn.sigmoid(f_ref[...])
    kk = 1.0 - f
    qb = q_ref[...]
    q = qb * jax.nn.sigmoid(qb)
    v = i_ref[...]
    pos = lax.broadcasted_iota(I32, (T, B_DIM), 0) % L
    b = jnp.log(f)
    sh = 1
    while sh < L:
        b = b + jnp.where(pos >= sh, pltpu.roll(b, sh, 0), 0.0)
        sh *= 2

    def per_chunk(row):
        return jnp.concatenate(
            [jnp.broadcast_to(b[c * L + row:c * L + row + 1], (L, B_DIM)) for c in range(nch)], axis=0)

    b_mid = per_chunk(L // 2)
    b_last = per_chunk(L - 1)
    e_mid = jnp.exp(b - b_mid)
    qa = (q * e_mid).astype(BF16)
    ka = (kk * jnp.exp(b_mid - b)).astype(BF16)
    qe = (q * jnp.exp(b)).astype(BF16)
    kd = (kk * jnp.exp(b_last - b)).astype(BF16)
    att = lax.dot_general(qa, ka, nt, preferred_element_type=F32)
    r = lax.broadcasted_iota(I32, (T, T), 0)
    c = lax.broadcasted_iota(I32, (T, T), 1)
    att = jnp.where((r >= c) & (r // L == c // L), att, 0.0)
    o_intra = jnp.dot(att.astype(BF16), v.astype(BF16), preferred_element_type=F32)
    vt = v.T.astype(BF16)
    st = st_ref[...]
    o_inter = []
    for ci in range(nch):
        sl = slice(ci * L, (ci + 1) * L)
        o_inter.append(lax.dot_general(qe[sl], st.astype(BF16), nt, preferred_element_type=F32))
        u_t = jnp.dot(vt[:, sl], kd[sl], preferred_element_type=F32)
        st = jnp.exp(b[ci * L + L - 1:ci * L + L]) * st + u_t
    st_ref[...] = st
    o = o_intra + jnp.concatenate(o_inter, axis=0)
    o = o * lax.rsqrt(jnp.mean(o * o, axis=-1, keepdims=True) + RMS_EPS) * gw
    gb = g_ref[...]
    o_ref[...] = (o * (gb * jax.nn.sigmoid(gb))).astype(o_ref.dtype)


def hgrn2(proj, lb, gnorm_w, bsz, seq):
    n = proj.shape[0]
    t = min(HGRN_T, seq)
    nt = seq // t
    base = (A_HEADS + 2 * A_KV_HEADS) * A_HEAD_DIM // LANES
    h = B_HEADS

    def col(k):
        return lambda b, hd, i: (b * nt + i, base + k * h + hd)

    return pl.pallas_call(
        _hgrn_body,
        grid=(bsz, h, nt),
        in_specs=[pl.BlockSpec((t, LANES), col(0)),
                  pl.BlockSpec((t, LANES), col(1)),
                  pl.BlockSpec((t, LANES), col(2)),
                  pl.BlockSpec((t, LANES), col(3)),
                  pl.BlockSpec((1, 1, B_DIM), lambda b, hd, i: (hd, 0, 0)),
                  pl.BlockSpec((1, 1, B_DIM), lambda b, hd, i: (hd, 0, 0))],
        out_specs=pl.BlockSpec((t, LANES), lambda b, hd, i: (b * nt + i, hd)),
        out_shape=jax.ShapeDtypeStruct((n, h * B_DIM), BF16),
        scratch_shapes=[pltpu.VMEM((B_DIM, B_DIM), F32)],
        compiler_params=_cparams(("arbitrary", "arbitrary", "arbitrary")),
        name="hgrn2",
    )(proj, proj, proj, proj, lb.reshape(h, 1, B_DIM), gnorm_w.reshape(h, 1, B_DIM))


def _outproj_ab_body(oa_ref, ob_ref, w_ref, b_ref, h_ref, gm_ref, lg_ref, lb_ref, o_ref):
    ka = oa_ref.shape[1]
    y = jnp.dot(oa_ref[...], w_ref[:ka, :], preferred_element_type=F32)
    y = y + jnp.dot(ob_ref[...], w_ref[ka:, :], preferred_element_type=F32) + b_ref[...]
    z = DN_ALPHA * h_ref[...] + (1.0 + gm_ref[0]) * y
    o_ref[...] = _layer_norm(z, lg_ref[...], lb_ref[...])


def outproj_ab(oa, ob, w_bf, b, h, gm, ln_g, ln_b, seq):
    n, d = h.shape
    tm = min(512, seq)
    per_b = seq // tm
    ka, kb = oa.shape[1], ob.shape[1]
    row = lambda i: (i, 0)
    const = lambda i: (0, 0)
    return pl.pallas_call(
        _outproj_ab_body,
        grid=(n // tm,),
        in_specs=[pl.BlockSpec((tm, ka), row),
                  pl.BlockSpec((tm, kb), row),
                  _resident((ka + kb, d), const),
                  pl.BlockSpec((1, d), const),
                  pl.BlockSpec((tm, d), row),
                  pl.BlockSpec((1, 1, d), lambda i: (i // per_b, 0, 0)),
                  pl.BlockSpec((1, d), const),
                  pl.BlockSpec((1, d), const)],
        out_specs=pl.BlockSpec((tm, d), row),
        out_shape=jax.ShapeDtypeStruct((n, d), F32),
        compiler_params=_cparams(("arbitrary",)),
        name="outproj_ab",
    )(oa, ob, w_bf, b.reshape(1, d), h, gm, ln_g.reshape(1, d), ln_b.reshape(1, d))


def _router_body(h_ref, sc_ref, sh_ref, rwh_ref, rwl_ref, rb_ref, meta_ref, cnt_ref, carry_ref):
    tm = h_ref.shape[0]

    @pl.when(pl.program_id(0) == 0)
    def _():
        carry_ref[...] = jnp.zeros_like(carry_ref)

    xin = h_ref[...] * (1.0 + sc_ref[0]) + sh_ref[0]
    x_hi = xin.astype(BF16)
    x_lo = (xin - x_hi.astype(F32)).astype(BF16)
    logits = (jnp.dot(x_hi, rwh_ref[...], preferred_element_type=F32)
              + (jnp.dot(x_hi, rwl_ref[...], preferred_element_type=F32)
                 + jnp.dot(x_lo, rwh_ref[...], preferred_element_type=F32))) + rb_ref[...]
    lane = lax.broadcasted_iota(I32, (tm, LANES), 1)
    l = logits
    vals, idxs = [], []
    sel = jnp.zeros((tm, LANES), F32)
    for _ in range(TOP_K):
        m = jnp.max(l, axis=-1, keepdims=True)
        idx = jnp.min(jnp.where(l == m, lane, LANES), axis=-1, keepdims=True)
        hit = lane == idx
        vals.append(m)
        idxs.append(idx)
        sel = jnp.where(hit, 1.0, sel)
        l = jnp.where(hit, -jnp.inf, l)
    es = [jnp.exp(v - vals[0]) for v in vals]
    den = es[0] + es[1] + es[2] + es[3]
    r = lax.broadcasted_iota(I32, (tm, tm), 0)
    c = lax.broadcasted_iota(I32, (tm, tm), 1)
    tril = (r >= c).astype(BF16)
    incl = jnp.dot(tril, sel.astype(BF16), preferred_element_type=F32)
    carry = carry_ref[...]
    excl = incl - sel + carry
    meta = jnp.zeros((tm, LANES), F32)
    for k in range(TOP_K):
        rank = jnp.sum(jnp.where(lane == idxs[k], excl, 0.0), axis=-1, keepdims=True)
        meta = jnp.where(lane == k, idxs[k].astype(F32), meta)
        meta = jnp.where(lane == TOP_K + k, es[k] / den, meta)
        meta = jnp.where(lane == 2 * TOP_K + k, rank, meta)
    meta_ref[...] = meta
    carry = carry + incl[tm - 1:tm, :]
    carry_ref[...] = carry
    cnt_ref[...] = jnp.broadcast_to(carry, cnt_ref.shape)


def moe_router(h, sc, sh, rw_pad, rb_pad, seq):
    n, d = h.shape
    rw_hi = rw_pad.astype(BF16)
    rw_hi_residual = (rw_pad - rw_hi.astype(F32)).astype(BF16)
    tm = min(512, seq)
    per_b = seq // tm
    return pl.pallas_call(
        _router_body,
        grid=(n // tm,),
        in_specs=[pl.BlockSpec((tm, d), lambda i: (i, 0)),
                  pl.BlockSpec((1, 1, d), lambda i: (i // per_b, 0, 0)),
                  pl.BlockSpec((1, 1, d), lambda i: (i // per_b, 0, 0)),
                  pl.BlockSpec((d, LANES), lambda i: (0, 0)),
                  pl.BlockSpec((d, LANES), lambda i: (0, 0)),
                  pl.BlockSpec((1, LANES), lambda i: (0, 0))],
        out_specs=[pl.BlockSpec((tm, LANES), lambda i: (i, 0)),
                   pl.BlockSpec((8, LANES), lambda i: (0, 0))],
        out_shape=[jax.ShapeDtypeStruct((n, LANES), F32),
                   jax.ShapeDtypeStruct((8, LANES), F32)],
        scratch_shapes=[pltpu.VMEM((1, LANES), F32)],
        compiler_params=_cparams(("arbitrary",)),
        name="moe_router",
    )(h, sc, sh, rw_hi, rw_hi_residual, rb_pad)


def _dispatch_body(pad_ref, dest_ref, h_ref, sc_ref, sh_ref, xs_hbm, buf_ref, zero_ref, sem, zsem):
    tm, d = h_ref.shape
    rt = d // LANES
    pitch = _row_pitch(d)
    i = pl.program_id(0)
    last = pl.num_programs(0) - 1
    slot = i % 2
    n_pad = pad_ref.shape[0]

    def wait_slot(s):
        for _ in range(TOP_K):
            pltpu.make_async_copy(buf_ref.at[s], xs_hbm.at[pl.ds(0, tm * pitch)], sem.at[s]).wait()

    @pl.when(i == 0)
    def _():
        zero_ref[...] = jnp.zeros_like(zero_ref)
        buf_ref[...] = jnp.zeros_like(buf_ref)

        def zfill(q, carry):
            pltpu.make_async_copy(zero_ref, xs_hbm.at[pl.ds(pad_ref[q] * pitch, pitch)], zsem.at[0]).start()
            return carry

        lax.fori_loop(0, n_pad, zfill, 0)

    @pl.when(i >= 2)
    def _():
        wait_slot(slot)

    xin = h_ref[...] * (1.0 + sc_ref[0]) + sh_ref[0]
    for s in range(rt):
        buf_ref[slot, pl.ds(s, tm, stride=pitch), :] = xin[:, s * LANES:(s + 1) * LANES]

    def issue(g, carry):
        for rr in range(DMA_GROUP):
            r = g * DMA_GROUP + rr
            src = buf_ref.at[slot, pl.ds(r * pitch, pitch)]
            for k in range(TOP_K):
                dst = dest_ref[0, 0, r * TOP_K + k] * pitch
                pltpu.make_async_copy(src, xs_hbm.at[pl.ds(dst, pitch)], sem.at[slot]).start()
        return carry

    lax.fori_loop(0, tm // DMA_GROUP, issue, 0)

    @pl.when(i == last)
    def _():
        wait_slot(slot)

        @pl.when(i >= 1)
        def _():
            wait_slot(1 - slot)

        for _ in range(n_pad // tm):
            pltpu.make_async_copy(buf_ref.at[0], xs_hbm.at[pl.ds(0, tm * pitch)], zsem.at[0]).wait()


def moe_dispatch(h, sc, sh, dest, pad_dest, n_rows, seq):
    n, d = h.shape
    tm = min(256, seq)
    per_b = seq // tm
    pitch = _row_pitch(d)
    assert pad_dest.shape[0] % tm == 0
    grid_spec = pltpu.PrefetchScalarGridSpec(
        num_scalar_prefetch=1,
        grid=(n // tm,),
        in_specs=[pl.BlockSpec((1, 1, tm * TOP_K), lambda i, p: (i, 0, 0), memory_space=pltpu.SMEM),
                  pl.BlockSpec((tm, d), lambda i, p: (i, 0)),
                  pl.BlockSpec((1, 1, d), lambda i, p: (i // per_b, 0, 0)),
                  pl.BlockSpec((1, 1, d), lambda i, p: (i // per_b, 0, 0))],
        out_specs=pl.BlockSpec(memory_space=pl.ANY),
        scratch_shapes=[pltpu.VMEM((2, tm * pitch, LANES), F32), pltpu.VMEM((pitch, LANES), F32),
                        pltpu.SemaphoreType.DMA((2,)), pltpu.SemaphoreType.DMA((1,))],
    )
    return pl.pallas_call(
        _dispatch_body,
        grid_spec=grid_spec,
        out_shape=jax.ShapeDtypeStruct((n_rows * pitch, LANES), F32),
        compiler_params=_cparams(("arbitrary",)),
        name="moe_dispatch",
    )(pad_dest, dest.reshape(n // tm, 1, tm * TOP_K), h, sc, sh)


def _expert_body(be_ref, nu_ref, xs_ref, w1_ref, b1_ref, w2_ref, b2_ref, y_ref):
    j = pl.program_id(0)
    ff = w2_ref.shape[1]
    d = w1_ref.shape[1]
    rt = d // LANES
    pitch = _row_pitch(d)
    blk = xs_ref.shape[0] // pitch

    @pl.when(j < nu_ref[0])
    def _():
        x = jnp.concatenate([xs_ref[pl.ds(s, blk, stride=pitch), :].astype(BF16) for s in range(rt)], axis=1)
        hid = jnp.dot(x, w1_ref[0], preferred_element_type=F32) + b1_ref[0]
        gate = jnp.minimum(hid[:, :ff], SWIGLU_LIMIT)
        lin = jnp.clip(hid[:, ff:], -SWIGLU_LIMIT, SWIGLU_LIMIT)
        act = gate * jax.nn.sigmoid(SWIGLU_ALPHA * gate) * (lin + 1.0)
        y = jnp.dot(act.astype(BF16), w2_ref[0], preferred_element_type=F32) + b2_ref[0]
        for s in range(rt):
            y_ref[pl.ds(s, blk, stride=pitch), :] = y[:, s * LANES:(s + 1) * LANES]
        for s in range(rt, pitch):
            y_ref[pl.ds(s, blk, stride=pitch), :] = jnp.zeros((blk, LANES), F32)

    @pl.when(j >= nu_ref[0])
    def _():
        y_ref[...] = jnp.zeros_like(y_ref)


def moe_experts(xs, blk_exp, n_used, w1_bf, b1, w2_bf, b2):
    ne, d, ff2 = w1_bf.shape
    pitch = _row_pitch(d)
    n_rows = xs.shape[0] // pitch
    ff = ff2 // 2
    nblk = n_rows // MOE_BLK
    grid_spec = pltpu.PrefetchScalarGridSpec(
        num_scalar_prefetch=2,
        grid=(nblk,),
        in_specs=[pl.BlockSpec((MOE_BLK * pitch, LANES), lambda j, be, nu: (jnp.minimum(j, nu[0] - 1), 0)),
                  pl.BlockSpec((1, d, ff2), lambda j, be, nu: (be[j], 0, 0)),
                  pl.BlockSpec((1, 1, ff2), lambda j, be, nu: (be[j], 0, 0)),
                  pl.BlockSpec((1, ff, d), lambda j, be, nu: (be[j], 0, 0)),
                  pl.BlockSpec((1, 1, d), lambda j, be, nu: (be[j], 0, 0))],
        out_specs=pl.BlockSpec((MOE_BLK * pitch, LANES), lambda j, be, nu: (j, 0)),
    )
    return pl.pallas_call(
        _expert_body,
        grid_spec=grid_spec,
        out_shape=jax.ShapeDtypeStruct((n_rows * pitch, LANES), F32),
        compiler_params=_cparams(("arbitrary",)),
        name="moe_experts",
    )(blk_exp, n_used, xs, w1_bf, b1.reshape(ne, 1, ff2), w2_bf, b2.reshape(ne, 1, d))


def _combine_body(dc_ref, dn_ref, meta_ref, h_ref, gf_ref, lg_ref, lb_ref, y_hbm, o_ref, buf_ref, sem):
    tm, d = h_ref.shape
    rt = d // LANES
    pitch = _row_pitch(d)
    grp = COMBINE_GROUP
    i = pl.program_id(0)
    n = pl.num_programs(0)
    slot = i % 2

    def issue_rows(dref, s, r0):
        for rr in range(grp):
            r = r0 + rr
            for k in range(TOP_K):
                src = dref[0, 0, r * TOP_K + k] * pitch
                pltpu.make_async_copy(y_hbm.at[pl.ds(src, rt)], buf_ref.at[s, k, pl.ds(r * pitch, rt)],
                                      sem.at[s]).start()

    @pl.when(i == 0)
    def _():
        def body(g, carry):
            issue_rows(dc_ref, 0, g * grp)
            return carry
        lax.fori_loop(0, tm // grp, body, 0)

    for k in range(TOP_K):
        pltpu.make_async_copy(y_hbm.at[pl.ds(0, tm * rt)], buf_ref.at[slot, k, pl.ds(0, tm * rt)],
                              sem.at[slot]).wait()

    def group(g, carry, prefetch):
        r0 = pl.multiple_of(g * grp, grp)
        if prefetch:
            issue_rows(dn_ref, 1 - slot, r0)
        meta = meta_ref[pl.ds(r0, grp), :]
        gates = [jnp.broadcast_to(meta[:, TOP_K + k:TOP_K + k + 1], (grp, LANES)) for k in range(TOP_K)]
        for s in range(rt):
            a = gates[0] * buf_ref[slot, 0, pl.ds(r0 * pitch + s, grp, stride=pitch), :]
            for k in range(1, TOP_K):
                a = a + gates[k] * buf_ref[slot, k, pl.ds(r0 * pitch + s, grp, stride=pitch), :]
            cs = slice(s * LANES, (s + 1) * LANES)
            o_ref[pl.ds(r0, grp), cs] = DN_ALPHA * h_ref[pl.ds(r0, grp), cs] + (1.0 + gf_ref[0, :, cs]) * a
        return carry

    @pl.when(i + 1 < n)
    def _():
        lax.fori_loop(0, tm // grp, functools.partial(group, prefetch=True), 0)

    @pl.when(i + 1 == n)
    def _():
        lax.fori_loop(0, tm // grp, functools.partial(group, prefetch=False), 0)

    o_ref[...] = _layer_norm(o_ref[...], lg_ref[...], lb_ref[...])


def moe_combine(y, dest, meta, h, gf, ln_g, ln_b, seq):
    n, d = h.shape
    tm = min(256, seq)
    per_b = seq // tm
    nblk = n // tm
    dest3 = dest.reshape(nblk, 1, tm * TOP_K)
    row = lambda i: (i, 0)
    const = lambda i: (0, 0)
    return pl.pallas_call(
        _combine_body,
        grid=(nblk,),
        in_specs=[pl.BlockSpec((1, 1, tm * TOP_K), lambda i: (i, 0, 0), memory_space=pltpu.SMEM),
                  pl.BlockSpec((1, 1, tm * TOP_K), lambda i: (jnp.minimum(i + 1, nblk - 1), 0, 0),
                               memory_space=pltpu.SMEM),
                  pl.BlockSpec((tm, LANES), row),
                  pl.BlockSpec((tm, d), row),
                  pl.BlockSpec((1, 1, d), lambda i: (i // per_b, 0, 0)),
                  pl.BlockSpec((1, d), const),
                  pl.BlockSpec((1, d), const),
                  pl.BlockSpec(memory_space=pl.ANY)],
        out_specs=pl.BlockSpec((tm, d), row),
        out_shape=jax.ShapeDtypeStruct((n, d), F32),
        scratch_shapes=[pltpu.VMEM((2, TOP_K, tm * _row_pitch(d), LANES), F32), pltpu.SemaphoreType.DMA((2,))],
        compiler_params=_cparams(("arbitrary",)),
        name="moe_combine",
    )(dest3, dest3, meta, h, gf, ln_g.reshape(1, d), ln_b.reshape(1, d), y)


def moe_layer(h, sc, sh, gf, ln_g, ln_b, router_w, router_b, layer, w1_bf, b1, w2_bf, b2, seq):
    n, d = h.shape
    ne = router_w.shape[1]
    rw_pad = jnp.pad(router_w, ((0, 0), (0, LANES - ne)))
    rb_pad = jnp.pad(router_b, (0, LANES - ne), constant_values=NEG).reshape(1, LANES)
    meta, cnt = moe_router(h, sc, sh, rw_pad, rb_pad, seq)
    idx4 = meta[:, 0:TOP_K].astype(I32)
    rank4 = meta[:, 2 * TOP_K:3 * TOP_K].astype(I32)
    counts = cnt[0, :ne].astype(I32)
    padded = (counts + MOE_BLK - 1) // MOE_BLK * MOE_BLK
    end = jnp.cumsum(padded)
    start = end - padded
    onehot = idx4[:, :, None] == jnp.arange(ne, dtype=I32)[None, None, :]
    dest = jnp.sum(jnp.where(onehot, start[None, None, :], 0), axis=-1) + rank4
    n_rows = -(-(n * TOP_K + ne * (MOE_BLK - 1)) // MOE_BLK) * MOE_BLK
    nblk = n_rows // MOE_BLK
    blk0 = jnp.arange(nblk, dtype=I32) * MOE_BLK
    n_used = (end[-1] // MOE_BLK).astype(I32)
    blk_exp = jnp.minimum(jnp.sum(end[None, :] <= blk0[:, None], axis=-1), ne - 1).astype(I32)
    last_exp = blk_exp[jnp.maximum(n_used - 1, 0)]
    blk_exp = jnp.where(jnp.arange(nblk) < n_used, blk_exp, last_exp)
    n_pad = n_rows - n * TOP_K
    seg_cnt = jnp.concatenate([padded - counts, (n_rows - end[-1])[None]])
    seg_base = jnp.concatenate([start + counts, end[-1:]])
    seg_end = jnp.cumsum(seg_cnt)
    q = jnp.arange(n_pad, dtype=I32)
    seg = jnp.sum(seg_end[None, :] <= q[:, None], axis=-1)
    pad_dest = (seg_base[seg] + q - (seg_end - seg_cnt)[seg]).astype(I32)
    xs = moe_dispatch(h, sc, sh, dest.reshape(-1), pad_dest, n_rows, seq)
    y = moe_experts(xs, blk_exp + layer * ne, n_used.reshape(1), w1_bf, b1, w2_bf, b2)
    return moe_combine(y, dest.reshape(-1), meta, h, gf, ln_g, ln_b, seq)


def _inproj_c_body(x_ref, sc_ref, sh_ref, w_ref, o_ref, acc_ref):
    tm = x_ref.shape[0]
    xb = (x_ref[...] * (1.0 + sc_ref[0]) + sh_ref[0]).astype(BF16)
    acc = jnp.dot(xb, w_ref[...], preferred_element_type=F32)
    ncol = acc.shape[1] // LANES
    for c in range(ncol):
        acc_ref[c] = acc[:, c * LANES:(c + 1) * LANES]
    for t in range(SSM_T):
        for c in range(ncol):
            o_ref[t, :, c * LANES:(c + 1) * LANES] = acc_ref[c, pl.ds(t, tm // SSM_T, stride=SSM_T), :]


def inproj_c(h, sc, sh, w_bf, seq):
    n, d = h.shape
    width = w_bf.shape[1]
    tm = min(512, seq)
    per_b = seq // tm
    return pl.pallas_call(
        _inproj_c_body,
        grid=(n // tm,),
        in_specs=[pl.BlockSpec((tm, d), lambda i: (i, 0)),
                  pl.BlockSpec((1, 1, d), lambda i: (i // per_b, 0, 0)),
                  pl.BlockSpec((1, 1, d), lambda i: (i // per_b, 0, 0)),
                  _resident((d, width), lambda i: (0, 0))],
        out_specs=pl.BlockSpec((SSM_T, tm // SSM_T, width), lambda i: (0, i, 0)),
        out_shape=jax.ShapeDtypeStruct((SSM_T, n // SSM_T, width), F32),
        scratch_shapes=[pltpu.VMEM((width // LANES, tm, LANES), F32)],
        compiler_params=_cparams(("arbitrary",)),
        name="inproj_c",
    )(h, sc, sh, w_bf)


def _s5_body(u_ref, wt_ref, wa_ref, wc_ref, lam_ref, d_ref, o_ref, st_ref, v_ref, xs_ref):
    nb, cb = u_ref.shape[1], u_ref.shape[2]
    ns = st_ref.shape[1] // 2
    kdim = SSM_T * LANES

    @pl.when(pl.program_id(1) == 0)
    def _():
        st_ref[...] = jnp.zeros_like(st_ref)

    xcat = jnp.concatenate(
        [jnp.concatenate([u_ref[t, b].astype(BF16) for t in range(SSM_T)], axis=1) for b in range(nb)], axis=0)
    v = jnp.dot(xcat, wa_ref[0], preferred_element_type=F32)
    nslab = 2 * ns // LANES
    for c in range(nslab):
        v_ref[c] = v[:, c * LANES:(c + 1) * LANES]
    lam = lam_ref[0]
    a_re, a_im = lam[:, :ns], lam[:, ns:]

    def step(r, st):
        for c in range(nslab):
            xs_ref[c, pl.ds(r, nb, stride=cb), :] = st[:, c * LANES:(c + 1) * LANES]
        v = jnp.concatenate([v_ref[c, pl.ds(r, nb, stride=cb), :] for c in range(nslab)], axis=1)
        s_re, s_im = st[:, :ns], st[:, ns:]
        n_re = a_re * s_re - a_im * s_im + v[:, :ns]
        n_im = a_re * s_im + a_im * s_re + v[:, ns:]
        return jnp.concatenate([n_re, n_im], axis=1)

    st_ref[...] = lax.fori_loop(0, cb, step, st_ref[...])
    xs_bf = jnp.concatenate([xs_ref[c].astype(BF16) for c in range(nslab)], axis=1)
    dvec = d_ref[0]
    for q in range(S5_COL_BLOCKS):
        c0, c1 = q * kdim // S5_COL_BLOCKS, (q + 1) * kdim // S5_COL_BLOCKS
        y = jnp.dot(xcat[:, :c1], wt_ref[0, :c1, c0:c1], preferred_element_type=F32)
        y = y + jnp.dot(xs_bf, wc_ref[0, :, c0:c1], preferred_element_type=F32)
        for tt in range((c1 - c0) // LANES):
            t = c0 // LANES + tt
            for b in range(nb):
                yt = y[b * cb:(b + 1) * cb, tt * LANES:(tt + 1) * LANES] + dvec * u_ref[t, b]
                gel = 0.5 * yt * (1.0 + jnp.tanh(0.7978845608028654 * (yt + 0.044715 * yt * yt * yt)))
                o_ref[t, b] = gel.astype(o_ref.dtype)


def s5_mixer(u2, wt, wa, wc, lam_t, dpack, bsz, seq):
    _, nchunk, width = u2.shape
    npack = width // LANES
    per_b = seq // SSM_T
    cb = min(256, per_b)
    nblk = per_b // cb
    ns2 = wa.shape[2]
    kdim = SSM_T * LANES
    pk = lambda j, i: (j, 0, 0)
    u4 = u2.reshape(SSM_T, bsz, per_b, width)
    y4 = pl.pallas_call(
        _s5_body,
        grid=(npack, nblk),
        in_specs=[pl.BlockSpec((SSM_T, bsz, cb, LANES), lambda j, i: (0, 0, i, j)),
                  _resident((1, kdim, kdim), pk),
                  _resident((1, kdim, ns2), pk),
                  _resident((1, ns2, kdim), pk),
                  pl.BlockSpec((1, 1, ns2), pk),
                  pl.BlockSpec((1, 1, LANES), pk)],
        out_specs=pl.BlockSpec((SSM_T, bsz, cb, LANES), lambda j, i: (0, 0, i, j)),
        out_shape=jax.ShapeDtypeStruct((SSM_T, bsz, per_b, width), F32),
        scratch_shapes=[pltpu.VMEM((bsz, ns2), F32), pltpu.VMEM((ns2 // LANES, bsz * cb, LANES), F32),
                        pltpu.VMEM((ns2 // LANES, bsz * cb, LANES), F32)],
        compiler_params=_cparams(("arbitrary", "arbitrary")),
        name="s5_mixer",
    )(u4, wt, wa, wc, lam_t, dpack)
    return y4.reshape(SSM_T, nchunk, width)


def s5_operators(a_re, a_im, log_dt, b_re, b_im, c_re, c_im, dskip):
    g, p = a_re.shape
    cg = b_re.shape[2]
    npack = g // SSM_PACK
    ns2 = SSM_PACK * 2 * p
    t = SSM_T
    lam = lax.complex(a_re, a_im)
    dt = jnp.exp(log_dt)[:, None]
    lam_bar = jnp.exp(lam * dt)
    b_bar = ((lam_bar - 1.0) / lam)[..., None] * lax.complex(b_re, b_im)
    c_mat = lax.complex(c_re, c_im)
    kpow = jnp.arange(t + 1, dtype=F32)[:, None, None]
    pows = jnp.exp((lam * dt)[None] * kpow)
    lane_c = jnp.arange(LANES) % cg
    lane_g = jnp.arange(LANES) // cg
    tile = (lane_c[None, :] == jnp.arange(cg)[:, None]).astype(F32)
    diag = lane_g[:, None] == lane_g[None, :]
    kern = jnp.real(jnp.einsum('gcp,kgp,gpd->kgdc', c_mat, pows[:t], b_bar, precision=HIGHEST))
    kfull = jnp.where(diag, jnp.dot(kern.reshape(t, npack, LANES, cg), tile, precision=HIGHEST), 0.0)
    kft = kfull.transpose(1, 2, 0, 3)
    wt = jnp.stack([jnp.pad(kft[:, :, :t - s].reshape(npack, LANES, (t - s) * LANES),
                            ((0, 0), (0, 0), (s * LANES, 0))) for s in range(t)], axis=1)
    wt = wt.reshape(npack, t * LANES, t * LANES)
    ns = SSM_PACK * p
    col_g = jnp.arange(ns) // p
    tile_p = (jnp.arange(ns)[None, :] % p == jnp.arange(p)[:, None]).astype(F32)
    wa_c = pows[t - 1 - jnp.arange(t)][:, :, :, None] * b_bar[None]
    wa_c = wa_c.transpose(0, 1, 3, 2).reshape(t, npack, LANES, p)
    own_a = lane_g[:, None] == col_g[None, :]
    wa = jnp.concatenate([jnp.where(own_a, jnp.dot(part, tile_p, precision=HIGHEST), 0.0)
                          for part in (jnp.real(wa_c), jnp.imag(wa_c))], axis=-1)
    wa = wa.transpose(1, 0, 2, 3).reshape(npack, t * LANES, ns2)
    wc_c = c_mat[None] * pows[1:t + 1][:, :, None, :]
    wc_c = wc_c.transpose(0, 1, 3, 2).reshape(t, npack, ns, cg)
    cc = jnp.concatenate([jnp.real(wc_c), -jnp.imag(wc_c)], axis=2)
    own_c = (jnp.arange(ns2) % ns // p)[:, None] == lane_g[None, :]
    wcf = jnp.where(own_c, jnp.dot(cc, tile, precision=HIGHEST), 0.0)
    wc = wcf.transpose(1, 2, 0, 3).reshape(npack, ns2, t * LANES)
    lam_t = pows[t].reshape(npack, 1, ns)
    lam_t = jnp.concatenate([jnp.real(lam_t), jnp.imag(lam_t)], axis=-1)
    dpack = dskip.reshape(npack, 1, LANES)
    return wt.astype(BF16), wa.astype(BF16), wc.astype(BF16), lam_t, dpack


def _glu_out_body(y_ref, wg_ref, bg_ref, wo_ref, h_ref, gm_ref, lg_ref, lb_ref, o_ref, ytm_ref):
    cb = y_ref.shape[1]
    ncol = y_ref.shape[2] // LANES
    for t in range(SSM_T):
        for c in range(ncol):
            ytm_ref[c, pl.ds(t, cb, stride=SSM_T), :] = y_ref[t, :, c * LANES:(c + 1) * LANES]
    y = jnp.concatenate([ytm_ref[c] for c in range(ncol)], axis=1)
    y_bf = y.astype(BF16)
    gate = jnp.dot(y_bf, wg_ref[...], preferred_element_type=F32) + bg_ref[...]
    z = (y * jax.nn.sigmoid(gate)).astype(BF16)
    out = jnp.dot(z, wo_ref[...], preferred_element_type=F32)
    zz = DN_ALPHA * h_ref[...] + (1.0 + gm_ref[0]) * out
    o_ref[...] = _layer_norm(zz, lg_ref[...], lb_ref[...])


def glu_out(y2, wg_bf, bg, wo_bf, h, gm, ln_g, ln_b, seq):
    n, d = h.shape
    width = y2.shape[2]
    tm = min(256, seq)
    cb = tm // SSM_T
    per_b = seq // tm
    row = lambda i: (i, 0)
    const = lambda i: (0, 0)
    return pl.pallas_call(
        _glu_out_body,
        grid=(n // tm,),
        in_specs=[pl.BlockSpec((SSM_T, cb, width), lambda i: (0, i, 0)),
                  _resident((width, width), const),
                  pl.BlockSpec((1, width), const),
                  _resident((width, d), const),
                  pl.BlockSpec((tm, d), row),
                  pl.BlockSpec((1, 1, d), lambda i: (i // per_b, 0, 0)),
                  pl.BlockSpec((1, d), const),
                  pl.BlockSpec((1, d), const)],
        out_specs=pl.BlockSpec((tm, d), row),
        out_shape=jax.ShapeDtypeStruct((n, d), F32),
        scratch_shapes=[pltpu.VMEM((width // LANES, tm, LANES), F32)],
        compiler_params=_cparams(("arbitrary",)),
        name="glu_out",
    )(y2, wg_bf, bg.reshape(1, width), wo_bf, h, gm, ln_g.reshape(1, d), ln_b.reshape(1, d))


def kernel(x, c, positions, ada_w, ada_b, ln_g, ln_b, ab_in_w, ab_in_b, ab_sinks, ab_gnorm_w, ab_out_w, ab_out_b, hgrn_lb_logits, c_in_w, c_A_re, c_A_im, c_log_dt, c_B_re, c_B_im, c_C_re, c_C_im, c_D, c_glu_w, c_glu_b, c_out_w, router_w, router_b, exp_w1, exp_b1, exp_w2, exp_b2):
    bsz, seq, d = x.shape
    n = bsz * seq
    depth = ada_w.shape[0]
    mod = ada_mod(c, ada_w, ada_b)
    lb_all = jnp.cumsum(jax.nn.softmax(hgrn_lb_logits.astype(F32), axis=0), axis=0)
    half = A_HEAD_DIM // 2
    inv_freq = ROPE_THETA ** (-jnp.arange(half, dtype=F32) / half)
    invf = jnp.tile(inv_freq, LANES // half).reshape(1, LANES)
    pos_col = positions.reshape(n, 1)
    ne_all = depth * exp_w1.shape[1]
    w1_all = exp_w1.astype(BF16).reshape(ne_all, d, exp_w1.shape[3])
    w2_all = exp_w2.astype(BF16).reshape(ne_all, exp_w2.shape[2], d)
    b1_all = exp_b1.reshape(ne_all, exp_b1.shape[2])
    b2_all = exp_b2.reshape(ne_all, d)
    h = x.reshape(n, d)
    for layer in range(depth):
        m6 = mod[layer].reshape(bsz, 6, 1, d)
        sh_m, sc_m, g_m, sh_f, sc_f, g_f = [m6[:, k] for k in range(6)]
        i = layer // 2
        if layer % 2 == 0:
            proj = inproj_ab(h, sc_m, sh_m, ab_in_w[i].astype(BF16), ab_in_b[i], seq)
            oa = swa_attention(proj, pos_col, invf, ab_sinks[i], bsz, seq)
            ob = hgrn2(proj, lb_all[layer], ab_gnorm_w[i], bsz, seq)
            h = outproj_ab(oa, ob, ab_out_w[i].astype(BF16), ab_out_b[i], h, g_m,
                           ln_g[layer, 0], ln_b[layer, 0], seq)
        else:
            wt, wa, wc, lam_t, dpack = s5_operators(c_A_re[i], c_A_im[i], c_log_dt[i], c_B_re[i], c_B_im[i],
                                                    c_C_re[i], c_C_im[i], c_D[i])
            u2 = inproj_c(h, sc_m, sh_m, c_in_w[i].astype(BF16), seq)
            y2 = s5_mixer(u2, wt, wa, wc, lam_t, dpack, bsz, seq)
            h = glu_out(y2, c_glu_w[i].astype(BF16), c_glu_b[i], c_out_w[i].astype(BF16), h, g_m,
                        ln_g[layer, 0], ln_b[layer, 0], seq)
        h = moe_layer(h, sc_f, sh_f, g_f, ln_g[layer, 1], ln_b[layer, 1], router_w[layer], router_b[layer],
                      layer, w1_all, b1_all, w2_all, b2_all, seq)
    return h.reshape(bsz, seq, d)
```

```python
import functools

import jax
import jax.numpy as jnp
from jax import lax
from jax.experimental import pallas as pl
from jax.experimental.pallas import tpu as pltpu

F32 = jnp.float32
BF16 = jnp.bfloat16
I32 = jnp.int32
HIGHEST = lax.Precision.HIGHEST

DEPTH = 2
A_HEADS = 16
A_KV_HEADS = 2
A_HEAD_DIM = 64
WINDOW = 128
ROPE_THETA = 10000.0
B_HEADS = 8
B_DIM = 128
GLA_CHUNK = 64
C_GROUP = 16
C_STATE = 64
N_EXPERTS = 32
TOP_K = 4
SWIGLU_LIMIT = 7.0
SWIGLU_ALPHA = 1.702
DN_ALPHA = (2 * DEPTH) ** 0.25
LN_EPS = 1e-5
RMS_EPS = 1e-6

LANES = 128
VMEM_LIMIT = 56 * 1024 * 1024

ATT_BLK = 128
HGRN_T = 512
MOE_BLK = 256
SSM_T = 16
SSM_PACK = LANES // C_GROUP
S5_COL_BLOCKS = 4
DMA_GROUP = 8
COMBINE_GROUP = 32
NEG = -1e30


def _row_pitch(d):
    return d // LANES + 1


def _cparams(sem):
    return pltpu.CompilerParams(dimension_semantics=sem, vmem_limit_bytes=VMEM_LIMIT)


def _resident(block_shape, index_map):
    return pl.BlockSpec(block_shape, index_map, pipeline_mode=pl.Buffered(1))


def _layer_norm(z, g, b):
    mu = jnp.mean(z, axis=-1, keepdims=True)
    zc = z - mu
    var = jnp.mean(zc * zc, axis=-1, keepdims=True)
    return zc * lax.rsqrt(var + LN_EPS) * g + b


def _ada_body(ct_ref, w_ref, b_ref, o_ref):
    ct = ct_ref[...]
    cond = ct * jax.nn.sigmoid(ct)
    w = w_ref[0]
    rows = [jnp.sum(w * cond[:, b:b + 1], axis=0, keepdims=True) for b in range(ct.shape[1])]
    o_ref[0] = jnp.concatenate(rows, axis=0) + b_ref[0]


def ada_mod(c, ada_w, ada_b):
    depth, d, n6 = ada_w.shape
    bsz = c.shape[0]
    tn = 1536
    return pl.pallas_call(
        _ada_body,
        grid=(depth, n6 // tn),
        in_specs=[pl.BlockSpec((d, bsz), lambda l, j: (0, 0)),
                  pl.BlockSpec((1, d, tn), lambda l, j: (l, 0, j)),
                  pl.BlockSpec((1, 1, tn), lambda l, j: (l, 0, j))],
        out_specs=pl.BlockSpec((1, bsz, tn), lambda l, j: (l, 0, j)),
        out_shape=jax.ShapeDtypeStruct((depth, bsz, n6), F32),
        compiler_params=_cparams(("arbitrary", "arbitrary")),
        name="ada_mod",
    )(c.T, ada_w, ada_b.reshape(depth, 1, n6))


def _inproj_ab_body(x_ref, sc_ref, sh_ref, w_ref, b_ref, o_ref, xb_ref):
    @pl.when(pl.program_id(1) == 0)
    def _():
        xb_ref[...] = (x_ref[...] * (1.0 + sc_ref[0]) + sh_ref[0]).astype(BF16)

    o_ref[...] = jnp.dot(xb_ref[...], w_ref[...], preferred_element_type=F32) + b_ref[...]


def inproj_ab(h, sc, sh, w_bf, b, seq):
    n, d = h.shape
    nc = w_bf.shape[1]
    tm = min(1024, seq)
    tn = 768
    per_b = seq // tm
    return pl.pallas_call(
        _inproj_ab_body,
        grid=(n // tm, nc // tn),
        in_specs=[pl.BlockSpec((tm, d), lambda i, j: (i, 0)),
                  pl.BlockSpec((1, 1, d), lambda i, j: (i // per_b, 0, 0)),
                  pl.BlockSpec((1, 1, d), lambda i, j: (i // per_b, 0, 0)),
                  pl.BlockSpec((d, tn), lambda i, j: (0, j)),
                  pl.BlockSpec((1, tn), lambda i, j: (0, j))],
        out_specs=pl.BlockSpec((tm, tn), lambda i, j: (i, j)),
        out_shape=jax.ShapeDtypeStruct((n, nc), F32),
        scratch_shapes=[pltpu.VMEM((tm, d), BF16)],
        compiler_params=_cparams(("arbitrary", "arbitrary")),
        name="inproj_ab",
    )(h, sc, sh, w_bf, b.reshape(1, nc))


def _attn_body(sink_ref, q_ref, kc_ref, vc_ref, vp_ref, pc_ref, invf_ref, o_ref, kprev_ref):
    blk = ATT_BLK
    half = A_HEAD_DIM // 2
    grp = A_HEADS // A_KV_HEADS
    i = pl.program_id(1)
    lane = lax.broadcasted_iota(I32, (blk, LANES), 1)
    first_half = (lane % A_HEAD_DIM) < half
    low_head = lane < A_HEAD_DIM
    invf = invf_ref[...]

    def cos_sin(pos_col):
        ang = pos_col.astype(F32) * invf
        s = jnp.sin(ang)
        return jnp.cos(ang), jnp.where(first_half, -s, s)

    def rope(x, cs):
        partner = jnp.where(first_half, pltpu.roll(x, LANES - half, 1), pltpu.roll(x, half, 1))
        return x * cs[0] + partner * cs[1]

    @pl.when(i == 0)
    def _():
        kprev_ref[...] = jnp.zeros_like(kprev_ref)

    cs_c = cos_sin(pc_ref[...])
    k_cur = rope(kc_ref[...], cs_c)
    k2 = jnp.concatenate([kprev_ref[...], k_cur], axis=0)
    kprev_ref[...] = k_cur
    v2 = jnp.concatenate([vp_ref[...], vc_ref[...]], axis=0)
    lane2 = lax.broadcasted_iota(I32, (2 * blk, LANES), 1)
    k2r = pltpu.roll(k2, A_HEAD_DIM, 1)
    v2r = pltpu.roll(v2, A_HEAD_DIM, 1)

    rows = lax.broadcasted_iota(I32, (grp * blk, 2 * blk), 0) % blk
    cols = lax.broadcasted_iota(I32, (grp * blk, 2 * blk), 1)
    valid = (cols > rows) & (cols <= rows + WINDOW) & ((cols >= blk) | (i > 0))

    q_chunks = [rope(q_ref[:, c * LANES:(c + 1) * LANES], cs_c) * (A_HEAD_DIM ** -0.5)
                for c in range(A_HEADS * A_HEAD_DIM // LANES)]

    ones_bf = jnp.ones((2 * blk, LANES), BF16)
    for g in range(A_KV_HEADS):
        own = (lane2 < A_HEAD_DIM) if g == 0 else (lane2 >= A_HEAD_DIM)
        kg = jnp.where(own, k2, k2r).astype(BF16)
        vg = jnp.where(own, v2, v2r).astype(BF16)
        pieces = []
        sinks = []
        for r in range(grp):
            hd = g * grp + r
            qc = q_chunks[hd // 2]
            keep = low_head if hd % 2 == 0 else jnp.logical_not(low_head)
            pieces.append(jnp.where(keep, qc, 0.0).astype(BF16))
            sinks.append(jnp.full((blk, 1), sink_ref[hd], F32))
        qs = jnp.concatenate(pieces, axis=0)
        sink = jnp.concatenate(sinks, axis=0)
        s = lax.dot_general(qs, kg, (((1,), (1,)), ((), ())), preferred_element_type=F32)
        s = jnp.where(valid, s, NEG)
        m = jnp.maximum(jnp.max(s, axis=-1, keepdims=True), sink)
        p = jnp.exp(s - m).astype(BF16)
        den = jnp.dot(p, ones_bf, preferred_element_type=F32) + jnp.exp(sink - m)
        o = jnp.dot(p, vg, preferred_element_type=F32) / den
        for r2 in range(grp // 2):
            c = (g * grp) // 2 + r2
            ev = o[(2 * r2) * blk:(2 * r2 + 1) * blk]
            od = o[(2 * r2 + 1) * blk:(2 * r2 + 2) * blk]
            o_ref[:, c * LANES:(c + 1) * LANES] = jnp.where(low_head, ev, od).astype(o_ref.dtype)


def swa_attention(proj, pos_col, invf, sinks, bsz, seq):
    n = proj.shape[0]
    blk = ATT_BLK
    nb = seq // blk
    aq = A_HEADS * A_HEAD_DIM
    kcol = aq // LANES
    vcol = kcol + 1

    def cur(b, i, s):
        return b * nb + i

    def prev(b, i, s):
        return b * nb + jnp.maximum(i - 1, 0)

    grid_spec = pltpu.PrefetchScalarGridSpec(
        num_scalar_prefetch=1,
        grid=(bsz, nb),
        in_specs=[pl.BlockSpec((blk, aq), lambda b, i, s: (cur(b, i, s), 0)),
                  pl.BlockSpec((blk, LANES), lambda b, i, s: (cur(b, i, s), kcol)),
                  pl.BlockSpec((blk, LANES), lambda b, i, s: (cur(b, i, s), vcol)),
                  pl.BlockSpec((blk, LANES), lambda b, i, s: (prev(b, i, s), vcol)),
                  pl.BlockSpec((blk, 1), lambda b, i, s: (cur(b, i, s), 0)),
                  pl.BlockSpec((1, LANES), lambda b, i, s: (0, 0))],
        out_specs=pl.BlockSpec((blk, aq), lambda b, i, s: (cur(b, i, s), 0)),
        scratch_shapes=[pltpu.VMEM((blk, LANES), F32)],
    )
    return pl.pallas_call(
        _attn_body,
        grid_spec=grid_spec,
        out_shape=jax.ShapeDtypeStruct((n, aq), BF16),
        compiler_params=_cparams(("arbitrary", "arbitrary")),
        name="swa_attention",
    )(sinks, proj, proj, proj, proj, pos_col, invf)


def _hgrn_body(q_ref, f_ref, i_ref, g_ref, lb_ref, gw_ref, o_ref, st_ref):
    L = GLA_CHUNK

    @pl.when(pl.program_id(2) == 0)
    def _():
        st_ref[...] = jnp.zeros_like(st_ref)

    T = q_ref.shape[0]
    nch = T // L
    lb = lb_ref[0]
    gw = gw_ref[0]
    nt = (((1,), (1,)), ((), ()))

    f = lb + (1.0 - lb) * jax.nn.sigmoid(f_ref[...])
    kk = 1.0 - f
    qb = q_ref[...]
    q = qb * jax.nn.sigmoid(qb)
    v = i_ref[...]
    pos = lax.broadcasted_iota(I32, (T, B_DIM), 0) % L
    b = jnp.log(f)
    sh = 1
    while sh < L:
        b = b + jnp.where(pos >= sh, pltpu.roll(b, sh, 0), 0.0)
        sh *= 2

    def per_chunk(row):
        return jnp.concatenate(
            [jnp.broadcast_to(b[c * L + row:c * L + row + 1], (L, B_DIM)) for c in range(nch)], axis=0)

    b_mid = per_chunk(L // 2)
    b_last = per_chunk(L - 1)
    e_mid = jnp.exp(b - b_mid)
    qa = (q * e_mid).astype(BF16)
    ka = (kk * jnp.exp(b_mid - b)).astype(BF16)
    qe = (q * jnp.exp(b)).astype(BF16)
    kd = (kk * jnp.exp(b_last - b)).astype(BF16)
    att = lax.dot_general(qa, ka, nt, preferred_element_type=F32)
    r = lax.broadcasted_iota(I32, (T, T), 0)
    c = lax.broadcasted_iota(I32, (T, T), 1)
    att = jnp.where((r >= c) & (r // L == c // L), att, 0.0)
    o_intra = jnp.dot(att.astype(BF16), v.astype(BF16), preferred_element_type=F32)
    vt = v.T.astype(BF16)
    st = st_ref[...]
    o_inter = []
    for ci in range(nch):
        sl = slice(ci * L, (ci + 1) * L)
        o_inter.append(lax.dot_general(qe[sl], st.astype(BF16), nt, preferred_element_type=F32))
        u_t = jnp.dot(vt[:, sl], kd[sl], preferred_element_type=F32)
        st = jnp.exp(b[ci * L + L - 1:ci * L + L]) * st + u_t
    st_ref[...] = st
    o = o_intra + jnp.concatenate(o_inter, axis=0)
    o = o * lax.rsqrt(jnp.mean(o * o, axis=-1, keepdims=True) + RMS_EPS) * gw
    gb = g_ref[...]
    o_ref[...] = (o * (gb * jax.nn.sigmoid(gb))).astype(o_ref.dtype)


def hgrn2(proj, lb, gnorm_w, bsz, seq):
    n = proj.shape[0]
    t = min(HGRN_T, seq)
    nt = seq // t
    base = (A_HEADS + 2 * A_KV_HEADS) * A_HEAD_DIM // LANES
    h = B_HEADS

    def col(k):
        return lambda b, hd, i: (b * nt + i, base + k * h + hd)

    return pl.pallas_call(
        _hgrn_body,
        grid=(bsz, h, nt),
        in_specs=[pl.BlockSpec((t, LANES), col(0)),
                  pl.BlockSpec((t, LANES), col(1)),
                  pl.BlockSpec((t, LANES), col(2)),
                  pl.BlockSpec((t, LANES), col(3)),
                  pl.BlockSpec((1, 1, B_DIM), lambda b, hd, i: (hd, 0, 0)),
                  pl.BlockSpec((1, 1, B_DIM), lambda b, hd, i: (hd, 0, 0))],
        out_specs=pl.BlockSpec((t, LANES), lambda b, hd, i: (b * nt + i, hd)),
        out_shape=jax.ShapeDtypeStruct((n, h * B_DIM), BF16),
        scratch_shapes=[pltpu.VMEM((B_DIM, B_DIM), F32)],
        compiler_params=_cparams(("arbitrary", "arbitrary", "arbitrary")),
        name="hgrn2",
    )(proj, proj, proj, proj, lb.reshape(h, 1, B_DIM), gnorm_w.reshape(h, 1, B_DIM))


def _outproj_ab_body(oa_ref, ob_ref, w_ref, b_ref, h_ref, gm_ref, lg_ref, lb_ref, o_ref):
    ka = oa_ref.shape[1]
    y = jnp.dot(oa_ref[...], w_ref[:ka, :], preferred_element_type=F32)
    y = y + jnp.dot(ob_ref[...], w_ref[ka:, :], preferred_element_type=F32) + b_ref[...]
    z = DN_ALPHA * h_ref[...] + (1.0 + gm_ref[0]) * y
    o_ref[...] = _layer_norm(z, lg_ref[...], lb_ref[...])


def outproj_ab(oa, ob, w_bf, b, h, gm, ln_g, ln_b, seq):
    n, d = h.shape
    tm = min(512, seq)
    per_b = seq // tm
    ka, kb = oa.shape[1], ob.shape[1]
    row = lambda i: (i, 0)
    const = lambda i: (0, 0)
    return pl.pallas_call(
        _outproj_ab_body,
        grid=(n // tm,),
        in_specs=[pl.BlockSpec((tm, ka), row),
                  pl.BlockSpec((tm, kb), row),
                  _resident((ka + kb, d), const),
                  pl.BlockSpec((1, d), const),
                  pl.BlockSpec((tm, d), row),
                  pl.BlockSpec((1, 1, d), lambda i: (i // per_b, 0, 0)),
                  pl.BlockSpec((1, d), const),
                  pl.BlockSpec((1, d), const)],
        out_specs=pl.BlockSpec((tm, d), row),
        out_shape=jax.ShapeDtypeStruct((n, d), F32),
        compiler_params=_cparams(("arbitrary",)),
        name="outproj_ab",
    )(oa, ob, w_bf, b.reshape(1, d), h, gm, ln_g.reshape(1, d), ln_b.reshape(1, d))


def _router_body(h_ref, sc_ref, sh_ref, rw_ref, rb_ref, meta_ref, cnt_ref, carry_ref, rwh_ref, rwl_ref):
    tm = h_ref.shape[0]

    @pl.when(pl.program_id(0) == 0)
    def _():
        carry_ref[...] = jnp.zeros_like(carry_ref)
        w = rw_ref[...]
        w_hi = w.astype(BF16)
        rwh_ref[...] = w_hi
        rwl_ref[...] = (w - w_hi.astype(F32)).astype(BF16)

    xin = h_ref[...] * (1.0 + sc_ref[0]) + sh_ref[0]
    x_hi = xin.astype(BF16)
    x_lo = (xin - x_hi.astype(F32)).astype(BF16)
    logits = (jnp.dot(x_hi, rwh_ref[...], preferred_element_type=F32)
              + (jnp.dot(x_hi, rwl_ref[...], preferred_element_type=F32)
                 + jnp.dot(x_lo, rwh_ref[...], preferred_element_type=F32))) + rb_ref[...]
    lane = lax.broadcasted_iota(I32, (tm, LANES), 1)
    l = logits
    vals, idxs = [], []
    sel = jnp.zeros((tm, LANES), F32)
    for _ in range(TOP_K):
        m = jnp.max(l, axis=-1, keepdims=True)
        idx = jnp.min(jnp.where(l == m, lane, LANES), axis=-1, keepdims=True)
        hit = lane == idx
        vals.append(m)
        idxs.append(idx)
        sel = jnp.where(hit, 1.0, sel)
        l = jnp.where(hit, -jnp.inf, l)
    es = [jnp.exp(v - vals[0]) for v in vals]
    den = es[0] + es[1] + es[2] + es[3]
    r = lax.broadcasted_iota(I32, (tm, tm), 0)
    c = lax.broadcasted_iota(I32, (tm, tm), 1)
    tril = (r >= c).astype(BF16)
    incl = jnp.dot(tril, sel.astype(BF16), preferred_element_type=F32)
    carry = carry_ref[...]
    excl = incl - sel + carry
    meta = jnp.zeros((tm, LANES), F32)
    for k in range(TOP_K):
        rank = jnp.sum(jnp.where(lane == idxs[k], excl, 0.0), axis=-1, keepdims=True)
        meta = jnp.where(lane == k, idxs[k].astype(F32), meta)
        meta = jnp.where(lane == TOP_K + k, es[k] / den, meta)
        meta = jnp.where(lane == 2 * TOP_K + k, rank, meta)
    meta_ref[...] = meta
    carry = carry + incl[tm - 1:tm, :]
    carry_ref[...] = carry
    cnt_ref[...] = jnp.broadcast_to(carry, cnt_ref.shape)


def moe_router(h, sc, sh, rw_pad, rb_pad, seq):
    n, d = h.shape
    tm = min(512, seq)
    per_b = seq // tm
    return pl.pallas_call(
        _router_body,
        grid=(n // tm,),
        in_specs=[pl.BlockSpec((tm, d), lambda i: (i, 0)),
                  pl.BlockSpec((1, 1, d), lambda i: (i // per_b, 0, 0)),
                  pl.BlockSpec((1, 1, d), lambda i: (i // per_b, 0, 0)),
                  pl.BlockSpec((d, LANES), lambda i: (0, 0)),
                  pl.BlockSpec((1, LANES), lambda i: (0, 0))],
        out_specs=[pl.BlockSpec((tm, LANES), lambda i: (i, 0)),
                   pl.BlockSpec((8, LANES), lambda i: (0, 0))],
        out_shape=[jax.ShapeDtypeStruct((n, LANES), F32),
                   jax.ShapeDtypeStruct((8, LANES), F32)],
        scratch_shapes=[pltpu.VMEM((1, LANES), F32), pltpu.VMEM((d, LANES), BF16), pltpu.VMEM((d, LANES), BF16)],
        compiler_params=_cparams(("arbitrary",)),
        name="moe_router",
    )(h, sc, sh, rw_pad, rb_pad)


def _dispatch_body(pad_ref, dest_ref, h_ref, sc_ref, sh_ref, xs_hbm, buf_ref, zero_ref, sem, zsem):
    tm, d = h_ref.shape
    rt = d // LANES
    pitch = _row_pitch(d)
    i = pl.program_id(0)
    last = pl.num_programs(0) - 1
    slot = i % 2
    n_pad = pad_ref.shape[0]

    def wait_slot(s):
        for _ in range(TOP_K):
            pltpu.make_async_copy(buf_ref.at[s], xs_hbm.at[pl.ds(0, tm * pitch)], sem.at[s]).wait()

    @pl.when(i == 0)
    def _():
        zero_ref[...] = jnp.zeros_like(zero_ref)
        buf_ref[...] = jnp.zeros_like(buf_ref)

        def zfill(q2, carry):
            for pr in range(2):
                dst = pad_ref[q2 * 2 + pr] * pitch
                pltpu.make_async_copy(zero_ref, xs_hbm.at[pl.ds(dst, pitch)], zsem.at[0]).start(priority=pr)
            return carry

        lax.fori_loop(0, n_pad // 2, zfill, 0)

    @pl.when(i >= 2)
    def _():
        wait_slot(slot)

    xin = h_ref[...] * (1.0 + sc_ref[0]) + sh_ref[0]
    for s in range(rt):
        buf_ref[slot, pl.ds(s, tm, stride=pitch), :] = xin[:, s * LANES:(s + 1) * LANES]

    def issue(g, carry):
        for rr in range(DMA_GROUP):
            r = g * DMA_GROUP + rr
            src = buf_ref.at[slot, pl.ds(r * pitch, pitch)]
            for k in range(TOP_K):
                dst = dest_ref[0, 0, r * TOP_K + k] * pitch
                pltpu.make_async_copy(src, xs_hbm.at[pl.ds(dst, pitch)], sem.at[slot]).start(priority=k % 2)
        return carry

    lax.fori_loop(0, tm // DMA_GROUP, issue, 0)

    @pl.when(i == last)
    def _():
        wait_slot(slot)

        @pl.when(i >= 1)
        def _():
            wait_slot(1 - slot)

        for _ in range(n_pad // tm):
            pltpu.make_async_copy(buf_ref.at[0], xs_hbm.at[pl.ds(0, tm * pitch)], zsem.at[0]).wait()


def moe_dispatch(h, sc, sh, dest, pad_dest, n_rows, seq):
    n, d = h.shape
    tm = min(256, seq)
    per_b = seq // tm
    pitch = _row_pitch(d)
    assert pad_dest.shape[0] % tm == 0
    grid_spec = pltpu.PrefetchScalarGridSpec(
        num_scalar_prefetch=1,
        grid=(n // tm,),
        in_specs=[pl.BlockSpec((1, 1, tm * TOP_K), lambda i, p: (i, 0, 0), memory_space=pltpu.SMEM),
                  pl.BlockSpec((tm, d), lambda i, p: (i, 0)),
                  pl.BlockSpec((1, 1, d), lambda i, p: (i // per_b, 0, 0)),
                  pl.BlockSpec((1, 1, d), lambda i, p: (i // per_b, 0, 0))],
        out_specs=pl.BlockSpec(memory_space=pl.ANY),
        scratch_shapes=[pltpu.VMEM((2, tm * pitch, LANES), F32), pltpu.VMEM((pitch, LANES), F32),
                        pltpu.SemaphoreType.DMA((2,)), pltpu.SemaphoreType.DMA((1,))],
    )
    return pl.pallas_call(
        _dispatch_body,
        grid_spec=grid_spec,
        out_shape=jax.ShapeDtypeStruct((n_rows * pitch, LANES), F32),
        compiler_params=_cparams(("arbitrary",)),
        name="moe_dispatch",
    )(pad_dest, dest.reshape(n // tm, 1, tm * TOP_K), h, sc, sh)


def _expert_body(be_ref, nu_ref, xs_ref, w1_ref, b1_ref, w2_ref, b2_ref, y_ref):
    j = pl.program_id(0)
    ff = w2_ref.shape[1]
    d = w1_ref.shape[1]
    rt = d // LANES
    pitch = _row_pitch(d)
    blk = xs_ref.shape[0] // pitch

    @pl.when(j < nu_ref[0])
    def _():
        x = jnp.concatenate([xs_ref[pl.ds(s, blk, stride=pitch), :].astype(BF16) for s in range(rt)], axis=1)
        hid = jnp.dot(x, w1_ref[0], preferred_element_type=F32) + b1_ref[0]
        gate = jnp.minimum(hid[:, :ff], SWIGLU_LIMIT)
        lin = jnp.clip(hid[:, ff:], -SWIGLU_LIMIT, SWIGLU_LIMIT)
        act = gate * jax.nn.sigmoid(SWIGLU_ALPHA * gate) * (lin + 1.0)
        y = jnp.dot(act.astype(BF16), w2_ref[0], preferred_element_type=F32) + b2_ref[0]
        for s in range(rt):
            y_ref[pl.ds(s, blk, stride=pitch), :] = y[:, s * LANES:(s + 1) * LANES]
        for s in range(rt, pitch):
            y_ref[pl.ds(s, blk, stride=pitch), :] = jnp.zeros((blk, LANES), F32)

    @pl.when(j >= nu_ref[0])
    def _():
        y_ref[...] = jnp.zeros_like(y_ref)


def moe_experts(xs, blk_exp, n_used, w1_bf, b1, w2_bf, b2):
    ne, d, ff2 = w1_bf.shape
    pitch = _row_pitch(d)
    n_rows = xs.shape[0] // pitch
    ff = ff2 // 2
    nblk = n_rows // MOE_BLK
    grid_spec = pltpu.PrefetchScalarGridSpec(
        num_scalar_prefetch=2,
        grid=(nblk,),
        in_specs=[pl.BlockSpec((MOE_BLK * pitch, LANES), lambda j, be, nu: (jnp.minimum(j, nu[0] - 1), 0)),
                  pl.BlockSpec((1, d, ff2), lambda j, be, nu: (be[j], 0, 0)),
                  pl.BlockSpec((1, 1, ff2), lambda j, be, nu: (be[j], 0, 0)),
                  pl.BlockSpec((1, ff, d), lambda j, be, nu: (be[j], 0, 0)),
                  pl.BlockSpec((1, 1, d), lambda j, be, nu: (be[j], 0, 0))],
        out_specs=pl.BlockSpec((MOE_BLK * pitch, LANES), lambda j, be, nu: (j, 0)),
    )
    return pl.pallas_call(
        _expert_body,
        grid_spec=grid_spec,
        out_shape=jax.ShapeDtypeStruct((n_rows * pitch, LANES), F32),
        compiler_params=_cparams(("arbitrary",)),
        name="moe_experts",
    )(blk_exp, n_used, xs, w1_bf, b1.reshape(ne, 1, ff2), w2_bf, b2.reshape(ne, 1, d))


def _combine_body(dc_ref, dn_ref, meta_ref, h_ref, gf_ref, lg_ref, lb_ref, y_hbm, o_ref, buf_ref, sem):
    tm, d = h_ref.shape
    rt = d // LANES
    pitch = _row_pitch(d)
    grp = COMBINE_GROUP
    i = pl.program_id(0)
    n = pl.num_programs(0)
    slot = i % 2

    def issue_rows(dref, s, r0):
        for rr in range(grp):
            r = r0 + rr
            for k in range(TOP_K):
                src = dref[0, 0, r * TOP_K + k] * pitch
                pltpu.make_async_copy(y_hbm.at[pl.ds(src, rt)], buf_ref.at[s, k, pl.ds(r * pitch, rt)],
                                      sem.at[s]).start(priority=k % 2)

    @pl.when(i == 0)
    def _():
        def body(g, carry):
            issue_rows(dc_ref, 0, g * grp)
            return carry
        lax.fori_loop(0, tm // grp, body, 0)

    for k in range(TOP_K):
        pltpu.make_async_copy(y_hbm.at[pl.ds(0, tm * rt)], buf_ref.at[slot, k, pl.ds(0, tm * rt)],
                              sem.at[slot]).wait()

    def group(g, carry, prefetch):
        r0 = pl.multiple_of(g * grp, grp)
        if prefetch:
            issue_rows(dn_ref, 1 - slot, r0)
        meta = meta_ref[pl.ds(r0, grp), :]
        gates = [jnp.broadcast_to(meta[:, TOP_K + k:TOP_K + k + 1], (grp, LANES)) for k in range(TOP_K)]
        for s in range(rt):
            a = gates[0] * buf_ref[slot, 0, pl.ds(r0 * pitch + s, grp, stride=pitch), :]
            for k in range(1, TOP_K):
                a = a + gates[k] * buf_ref[slot, k, pl.ds(r0 * pitch + s, grp, stride=pitch), :]
            cs = slice(s * LANES, (s + 1) * LANES)
            o_ref[pl.ds(r0, grp), cs] = DN_ALPHA * h_ref[pl.ds(r0, grp), cs] + (1.0 + gf_ref[0, :, cs]) * a
        return carry

    @pl.when(i + 1 < n)
    def _():
        lax.fori_loop(0, tm // grp, functools.partial(group, prefetch=True), 0)

    @pl.when(i + 1 == n)
    def _():
        lax.fori_loop(0, tm // grp, functools.partial(group, prefetch=False), 0)

    o_ref[...] = _layer_norm(o_ref[...], lg_ref[...], lb_ref[...])


def moe_combine(y, dest, meta, h, gf, ln_g, ln_b, seq):
    n, d = h.shape
    tm = min(256, seq)
    per_b = seq // tm
    nblk = n // tm
    dest3 = dest.reshape(nblk, 1, tm * TOP_K)
    row = lambda i: (i, 0)
    const = lambda i: (0, 0)
    return pl.pallas_call(
        _combine_body,
        grid=(nblk,),
        in_specs=[pl.BlockSpec((1, 1, tm * TOP_K), lambda i: (i, 0, 0), memory_space=pltpu.SMEM),
                  pl.BlockSpec((1, 1, tm * TOP_K), lambda i: (jnp.minimum(i + 1, nblk - 1), 0, 0),
                               memory_space=pltpu.SMEM),
                  pl.BlockSpec((tm, LANES), row),
                  pl.BlockSpec((tm, d), row),
                  pl.BlockSpec((1, 1, d), lambda i: (i // per_b, 0, 0)),
                  pl.BlockSpec((1, d), const),
                  pl.BlockSpec((1, d), const),
                  pl.BlockSpec(memory_space=pl.ANY)],
        out_specs=pl.BlockSpec((tm, d), row),
        out_shape=jax.ShapeDtypeStruct((n, d), F32),
        scratch_shapes=[pltpu.VMEM((2, TOP_K, tm * _row_pitch(d), LANES), F32), pltpu.SemaphoreType.DMA((2,))],
        compiler_params=_cparams(("arbitrary",)),
        name="moe_combine",
    )(dest3, dest3, meta, h, gf, ln_g.reshape(1, d), ln_b.reshape(1, d), y)


def moe_layer(h, sc, sh, gf, ln_g, ln_b, router_w, router_b, layer, w1_bf, b1, w2_bf, b2, seq):
    n, d = h.shape
    ne = router_w.shape[1]
    rw_pad = jnp.pad(router_w, ((0, 0), (0, LANES - ne)))
    rb_pad = jnp.pad(router_b, (0, LANES - ne), constant_values=NEG).reshape(1, LANES)
    meta, cnt = moe_router(h, sc, sh, rw_pad, rb_pad, seq)
    idx4 = meta[:, 0:TOP_K].astype(I32)
    rank4 = meta[:, 2 * TOP_K:3 * TOP_K].astype(I32)
    counts = cnt[0, :ne].astype(I32)
    padded = (counts + MOE_BLK - 1) // MOE_BLK * MOE_BLK
    end = jnp.cumsum(padded)
    start = end - padded
    onehot = idx4[:, :, None] == jnp.arange(ne, dtype=I32)[None, None, :]
    dest = jnp.sum(jnp.where(onehot, start[None, None, :], 0), axis=-1) + rank4
    n_rows = -(-(n * TOP_K + ne * (MOE_BLK - 1)) // MOE_BLK) * MOE_BLK
    nblk = n_rows // MOE_BLK
    blk0 = jnp.arange(nblk, dtype=I32) * MOE_BLK
    n_used = (end[-1] // MOE_BLK).astype(I32)
    blk_exp = jnp.minimum(jnp.sum(end[None, :] <= blk0[:, None], axis=-1), ne - 1).astype(I32)
    last_exp = blk_exp[jnp.maximum(n_used - 1, 0)]
    blk_exp = jnp.where(jnp.arange(nblk) < n_used, blk_exp, last_exp)
    n_pad = n_rows - n * TOP_K
    seg_cnt = jnp.concatenate([padded - counts, (n_rows - end[-1])[None]])
    seg_base = jnp.concatenate([start + counts, end[-1:]])
    seg_end = jnp.cumsum(seg_cnt)
    q = jnp.arange(n_pad, dtype=I32)
    seg = jnp.sum(seg_end[None, :] <= q[:, None], axis=-1)
    pad_dest = (seg_base[seg] + q - (seg_end - seg_cnt)[seg]).astype(I32)
    xs = moe_dispatch(h, sc, sh, dest.reshape(-1), pad_dest, n_rows, seq)
    y = moe_experts(xs, blk_exp + layer * ne, n_used.reshape(1), w1_bf, b1, w2_bf, b2)
    return moe_combine(y, dest.reshape(-1), meta, h, gf, ln_g, ln_b, seq)


def _inproj_c_body(x_ref, sc_ref, sh_ref, w_ref, o_ref, acc_ref):
    tm = x_ref.shape[0]
    xb = (x_ref[...] * (1.0 + sc_ref[0]) + sh_ref[0]).astype(BF16)
    acc = jnp.dot(xb, w_ref[...], preferred_element_type=F32)
    ncol = acc.shape[1] // LANES
    for c in range(ncol):
        acc_ref[c] = acc[:, c * LANES:(c + 1) * LANES]
    for t in range(SSM_T):
        for c in range(ncol):
            o_ref[t, :, c * LANES:(c + 1) * LANES] = acc_ref[c, pl.ds(t, tm // SSM_T, stride=SSM_T), :]


def inproj_c(h, sc, sh, w_bf, seq):
    n, d = h.shape
    width = w_bf.shape[1]
    tm = min(512, seq)
    per_b = seq // tm
    return pl.pallas_call(
        _inproj_c_body,
        grid=(n // tm,),
        in_specs=[pl.BlockSpec((tm, d), lambda i: (i, 0)),
                  pl.BlockSpec((1, 1, d), lambda i: (i // per_b, 0, 0)),
                  pl.BlockSpec((1, 1, d), lambda i: (i // per_b, 0, 0)),
                  _resident((d, width), lambda i: (0, 0))],
        out_specs=pl.BlockSpec((SSM_T, tm // SSM_T, width), lambda i: (0, i, 0)),
        out_shape=jax.ShapeDtypeStruct((SSM_T, n // SSM_T, width), F32),
        scratch_shapes=[pltpu.VMEM((width // LANES, tm, LANES), F32)],
        compiler_params=_cparams(("arbitrary",)),
        name="inproj_c",
    )(h, sc, sh, w_bf)


def _s5_body(u_ref, wt_ref, wa_ref, wc_ref, lam_ref, d_ref, o_ref, st_ref, v_ref, xs_ref):
    nb, cb = u_ref.shape[1], u_ref.shape[2]
    ns = st_ref.shape[1] // 2
    kdim = SSM_T * LANES

    @pl.when(pl.program_id(1) == 0)
    def _():
        st_ref[...] = jnp.zeros_like(st_ref)

    xcat = jnp.concatenate(
        [jnp.concatenate([u_ref[t, b].astype(BF16) for t in range(SSM_T)], axis=1) for b in range(nb)], axis=0)
    v = jnp.dot(xcat, wa_ref[0], preferred_element_type=F32)
    nslab = 2 * ns // LANES
    for c in range(nslab):
        v_ref[c] = v[:, c * LANES:(c + 1) * LANES]
    lam = lam_ref[0]
    a_re, a_im = lam[:, :ns], lam[:, ns:]

    def step(r, st):
        for c in range(nslab):
            xs_ref[c, pl.ds(r, nb, stride=cb), :] = st[:, c * LANES:(c + 1) * LANES]
        v = jnp.concatenate([v_ref[c, pl.ds(r, nb, stride=cb), :] for c in range(nslab)], axis=1)
        s_re, s_im = st[:, :ns], st[:, ns:]
        n_re = a_re * s_re - a_im * s_im + v[:, :ns]
        n_im = a_re * s_im + a_im * s_re + v[:, ns:]
        return jnp.concatenate([n_re, n_im], axis=1)

    st_ref[...] = lax.fori_loop(0, cb, step, st_ref[...])
    xs_bf = jnp.concatenate([xs_ref[c].astype(BF16) for c in range(nslab)], axis=1)
    dvec = d_ref[0]
    for q in range(S5_COL_BLOCKS):
        c0, c1 = q * kdim // S5_COL_BLOCKS, (q + 1) * kdim // S5_COL_BLOCKS
        y = jnp.dot(xcat[:, :c1], wt_ref[0, :c1, c0:c1], preferred_element_type=F32)
        y = y + jnp.dot(xs_bf, wc_ref[0, :, c0:c1], preferred_element_type=F32)
        for tt in range((c1 - c0) // LANES):
            t = c0 // LANES + tt
            for b in range(nb):
                yt = y[b * cb:(b + 1) * cb, tt * LANES:(tt + 1) * LANES] + dvec * u_ref[t, b]
                gel = 0.5 * yt * (1.0 + jnp.tanh(0.7978845608028654 * (yt + 0.044715 * yt * yt * yt)))
                o_ref[t, b] = gel.astype(o_ref.dtype)


def s5_mixer(u2, wt, wa, wc, lam_t, dpack, bsz, seq):
    _, nchunk, width = u2.shape
    npack = width // LANES
    per_b = seq // SSM_T
    cb = min(256, per_b)
    nblk = per_b // cb
    ns2 = wa.shape[2]
    kdim = SSM_T * LANES
    pk = lambda j, i: (j, 0, 0)
    u4 = u2.reshape(SSM_T, bsz, per_b, width)
    y4 = pl.pallas_call(
        _s5_body,
        grid=(npack, nblk),
        in_specs=[pl.BlockSpec((SSM_T, bsz, cb, LANES), lambda j, i: (0, 0, i, j)),
                  _resident((1, kdim, kdim), pk),
                  _resident((1, kdim, ns2), pk),
                  _resident((1, ns2, kdim), pk),
                  pl.BlockSpec((1, 1, ns2), pk),
                  pl.BlockSpec((1, 1, LANES), pk)],
        out_specs=pl.BlockSpec((SSM_T, bsz, cb, LANES), lambda j, i: (0, 0, i, j)),
        out_shape=jax.ShapeDtypeStruct((SSM_T, bsz, per_b, width), F32),
        scratch_shapes=[pltpu.VMEM((bsz, ns2), F32), pltpu.VMEM((ns2 // LANES, bsz * cb, LANES), F32),
                        pltpu.VMEM((ns2 // LANES, bsz * cb, LANES), F32)],
        compiler_params=_cparams(("arbitrary", "arbitrary")),
        name="s5_mixer",
    )(u4, wt, wa, wc, lam_t, dpack)
    return y4.reshape(SSM_T, nchunk, width)


def s5_operators(a_re, a_im, log_dt, b_re, b_im, c_re, c_im, dskip):
    g, p = a_re.shape
    cg = b_re.shape[2]
    npack = g // SSM_PACK
    ns2 = SSM_PACK * 2 * p
    t = SSM_T
    lam = lax.complex(a_re, a_im)
    dt = jnp.exp(log_dt)[:, None]
    lam_bar = jnp.exp(lam * dt)
    b_bar = ((lam_bar - 1.0) / lam)[..., None] * lax.complex(b_re, b_im)
    c_mat = lax.complex(c_re, c_im)
    kpow = jnp.arange(t + 1, dtype=F32)[:, None, None]
    pows = jnp.exp((lam * dt)[None] * kpow)
    lane_c = jnp.arange(LANES) % cg
    lane_g = jnp.arange(LANES) // cg
    tile = (lane_c[None, :] == jnp.arange(cg)[:, None]).astype(F32)
    diag = lane_g[:, None] == lane_g[None, :]
    kern = jnp.real(jnp.einsum('gcp,kgp,gpd->kgdc', c_mat, pows[:t], b_bar, precision=HIGHEST))
    kfull = jnp.where(diag, jnp.dot(kern.reshape(t, npack, LANES, cg), tile, precision=HIGHEST), 0.0)
    kft = kfull.transpose(1, 2, 0, 3)
    wt = jnp.stack([jnp.pad(kft[:, :, :t - s].reshape(npack, LANES, (t - s) * LANES),
                            ((0, 0), (0, 0), (s * LANES, 0))) for s in range(t)], axis=1)
    wt = wt.reshape(npack, t * LANES, t * LANES)
    ns = SSM_PACK * p
    col_g = jnp.arange(ns) // p
    tile_p = (jnp.arange(ns)[None, :] % p == jnp.arange(p)[:, None]).astype(F32)
    wa_c = pows[t - 1 - jnp.arange(t)][:, :, :, None] * b_bar[None]
    wa_c = wa_c.transpose(0, 1, 3, 2).reshape(t, npack, LANES, p)
    own_a = lane_g[:, None] == col_g[None, :]
    wa = jnp.concatenate([jnp.where(own_a, jnp.dot(part, tile_p, precision=HIGHEST), 0.0)
                          for part in (jnp.real(wa_c), jnp.imag(wa_c))], axis=-1)
    wa = wa.transpose(1, 0, 2, 3).reshape(npack, t * LANES, ns2)
    wc_c = c_mat[None] * pows[1:t + 1][:, :, None, :]
    wc_c = wc_c.transpose(0, 1, 3, 2).reshape(t, npack, ns, cg)
    cc = jnp.concatenate([jnp.real(wc_c), -jnp.imag(wc_c)], axis=2)
    own_c = (jnp.arange(ns2) % ns // p)[:, None] == lane_g[None, :]
    wcf = jnp.where(own_c, jnp.dot(cc, tile, precision=HIGHEST), 0.0)
    wc = wcf.transpose(1, 2, 0, 3).reshape(npack, ns2, t * LANES)
    lam_t = pows[t].reshape(npack, 1, ns)
    lam_t = jnp.concatenate([jnp.real(lam_t), jnp.imag(lam_t)], axis=-1)
    dpack = dskip.reshape(npack, 1, LANES)
    return wt.astype(BF16), wa.astype(BF16), wc.astype(BF16), lam_t, dpack


def _glu_out_body(y_ref, wg_ref, bg_ref, wo_ref, h_ref, gm_ref, lg_ref, lb_ref, o_ref, ytm_ref):
    cb = y_ref.shape[1]
    ncol = y_ref.shape[2] // LANES
    for t in range(SSM_T):
        for c in range(ncol):
            ytm_ref[c, pl.ds(t, cb, stride=SSM_T), :] = y_ref[t, :, c * LANES:(c + 1) * LANES]
    y = jnp.concatenate([ytm_ref[c] for c in range(ncol)], axis=1)
    y_bf = y.astype(BF16)
    gate = jnp.dot(y_bf, wg_ref[...], preferred_element_type=F32) + bg_ref[...]
    z = (y * jax.nn.sigmoid(gate)).astype(BF16)
    out = jnp.dot(z, wo_ref[...], preferred_element_type=F32)
    zz = DN_ALPHA * h_ref[...] + (1.0 + gm_ref[0]) * out
    o_ref[...] = _layer_norm(zz, lg_ref[...], lb_ref[...])


def glu_out(y2, wg_bf, bg, wo_bf, h, gm, ln_g, ln_b, seq):
    n, d = h.shape
    width = y2.shape[2]
    tm = min(256, seq)
    cb = tm // SSM_T
    per_b = seq // tm
    row = lambda i: (i, 0)
    const = lambda i: (0, 0)
    return pl.pallas_call(
        _glu_out_body,
        grid=(n // tm,),
        in_specs=[pl.BlockSpec((SSM_T, cb, width), lambda i: (0, i, 0)),
                  _resident((width, width), const),
                  pl.BlockSpec((1, width), const),
                  _resident((width, d), const),
                  pl.BlockSpec((tm, d), row),
                  pl.BlockSpec((1, 1, d), lambda i: (i // per_b, 0, 0)),
                  pl.BlockSpec((1, d), const),
                  pl.BlockSpec((1, d), const)],
        out_specs=pl.BlockSpec((tm, d), row),
        out_shape=jax.ShapeDtypeStruct((n, d), F32),
        scratch_shapes=[pltpu.VMEM((width // LANES, tm, LANES), F32)],
        compiler_params=_cparams(("arbitrary",)),
        name="glu_out",
    )(y2, wg_bf, bg.reshape(1, width), wo_bf, h, gm, ln_g.reshape(1, d), ln_b.reshape(1, d))


def kernel(x, c, positions, ada_w, ada_b, ln_g, ln_b, ab_in_w, ab_in_b, ab_sinks, ab_gnorm_w, ab_out_w, ab_out_b, hgrn_lb_logits, c_in_w, c_A_re, c_A_im, c_log_dt, c_B_re, c_B_im, c_C_re, c_C_im, c_D, c_glu_w, c_glu_b, c_out_w, router_w, router_b, exp_w1, exp_b1, exp_w2, exp_b2):
    bsz, seq, d = x.shape
    n = bsz * seq
    depth = ada_w.shape[0]
    mod = ada_mod(c, ada_w, ada_b)
    lb_all = jnp.cumsum(jax.nn.softmax(hgrn_lb_logits.astype(F32), axis=0), axis=0)
    half = A_HEAD_DIM // 2
    inv_freq = ROPE_THETA ** (-jnp.arange(half, dtype=F32) / half)
    invf = jnp.tile(inv_freq, LANES // half).reshape(1, LANES)
    pos_col = positions.reshape(n, 1)
    ne_all = depth * exp_w1.shape[1]
    w1_all = exp_w1.astype(BF16).reshape(ne_all, d, exp_w1.shape[3])
    w2_all = exp_w2.astype(BF16).reshape(ne_all, exp_w2.shape[2], d)
    b1_all = exp_b1.reshape(ne_all, exp_b1.shape[2])
    b2_all = exp_b2.reshape(ne_all, d)
    h = x.reshape(n, d)
    for layer in range(depth):
        m6 = mod[layer].reshape(bsz, 6, 1, d)
        sh_m, sc_m, g_m, sh_f, sc_f, g_f = [m6[:, k] for k in range(6)]
        i = layer // 2
        if layer % 2 == 0:
            proj = inproj_ab(h, sc_m, sh_m, ab_in_w[i].astype(BF16), ab_in_b[i], seq)
            oa = swa_attention(proj, pos_col, invf, ab_sinks[i], bsz, seq)
            ob = hgrn2(proj, lb_all[layer], ab_gnorm_w[i], bsz, seq)
            h = outproj_ab(oa, ob, ab_out_w[i].astype(BF16), ab_out_b[i], h, g_m,
                           ln_g[layer, 0], ln_b[layer, 0], seq)
        else:
            wt, wa, wc, lam_t, dpack = s5_operators(c_A_re[i], c_A_im[i], c_log_dt[i], c_B_re[i], c_B_im[i],
                                                    c_C_re[i], c_C_im[i], c_D[i])
            u2 = inproj_c(h, sc_m, sh_m, c_in_w[i].astype(BF16), seq)
            y2 = s5_mixer(u2, wt, wa, wc, lam_t, dpack, bsz, seq)
            h = glu_out(y2, c_glu_w[i].astype(BF16), c_glu_b[i], c_out_w[i].astype(BF16), h, g_m,
                        ln_g[layer, 0], ln_b[layer, 0], seq)
        h = moe_layer(h, sc_f, sh_f, g_f, ln_g[layer, 1], ln_b[layer, 1], router_w[layer], router_b[layer],
                      layer, w1_all, b1_all, w2_all, b2_all, seq)
    return h.reshape(bsz, seq, d)
```

```python
import functools

import jax
import jax.numpy as jnp
from jax import lax
from jax.experimental import pallas as pl
from jax.experimental.pallas import tpu as pltpu

F32 = jnp.float32
BF16 = jnp.bfloat16
I32 = jnp.int32
HIGHEST = lax.Precision.HIGHEST

DEPTH = 2
A_HEADS = 16
A_KV_HEADS = 2
A_HEAD_DIM = 64
WINDOW = 128
ROPE_THETA = 10000.0
B_HEADS = 8
B_DIM = 128
GLA_CHUNK = 64
C_GROUP = 16
C_STATE = 64
N_EXPERTS = 32
TOP_K = 4
SWIGLU_LIMIT = 7.0
SWIGLU_ALPHA = 1.702
DN_ALPHA = (2 * DEPTH) ** 0.25
LN_EPS = 1e-5
RMS_EPS = 1e-6

LANES = 128
VMEM_LIMIT = 56 * 1024 * 1024

ATT_BLK = 128
HGRN_T = 512
MOE_BLK = 256
SSM_T = 16
SSM_PACK = LANES // C_GROUP
S5_COL_BLOCKS = 4
DMA_GROUP = 8
COMBINE_GROUP = 32
ROW_SUB = 256
W_CHUNK = 256
NEG = -1e30


def _row_pitch(d):
    return d // LANES + 1


def _cparams(sem):
    return pltpu.CompilerParams(dimension_semantics=sem, vmem_limit_bytes=VMEM_LIMIT)


def _resident(block_shape, index_map):
    return pl.BlockSpec(block_shape, index_map, pipeline_mode=pl.Buffered(1))


def _layer_norm(z, g, b):
    mu = jnp.mean(z, axis=-1, keepdims=True)
    zc = z - mu
    var = jnp.mean(zc * zc, axis=-1, keepdims=True)
    return zc * lax.rsqrt(var + LN_EPS) * g + b


def _ada_body(ct_ref, w_ref, b_ref, o_ref):
    ct = ct_ref[...]
    cond = ct * jax.nn.sigmoid(ct)
    w = w_ref[0]
    rows = [jnp.sum(w * cond[:, b:b + 1], axis=0, keepdims=True) for b in range(ct.shape[1])]
    o_ref[0] = jnp.concatenate(rows, axis=0) + b_ref[0]


def ada_mod(c, ada_w, ada_b):
    depth, d, n6 = ada_w.shape
    bsz = c.shape[0]
    tn = 1536
    return pl.pallas_call(
        _ada_body,
        grid=(depth, n6 // tn),
        in_specs=[pl.BlockSpec((d, bsz), lambda l, j: (0, 0)),
                  pl.BlockSpec((1, d, tn), lambda l, j: (l, 0, j)),
                  pl.BlockSpec((1, 1, tn), lambda l, j: (l, 0, j))],
        out_specs=pl.BlockSpec((1, bsz, tn), lambda l, j: (l, 0, j)),
        out_shape=jax.ShapeDtypeStruct((depth, bsz, n6), F32),
        compiler_params=_cparams(("arbitrary", "arbitrary")),
        name="ada_mod",
    )(c.T, ada_w, ada_b.reshape(depth, 1, n6))


def _inproj_ab_body(x_ref, sc_ref, sh_ref, w_ref, b_ref, o_ref, xb_ref):
    @pl.when(pl.program_id(1) == 0)
    def _():
        xb_ref[...] = (x_ref[...] * (1.0 + sc_ref[0]) + sh_ref[0]).astype(BF16)

    o_ref[...] = jnp.dot(xb_ref[...], w_ref[...], preferred_element_type=F32) + b_ref[...]


def inproj_ab(h, sc, sh, w_bf, b, seq):
    n, d = h.shape
    nc = w_bf.shape[1]
    tm = min(1024, seq)
    tn = 768
    per_b = seq // tm
    return pl.pallas_call(
        _inproj_ab_body,
        grid=(n // tm, nc // tn),
        in_specs=[pl.BlockSpec((tm, d), lambda i, j: (i, 0)),
                  pl.BlockSpec((1, 1, d), lambda i, j: (i // per_b, 0, 0)),
                  pl.BlockSpec((1, 1, d), lambda i, j: (i // per_b, 0, 0)),
                  pl.BlockSpec((d, tn), lambda i, j: (0, j)),
                  pl.BlockSpec((1, tn), lambda i, j: (0, j))],
        out_specs=pl.BlockSpec((tm, tn), lambda i, j: (i, j)),
        out_shape=jax.ShapeDtypeStruct((n, nc), F32),
        scratch_shapes=[pltpu.VMEM((tm, d), BF16)],
        compiler_params=_cparams(("arbitrary", "arbitrary")),
        name="inproj_ab",
    )(h, sc, sh, w_bf, b.reshape(1, nc))


def _attn_body(sink_ref, q_ref, kc_ref, vc_ref, vp_ref, pc_ref, invf_ref, o_ref, kprev_ref):
    blk = ATT_BLK
    half = A_HEAD_DIM // 2
    grp = A_HEADS // A_KV_HEADS
    i = pl.program_id(1)
    lane = lax.broadcasted_iota(I32, (blk, LANES), 1)
    first_half = (lane % A_HEAD_DIM) < half
    low_head = lane < A_HEAD_DIM
    invf = invf_ref[...]

    def cos_sin(pos_col):
        ang = pos_col.astype(F32) * invf
        s = jnp.sin(ang)
        return jnp.cos(ang), jnp.where(first_half, -s, s)

    def rope(x, cs):
        partner = jnp.where(first_half, pltpu.roll(x, LANES - half, 1), pltpu.roll(x, half, 1))
        return x * cs[0] + partner * cs[1]

    @pl.when(i == 0)
    def _():
        kprev_ref[...] = jnp.zeros_like(kprev_ref)

    cs_c = cos_sin(pc_ref[...])
    k_cur = rope(kc_ref[...], cs_c)
    k2 = jnp.concatenate([kprev_ref[...], k_cur], axis=0)
    kprev_ref[...] = k_cur
    v2 = jnp.concatenate([vp_ref[...], vc_ref[...]], axis=0)
    lane2 = lax.broadcasted_iota(I32, (2 * blk, LANES), 1)
    k2r = pltpu.roll(k2, A_HEAD_DIM, 1)
    v2r = pltpu.roll(v2, A_HEAD_DIM, 1)

    rows = lax.broadcasted_iota(I32, (grp * blk, 2 * blk), 0) % blk
    cols = lax.broadcasted_iota(I32, (grp * blk, 2 * blk), 1)
    valid = (cols > rows) & (cols <= rows + WINDOW) & ((cols >= blk) | (i > 0))

    q_chunks = [rope(q_ref[:, c * LANES:(c + 1) * LANES], cs_c) * (A_HEAD_DIM ** -0.5)
                for c in range(A_HEADS * A_HEAD_DIM // LANES)]

    for g in range(A_KV_HEADS):
        own = (lane2 < A_HEAD_DIM) if g == 0 else (lane2 >= A_HEAD_DIM)
        kg = jnp.where(own, k2, k2r).astype(BF16)
        vg = jnp.where(own, v2, v2r).astype(BF16)
        pieces = []
        sinks = []
        for r in range(grp):
            hd = g * grp + r
            qc = q_chunks[hd // 2]
            keep = low_head if hd % 2 == 0 else jnp.logical_not(low_head)
            pieces.append(jnp.where(keep, qc, 0.0).astype(BF16))
            sinks.append(jnp.full((blk, 1), sink_ref[hd], F32))
        qs = jnp.concatenate(pieces, axis=0)
        sink = jnp.concatenate(sinks, axis=0)
        s = lax.dot_general(qs, kg, (((1,), (1,)), ((), ())), preferred_element_type=F32)
        s = jnp.where(valid, s, NEG)
        m = jnp.maximum(jnp.max(s, axis=-1, keepdims=True), sink)
        p = jnp.exp(s - m)
        den = jnp.sum(p, axis=-1, keepdims=True) + jnp.exp(sink - m)
        o = jnp.dot(p.astype(BF16), vg, preferred_element_type=F32) / den
        for r2 in range(grp // 2):
            c = (g * grp) // 2 + r2
            ev = o[(2 * r2) * blk:(2 * r2 + 1) * blk]
            od = o[(2 * r2 + 1) * blk:(2 * r2 + 2) * blk]
            o_ref[:, c * LANES:(c + 1) * LANES] = jnp.where(low_head, ev, od).astype(o_ref.dtype)


def swa_attention(proj, pos_col, invf, sinks, bsz, seq):
    n = proj.shape[0]
    blk = ATT_BLK
    nb = seq // blk
    aq = A_HEADS * A_HEAD_DIM
    kcol = aq // LANES
    vcol = kcol + 1

    def cur(b, i, s):
        return b * nb + i

    def prev(b, i, s):
        return b * nb + jnp.maximum(i - 1, 0)

    grid_spec = pltpu.PrefetchScalarGridSpec(
        num_scalar_prefetch=1,
        grid=(bsz, nb),
        in_specs=[pl.BlockSpec((blk, aq), lambda b, i, s: (cur(b, i, s), 0)),
                  pl.BlockSpec((blk, LANES), lambda b, i, s: (cur(b, i, s), kcol)),
                  pl.BlockSpec((blk, LANES), lambda b, i, s: (cur(b, i, s), vcol)),
                  pl.BlockSpec((blk, LANES), lambda b, i, s: (prev(b, i, s), vcol)),
                  pl.BlockSpec((blk, 1), lambda b, i, s: (cur(b, i, s), 0)),
                  pl.BlockSpec((1, LANES), lambda b, i, s: (0, 0))],
        out_specs=pl.BlockSpec((blk, aq), lambda b, i, s: (cur(b, i, s), 0)),
        scratch_shapes=[pltpu.VMEM((blk, LANES), F32)],
    )
    return pl.pallas_call(
        _attn_body,
        grid_spec=grid_spec,
        out_shape=jax.ShapeDtypeStruct((n, aq), BF16),
        compiler_params=_cparams(("arbitrary", "arbitrary")),
        name="swa_attention",
    )(sinks, proj, proj, proj, proj, pos_col, invf)


def _hgrn_body(q_ref, f_ref, i_ref, g_ref, lb_ref, gw_ref, o_ref, st_ref):
    L = GLA_CHUNK

    @pl.when(pl.program_id(2) == 0)
    def _():
        st_ref[...] = jnp.zeros_like(st_ref)

    T = q_ref.shape[0]
    nch = T // L
    lb = lb_ref[0]
    gw = gw_ref[0]
    nt = (((1,), (1,)), ((), ()))

    f = lb + (1.0 - lb) * jax.nn.sigmoid(f_ref[...])
    kk = 1.0 - f
    qb = q_ref[...]
    q = qb * jax.nn.sigmoid(qb)
    v = i_ref[...]
    pos = lax.broadcasted_iota(I32, (T, B_DIM), 0) % L
    b = jnp.log(f)
    sh = 1
    while sh < L:
        b = b + jnp.where(pos >= sh, pltpu.roll(b, sh, 0), 0.0)
        sh *= 2

    def per_chunk(row):
        return jnp.concatenate(
            [jnp.broadcast_to(b[c * L + row:c * L + row + 1], (L, B_DIM)) for c in range(nch)], axis=0)

    b_mid = per_chunk(L // 2)
    b_last = per_chunk(L - 1)
    e_mid = jnp.exp(b - b_mid)
    qa = (q * e_mid).astype(BF16)
    ka = (kk * jnp.exp(b_mid - b)).astype(BF16)
    qe = (q * jnp.exp(b)).astype(BF16)
    kd = (kk * jnp.exp(b_last - b)).astype(BF16)
    att = lax.dot_general(qa, ka, nt, preferred_element_type=F32)
    r = lax.broadcasted_iota(I32, (T, T), 0)
    c = lax.broadcasted_iota(I32, (T, T), 1)
    att = jnp.where((r >= c) & (r // L == c // L), att, 0.0)
    o_intra = jnp.dot(att.astype(BF16), v.astype(BF16), preferred_element_type=F32)
    vt = v.T.astype(BF16)
    st = st_ref[...]
    o_inter = []
    for ci in range(nch):
        sl = slice(ci * L, (ci + 1) * L)
        o_inter.append(lax.dot_general(qe[sl], st.astype(BF16), nt, preferred_element_type=F32))
        u_t = jnp.dot(vt[:, sl], kd[sl], preferred_element_type=F32)
        st = jnp.exp(b[ci * L + L - 1:ci * L + L]) * st + u_t
    st_ref[...] = st
    o = o_intra + jnp.concatenate(o_inter, axis=0)
    o = o * lax.rsqrt(jnp.mean(o * o, axis=-1, keepdims=True) + RMS_EPS) * gw
    gb = g_ref[...]
    o_ref[...] = (o * (gb * jax.nn.sigmoid(gb))).astype(o_ref.dtype)


def hgrn2(proj, lb, gnorm_w, bsz, seq):
    n = proj.shape[0]
    t = min(HGRN_T, seq)
    nt = seq // t
    base = (A_HEADS + 2 * A_KV_HEADS) * A_HEAD_DIM // LANES
    h = B_HEADS

    def col(k):
        return lambda b, hd, i: (b * nt + i, base + k * h + hd)

    return pl.pallas_call(
        _hgrn_body,
        grid=(bsz, h, nt),
        in_specs=[pl.BlockSpec((t, LANES), col(0)),
                  pl.BlockSpec((t, LANES), col(1)),
                  pl.BlockSpec((t, LANES), col(2)),
                  pl.BlockSpec((t, LANES), col(3)),
                  pl.BlockSpec((1, 1, B_DIM), lambda b, hd, i: (hd, 0, 0)),
                  pl.BlockSpec((1, 1, B_DIM), lambda b, hd, i: (hd, 0, 0))],
        out_specs=pl.BlockSpec((t, LANES), lambda b, hd, i: (b * nt + i, hd)),
        out_shape=jax.ShapeDtypeStruct((n, h * B_DIM), BF16),
        scratch_shapes=[pltpu.VMEM((B_DIM, B_DIM), F32)],
        compiler_params=_cparams(("arbitrary", "arbitrary", "arbitrary")),
        name="hgrn2",
    )(proj, proj, proj, proj, lb.reshape(h, 1, B_DIM), gnorm_w.reshape(h, 1, B_DIM))


def _outproj_ab_body(oa_ref, ob_ref, w_ref, b_ref, h_ref, gm_ref, lg_ref, lb_ref, o_ref):
    ka = oa_ref.shape[1]
    tm = oa_ref.shape[0]
    sub = min(tm, ROW_SUB)
    for r0 in range(0, tm, sub):
        rs = slice(r0, r0 + sub)
        y = jnp.dot(oa_ref[rs, :], w_ref[:ka, :], preferred_element_type=F32)
        y = y + jnp.dot(ob_ref[rs, :], w_ref[ka:, :], preferred_element_type=F32) + b_ref[...]
        z = DN_ALPHA * h_ref[rs, :] + (1.0 + gm_ref[0]) * y
        o_ref[rs, :] = _layer_norm(z, lg_ref[...], lb_ref[...])


def outproj_ab(oa, ob, w_bf, b, h, gm, ln_g, ln_b, seq):
    n, d = h.shape
    tm = min(512, seq)
    per_b = seq // tm
    ka, kb = oa.shape[1], ob.shape[1]
    row = lambda i: (i, 0)
    const = lambda i: (0, 0)
    return pl.pallas_call(
        _outproj_ab_body,
        grid=(n // tm,),
        in_specs=[pl.BlockSpec((tm, ka), row),
                  pl.BlockSpec((tm, kb), row),
                  _resident((ka + kb, d), const),
                  pl.BlockSpec((1, d), const),
                  pl.BlockSpec((tm, d), row),
                  pl.BlockSpec((1, 1, d), lambda i: (i // per_b, 0, 0)),
                  pl.BlockSpec((1, d), const),
                  pl.BlockSpec((1, d), const)],
        out_specs=pl.BlockSpec((tm, d), row),
        out_shape=jax.ShapeDtypeStruct((n, d), F32),
        compiler_params=_cparams(("arbitrary",)),
        name="outproj_ab",
    )(oa, ob, w_bf, b.reshape(1, d), h, gm, ln_g.reshape(1, d), ln_b.reshape(1, d))


def _router_body(h_ref, sc_ref, sh_ref, rw_ref, rb_ref, meta_ref, cnt_ref, carry_ref, rwh_ref, rwl_ref):
    tm = h_ref.shape[0]

    @pl.when(pl.program_id(0) == 0)
    def _():
        carry_ref[...] = jnp.zeros_like(carry_ref)
        w = rw_ref[...]
        w_hi = w.astype(BF16)
        rwh_ref[...] = w_hi
        rwl_ref[...] = (w - w_hi.astype(F32)).astype(BF16)

    xin = h_ref[...] * (1.0 + sc_ref[0]) + sh_ref[0]
    x_hi = xin.astype(BF16)
    x_lo = (xin - x_hi.astype(F32)).astype(BF16)
    logits = (jnp.dot(x_hi, rwh_ref[...], preferred_element_type=F32)
              + (jnp.dot(x_hi, rwl_ref[...], preferred_element_type=F32)
                 + jnp.dot(x_lo, rwh_ref[...], preferred_element_type=F32))) + rb_ref[...]
    lane = lax.broadcasted_iota(I32, (tm, LANES), 1)
    l = logits
    vals, idxs = [], []
    sel = jnp.zeros((tm, LANES), F32)
    for _ in range(TOP_K):
        m = jnp.max(l, axis=-1, keepdims=True)
        idx = jnp.min(jnp.where(l == m, lane, LANES), axis=-1, keepdims=True)
        hit = lane == idx
        vals.append(m)
        idxs.append(idx)
        sel = jnp.where(hit, 1.0, sel)
        l = jnp.where(hit, -jnp.inf, l)
    es = [jnp.exp(v - vals[0]) for v in vals]
    den = es[0] + es[1] + es[2] + es[3]
    r = lax.broadcasted_iota(I32, (tm, tm), 0)
    c = lax.broadcasted_iota(I32, (tm, tm), 1)
    tril = (r >= c).astype(BF16)
    incl = jnp.dot(tril, sel.astype(BF16), preferred_element_type=F32)
    carry = carry_ref[...]
    excl = incl - sel + carry
    meta = jnp.zeros((tm, LANES), F32)
    for k in range(TOP_K):
        rank = jnp.sum(jnp.where(lane == idxs[k], excl, 0.0), axis=-1, keepdims=True)
        meta = jnp.where(lane == k, idxs[k].astype(F32), meta)
        meta = jnp.where(lane == TOP_K + k, es[k] / den, meta)
        meta = jnp.where(lane == 2 * TOP_K + k, rank, meta)
    meta_ref[...] = meta
    carry = carry + incl[tm - 1:tm, :]
    carry_ref[...] = carry
    cnt_ref[...] = jnp.broadcast_to(carry, cnt_ref.shape)


def moe_router(h, sc, sh, rw_pad, rb_pad, seq):
    n, d = h.shape
    tm = min(512, seq)
    per_b = seq // tm
    return pl.pallas_call(
        _router_body,
        grid=(n // tm,),
        in_specs=[pl.BlockSpec((tm, d), lambda i: (i, 0)),
                  pl.BlockSpec((1, 1, d), lambda i: (i // per_b, 0, 0)),
                  pl.BlockSpec((1, 1, d), lambda i: (i // per_b, 0, 0)),
                  pl.BlockSpec((d, LANES), lambda i: (0, 0)),
                  pl.BlockSpec((1, LANES), lambda i: (0, 0))],
        out_specs=[pl.BlockSpec((tm, LANES), lambda i: (i, 0)),
                   pl.BlockSpec((8, LANES), lambda i: (0, 0))],
        out_shape=[jax.ShapeDtypeStruct((n, LANES), F32),
                   jax.ShapeDtypeStruct((8, LANES), F32)],
        scratch_shapes=[pltpu.VMEM((1, LANES), F32), pltpu.VMEM((d, LANES), BF16), pltpu.VMEM((d, LANES), BF16)],
        compiler_params=_cparams(("arbitrary",)),
        name="moe_router",
    )(h, sc, sh, rw_pad, rb_pad)


def _dispatch_body(pad_ref, dest_ref, h_ref, sc_ref, sh_ref, xs_hbm, buf_ref, zero_ref, sem, zsem):
    tm, d = h_ref.shape
    rt = d // LANES
    pitch = _row_pitch(d)
    i = pl.program_id(0)
    last = pl.num_programs(0) - 1
    slot = i % 2
    n_pad = pad_ref.shape[0]

    def wait_slot(s):
        for _ in range(TOP_K):
            pltpu.make_async_copy(buf_ref.at[s], xs_hbm.at[pl.ds(0, tm * pitch)], sem.at[s]).wait()

    @pl.when(i == 0)
    def _():
        zero_ref[...] = jnp.zeros_like(zero_ref)
        buf_ref[...] = jnp.zeros_like(buf_ref)

        def zfill(q2, carry):
            for pr in range(2):
                dst = pad_ref[q2 * 2 + pr] * pitch
                pltpu.make_async_copy(zero_ref, xs_hbm.at[pl.ds(dst, pitch)], zsem.at[0]).start(priority=pr)
            return carry

        lax.fori_loop(0, n_pad // 2, zfill, 0)

    @pl.when(i >= 2)
    def _():
        wait_slot(slot)

    xin = h_ref[...] * (1.0 + sc_ref[0]) + sh_ref[0]
    for s in range(rt):
        buf_ref[slot, pl.ds(s, tm, stride=pitch), :] = xin[:, s * LANES:(s + 1) * LANES]

    def issue(g, carry):
        for rr in range(DMA_GROUP):
            r = g * DMA_GROUP + rr
            src = buf_ref.at[slot, pl.ds(r * pitch, pitch)]
            for k in range(TOP_K):
                dst = dest_ref[0, 0, r * TOP_K + k] * pitch
                pltpu.make_async_copy(src, xs_hbm.at[pl.ds(dst, pitch)], sem.at[slot]).start(priority=k % 2)
        return carry

    lax.fori_loop(0, tm // DMA_GROUP, issue, 0)

    @pl.when(i == last)
    def _():
        wait_slot(slot)

        @pl.when(i >= 1)
        def _():
            wait_slot(1 - slot)

        for _ in range(n_pad // tm):
            pltpu.make_async_copy(buf_ref.at[0], xs_hbm.at[pl.ds(0, tm * pitch)], zsem.at[0]).wait()


def moe_dispatch(h, sc, sh, dest, pad_dest, n_rows, seq):
    n, d = h.shape
    tm = min(256, seq)
    per_b = seq // tm
    pitch = _row_pitch(d)
    assert pad_dest.shape[0] % tm == 0
    grid_spec = pltpu.PrefetchScalarGridSpec(
        num_scalar_prefetch=1,
        grid=(n // tm,),
        in_specs=[pl.BlockSpec((1, 1, tm * TOP_K), lambda i, p: (i, 0, 0), memory_space=pltpu.SMEM),
                  pl.BlockSpec((tm, d), lambda i, p: (i, 0)),
                  pl.BlockSpec((1, 1, d), lambda i, p: (i // per_b, 0, 0)),
                  pl.BlockSpec((1, 1, d), lambda i, p: (i // per_b, 0, 0))],
        out_specs=pl.BlockSpec(memory_space=pl.ANY),
        scratch_shapes=[pltpu.VMEM((2, tm * pitch, LANES), F32), pltpu.VMEM((pitch, LANES), F32),
                        pltpu.SemaphoreType.DMA((2,)), pltpu.SemaphoreType.DMA((1,))],
    )
    return pl.pallas_call(
        _dispatch_body,
        grid_spec=grid_spec,
        out_shape=jax.ShapeDtypeStruct((n_rows * pitch, LANES), F32),
        compiler_params=_cparams(("arbitrary",)),
        name="moe_dispatch",
    )(pad_dest, dest.reshape(n // tm, 1, tm * TOP_K), h, sc, sh)


SCHED_MODE, SCHED_BLOCK, SCHED_EXPERT, SCHED_SLOT, SCHED_DL_EXPERT, SCHED_DL_C1, SCHED_DL_C2, SCHED_DL_SLOT = range(8)
MODE_IDLE, MODE_COMPUTE, MODE_ZERO = 0, 1, 2


def _expert_body(sched_ref, xs_ref, w1c_ref, w2c_ref, b1_ref, b2_ref, y_ref, wb1a_ref, wb1b_ref, wb2a_ref, wb2b_ref):
    j = pl.program_id(0)
    nstep = pl.num_programs(0)
    d = wb1a_ref.shape[0]
    ff = wb2a_ref.shape[0]
    rt = d // LANES
    pitch = _row_pitch(d)
    blk = xs_ref.shape[0] // pitch
    cache = ((wb1a_ref, wb2a_ref), (wb1b_ref, wb2b_ref))

    def sched(field):
        return sched_ref[field * nstep + j]

    def cast_chunks(wb1_ref, wb2_ref):
        r1 = pl.multiple_of(sched(SCHED_DL_C1) * W_CHUNK, W_CHUNK)
        r2 = pl.multiple_of(sched(SCHED_DL_C2) * W_CHUNK, W_CHUNK)
        wb1_ref[pl.ds(r1, W_CHUNK), :] = w1c_ref[0].astype(BF16)
        wb2_ref[pl.ds(r2, W_CHUNK), :] = w2c_ref[0].astype(BF16)

    def compute(wb1_ref, wb2_ref):
        x = jnp.concatenate([xs_ref[pl.ds(s, blk, stride=pitch), :].astype(BF16) for s in range(rt)], axis=1)
        hid = jnp.dot(x, wb1_ref[...], preferred_element_type=F32) + b1_ref[0]
        gate = jnp.minimum(hid[:, :ff], SWIGLU_LIMIT)
        lin = jnp.clip(hid[:, ff:], -SWIGLU_LIMIT, SWIGLU_LIMIT)
        act = gate * jax.nn.sigmoid(SWIGLU_ALPHA * gate) * (lin + 1.0)
        y = jnp.dot(act.astype(BF16), wb2_ref[...], preferred_element_type=F32) + b2_ref[0]
        for s in range(rt):
            y_ref[pl.ds(s, blk, stride=pitch), :] = y[:, s * LANES:(s + 1) * LANES]
        for s in range(rt, pitch):
            y_ref[pl.ds(s, blk, stride=pitch), :] = jnp.zeros((blk, LANES), F32)

    mode = sched(SCHED_MODE)
    slot = sched(SCHED_SLOT)
    dl_slot = sched(SCHED_DL_SLOT)
    for par in range(2):
        @pl.when((mode == MODE_COMPUTE) & (slot == par))
        def _(par=par):
            cast_chunks(*cache[1 - par])
            compute(*cache[par])

        @pl.when((mode != MODE_COMPUTE) & (dl_slot == par))
        def _(par=par):
            cast_chunks(*cache[par])

    @pl.when(mode == MODE_ZERO)
    def _():
        y_ref[...] = jnp.zeros_like(y_ref)


def expert_schedule(padded, n_used, nblk, layer, ne, need, need2):
    nb = padded // MOE_BLK
    seg_len = jnp.concatenate([jnp.full((1,), need, I32), jnp.maximum(nb[:-1], need), nb[-1:]])
    seg_end = jnp.cumsum(seg_len)
    seg_start = seg_end - seg_len
    nstep = nblk + (ne + 1) * need
    j = jnp.arange(nstep, dtype=I32)
    seg = jnp.sum(seg_end[None, :] <= j[:, None], axis=-1).astype(I32)
    in_seg = seg <= ne
    segc = jnp.minimum(seg, ne)
    loc = j - seg_start[segc]
    expert = jnp.clip(segc - 1, 0, ne - 1)
    compute = in_seg & (segc >= 1) & (loc < nb[expert])
    done = jnp.cumsum(compute.astype(I32)) - compute.astype(I32)
    tail = j - seg_end[-1]
    ntail = nblk - n_used
    zero = jnp.logical_not(in_seg) & (tail < ntail)
    parked = jnp.where(jnp.logical_not(in_seg) & (ntail > 0), nblk - 1, jnp.maximum(done - 1, 0))
    block = jnp.where(compute, done, jnp.where(zero, n_used + tail, parked))
    mode = jnp.where(compute, MODE_COMPUTE, jnp.where(zero, MODE_ZERO, MODE_IDLE))
    deliver = in_seg & (segc <= ne - 1) & (loc < need)
    last = lax.cummax(jnp.where(deliver, j, -1), axis=0)
    dl_expert = segc[last]
    dl_c1 = loc[last]
    dl_c2 = jnp.minimum(loc[last], need2 - 1)
    table = jnp.stack([mode, block, expert + layer * ne, expert % 2,
                       dl_expert + layer * ne, dl_c1, dl_c2, dl_expert % 2])
    return table.astype(I32).reshape(-1), nstep


def moe_experts(xs, sched, nstep, w1, b1, w2, b2):
    ne, d, ff2 = w1.shape
    pitch = _row_pitch(d)
    n_rows = xs.shape[0] // pitch
    ff = ff2 // 2

    def fld(s, j, field):
        return s[field * nstep + j]

    grid_spec = pltpu.PrefetchScalarGridSpec(
        num_scalar_prefetch=1,
        grid=(nstep,),
        in_specs=[pl.BlockSpec((MOE_BLK * pitch, LANES), lambda j, s: (fld(s, j, SCHED_BLOCK), 0)),
                  pl.BlockSpec((1, W_CHUNK, ff2),
                               lambda j, s: (fld(s, j, SCHED_DL_EXPERT), fld(s, j, SCHED_DL_C1), 0)),
                  pl.BlockSpec((1, W_CHUNK, d),
                               lambda j, s: (fld(s, j, SCHED_DL_EXPERT), fld(s, j, SCHED_DL_C2), 0)),
                  pl.BlockSpec((1, 1, ff2), lambda j, s: (fld(s, j, SCHED_EXPERT), 0, 0)),
                  pl.BlockSpec((1, 1, d), lambda j, s: (fld(s, j, SCHED_EXPERT), 0, 0))],
        out_specs=pl.BlockSpec((MOE_BLK * pitch, LANES), lambda j, s: (fld(s, j, SCHED_BLOCK), 0)),
        scratch_shapes=[pltpu.VMEM((d, ff2), BF16), pltpu.VMEM((d, ff2), BF16),
                        pltpu.VMEM((ff, d), BF16), pltpu.VMEM((ff, d), BF16)],
    )
    return pl.pallas_call(
        _expert_body,
        grid_spec=grid_spec,
        out_shape=jax.ShapeDtypeStruct((n_rows * pitch, LANES), F32),
        compiler_params=_cparams(("arbitrary",)),
        name="moe_experts",
    )(sched, xs, w1, w2, b1.reshape(ne, 1, ff2), b2.reshape(ne, 1, d))


def _combine_body(dc_ref, dn_ref, meta_ref, h_ref, gf_ref, lg_ref, lb_ref, y_hbm, o_ref, buf_ref, sem):
    tm, d = h_ref.shape
    rt = d // LANES
    pitch = _row_pitch(d)
    grp = COMBINE_GROUP
    i = pl.program_id(0)
    n = pl.num_programs(0)
    slot = i % 2

    def issue_rows(dref, s, r0):
        for rr in range(grp):
            r = r0 + rr
            for k in range(TOP_K):
                src = dref[0, 0, r * TOP_K + k] * pitch
                pltpu.make_async_copy(y_hbm.at[pl.ds(src, rt)], buf_ref.at[s, k, pl.ds(r * pitch, rt)],
                                      sem.at[s]).start(priority=k % 2)

    @pl.when(i == 0)
    def _():
        def body(g, carry):
            issue_rows(dc_ref, 0, g * grp)
            return carry
        lax.fori_loop(0, tm // grp, body, 0)

    for k in range(TOP_K):
        pltpu.make_async_copy(y_hbm.at[pl.ds(0, tm * rt)], buf_ref.at[slot, k, pl.ds(0, tm * rt)],
                              sem.at[slot]).wait()

    def group(g, carry, prefetch):
        r0 = pl.multiple_of(g * grp, grp)
        if prefetch:
            issue_rows(dn_ref, 1 - slot, r0)
        meta = meta_ref[pl.ds(r0, grp), :]
        gates = [jnp.broadcast_to(meta[:, TOP_K + k:TOP_K + k + 1], (grp, LANES)) for k in range(TOP_K)]
        for s in range(rt):
            a = gates[0] * buf_ref[slot, 0, pl.ds(r0 * pitch + s, grp, stride=pitch), :]
            for k in range(1, TOP_K):
                a = a + gates[k] * buf_ref[slot, k, pl.ds(r0 * pitch + s, grp, stride=pitch), :]
            cs = slice(s * LANES, (s + 1) * LANES)
            o_ref[pl.ds(r0, grp), cs] = DN_ALPHA * h_ref[pl.ds(r0, grp), cs] + (1.0 + gf_ref[0, :, cs]) * a
        return carry

    @pl.when(i + 1 < n)
    def _():
        lax.fori_loop(0, tm // grp, functools.partial(group, prefetch=True), 0)

    @pl.when(i + 1 == n)
    def _():
        lax.fori_loop(0, tm // grp, functools.partial(group, prefetch=False), 0)

    o_ref[...] = _layer_norm(o_ref[...], lg_ref[...], lb_ref[...])


def moe_combine(y, dest, meta, h, gf, ln_g, ln_b, seq):
    n, d = h.shape
    tm = min(256, seq)
    per_b = seq // tm
    nblk = n // tm
    dest3 = dest.reshape(nblk, 1, tm * TOP_K)
    row = lambda i: (i, 0)
    const = lambda i: (0, 0)
    return pl.pallas_call(
        _combine_body,
        grid=(nblk,),
        in_specs=[pl.BlockSpec((1, 1, tm * TOP_K), lambda i: (i, 0, 0), memory_space=pltpu.SMEM),
                  pl.BlockSpec((1, 1, tm * TOP_K), lambda i: (jnp.minimum(i + 1, nblk - 1), 0, 0),
                               memory_space=pltpu.SMEM),
                  pl.BlockSpec((tm, LANES), row),
                  pl.BlockSpec((tm, d), row),
                  pl.BlockSpec((1, 1, d), lambda i: (i // per_b, 0, 0)),
                  pl.BlockSpec((1, d), const),
                  pl.BlockSpec((1, d), const),
                  pl.BlockSpec(memory_space=pl.ANY)],
        out_specs=pl.BlockSpec((tm, d), row),
        out_shape=jax.ShapeDtypeStruct((n, d), F32),
        scratch_shapes=[pltpu.VMEM((2, TOP_K, tm * _row_pitch(d), LANES), F32), pltpu.SemaphoreType.DMA((2,))],
        compiler_params=_cparams(("arbitrary",)),
        name="moe_combine",
    )(dest3, dest3, meta, h, gf, ln_g.reshape(1, d), ln_b.reshape(1, d), y)


def moe_layer(h, sc, sh, gf, ln_g, ln_b, router_w, router_b, layer, w1, b1, w2, b2, seq):
    n, d = h.shape
    ne = router_w.shape[1]
    rw_pad = jnp.pad(router_w, ((0, 0), (0, LANES - ne)))
    rb_pad = jnp.pad(router_b, (0, LANES - ne), constant_values=NEG).reshape(1, LANES)
    meta, cnt = moe_router(h, sc, sh, rw_pad, rb_pad, seq)
    idx4 = meta[:, 0:TOP_K].astype(I32)
    rank4 = meta[:, 2 * TOP_K:3 * TOP_K].astype(I32)
    counts = cnt[0, :ne].astype(I32)
    padded = (counts + MOE_BLK - 1) // MOE_BLK * MOE_BLK
    end = jnp.cumsum(padded)
    start = end - padded
    onehot = idx4[:, :, None] == jnp.arange(ne, dtype=I32)[None, None, :]
    dest = jnp.sum(jnp.where(onehot, start[None, None, :], 0), axis=-1) + rank4
    n_rows = -(-(n * TOP_K + ne * (MOE_BLK - 1)) // MOE_BLK) * MOE_BLK
    nblk = n_rows // MOE_BLK
    blk0 = jnp.arange(nblk, dtype=I32) * MOE_BLK
    n_used = (end[-1] // MOE_BLK).astype(I32)
    n_pad = n_rows - n * TOP_K
    seg_cnt = jnp.concatenate([padded - counts, (n_rows - end[-1])[None]])
    seg_base = jnp.concatenate([start + counts, end[-1:]])
    seg_end = jnp.cumsum(seg_cnt)
    q = jnp.arange(n_pad, dtype=I32)
    seg = jnp.sum(seg_end[None, :] <= q[:, None], axis=-1)
    pad_dest = (seg_base[seg] + q - (seg_end - seg_cnt)[seg]).astype(I32)
    xs = moe_dispatch(h, sc, sh, dest.reshape(-1), pad_dest, n_rows, seq)
    sched, nstep = expert_schedule(padded, n_used, nblk, layer, ne, w1.shape[1] // W_CHUNK, w2.shape[1] // W_CHUNK)
    y = moe_experts(xs, sched, nstep, w1, b1, w2, b2)
    return moe_combine(y, dest.reshape(-1), meta, h, gf, ln_g, ln_b, seq)


def _inproj_c_body(x_ref, sc_ref, sh_ref, w_ref, o_ref, acc_ref):
    tm = x_ref.shape[0]
    xb = (x_ref[...] * (1.0 + sc_ref[0]) + sh_ref[0]).astype(BF16)
    acc = jnp.dot(xb, w_ref[...], preferred_element_type=F32)
    ncol = acc.shape[1] // LANES
    for c in range(ncol):
        acc_ref[c] = acc[:, c * LANES:(c + 1) * LANES]
    for t in range(SSM_T):
        for c in range(ncol):
            o_ref[t, :, c * LANES:(c + 1) * LANES] = acc_ref[c, pl.ds(t, tm // SSM_T, stride=SSM_T), :]


def inproj_c(h, sc, sh, w_bf, seq):
    n, d = h.shape
    width = w_bf.shape[1]
    tm = min(512, seq)
    per_b = seq // tm
    return pl.pallas_call(
        _inproj_c_body,
        grid=(n // tm,),
        in_specs=[pl.BlockSpec((tm, d), lambda i: (i, 0)),
                  pl.BlockSpec((1, 1, d), lambda i: (i // per_b, 0, 0)),
                  pl.BlockSpec((1, 1, d), lambda i: (i // per_b, 0, 0)),
                  _resident((d, width), lambda i: (0, 0))],
        out_specs=pl.BlockSpec((SSM_T, tm // SSM_T, width), lambda i: (0, i, 0)),
        out_shape=jax.ShapeDtypeStruct((SSM_T, n // SSM_T, width), F32),
        scratch_shapes=[pltpu.VMEM((width // LANES, tm, LANES), F32)],
        compiler_params=_cparams(("arbitrary",)),
        name="inproj_c",
    )(h, sc, sh, w_bf)


def _s5_body(u_ref, wt_ref, wa_ref, wc_ref, lam_ref, d_ref, o_ref, st_ref, v_ref, xs_ref):
    nb, cb = u_ref.shape[1], u_ref.shape[2]
    ns = st_ref.shape[1] // 2
    kdim = SSM_T * LANES

    @pl.when(pl.program_id(1) == 0)
    def _():
        st_ref[...] = jnp.zeros_like(st_ref)

    xcat = jnp.concatenate(
        [jnp.concatenate([u_ref[t, b].astype(BF16) for t in range(SSM_T)], axis=1) for b in range(nb)], axis=0)
    v = jnp.dot(xcat, wa_ref[0], preferred_element_type=F32)
    nslab = 2 * ns // LANES
    for c in range(nslab):
        v_ref[c] = v[:, c * LANES:(c + 1) * LANES]
    lam = lam_ref[0]
    a_re, a_im = lam[:, :ns], lam[:, ns:]

    def step(r, st):
        for c in range(nslab):
            xs_ref[c, pl.ds(r, nb, stride=cb), :] = st[:, c * LANES:(c + 1) * LANES]
        v = jnp.concatenate([v_ref[c, pl.ds(r, nb, stride=cb), :] for c in range(nslab)], axis=1)
        s_re, s_im = st[:, :ns], st[:, ns:]
        n_re = a_re * s_re - a_im * s_im + v[:, :ns]
        n_im = a_re * s_im + a_im * s_re + v[:, ns:]
        return jnp.concatenate([n_re, n_im], axis=1)

    st_ref[...] = lax.fori_loop(0, cb, step, st_ref[...])
    xs_bf = jnp.concatenate([xs_ref[c].astype(BF16) for c in range(nslab)], axis=1)
    dvec = d_ref[0]
    for q in range(S5_COL_BLOCKS):
        c0, c1 = q * kdim // S5_COL_BLOCKS, (q + 1) * kdim // S5_COL_BLOCKS
        y = jnp.dot(xcat[:, :c1], wt_ref[0, :c1, c0:c1], preferred_element_type=F32)
        y = y + jnp.dot(xs_bf, wc_ref[0, :, c0:c1], preferred_element_type=F32)
        for tt in range((c1 - c0) // LANES):
            t = c0 // LANES + tt
            for b in range(nb):
                yt = y[b * cb:(b + 1) * cb, tt * LANES:(tt + 1) * LANES] + dvec * u_ref[t, b]
                gel = 0.5 * yt * (1.0 + jnp.tanh(0.7978845608028654 * (yt + 0.044715 * yt * yt * yt)))
                o_ref[t, b] = gel.astype(o_ref.dtype)


def s5_mixer(u2, wt, wa, wc, lam_t, dpack, bsz, seq):
    _, nchunk, width = u2.shape
    npack = width // LANES
    per_b = seq // SSM_T
    cb = min(256, per_b)
    nblk = per_b // cb
    ns2 = wa.shape[2]
    kdim = SSM_T * LANES
    pk = lambda j, i: (j, 0, 0)
    u4 = u2.reshape(SSM_T, bsz, per_b, width)
    y4 = pl.pallas_call(
        _s5_body,
        grid=(npack, nblk),
        in_specs=[pl.BlockSpec((SSM_T, bsz, cb, LANES), lambda j, i: (0, 0, i, j)),
                  _resident((1, kdim, kdim), pk),
                  _resident((1, kdim, ns2), pk),
                  _resident((1, ns2, kdim), pk),
                  pl.BlockSpec((1, 1, ns2), pk),
                  pl.BlockSpec((1, 1, LANES), pk)],
        out_specs=pl.BlockSpec((SSM_T, bsz, cb, LANES), lambda j, i: (0, 0, i, j)),
        out_shape=jax.ShapeDtypeStruct((SSM_T, bsz, per_b, width), F32),
        scratch_shapes=[pltpu.VMEM((bsz, ns2), F32), pltpu.VMEM((ns2 // LANES, bsz * cb, LANES), F32),
                        pltpu.VMEM((ns2 // LANES, bsz * cb, LANES), F32)],
        compiler_params=_cparams(("arbitrary", "arbitrary")),
        name="s5_mixer",
    )(u4, wt, wa, wc, lam_t, dpack)
    return y4.reshape(SSM_T, nchunk, width)


def s5_operators(a_re, a_im, log_dt, b_re, b_im, c_re, c_im, dskip):
    g, p = a_re.shape
    cg = b_re.shape[2]
    npack = g // SSM_PACK
    ns2 = SSM_PACK * 2 * p
    t = SSM_T
    lam = lax.complex(a_re, a_im)
    dt = jnp.exp(log_dt)[:, None]
    lam_bar = jnp.exp(lam * dt)
    b_bar = ((lam_bar - 1.0) / lam)[..., None] * lax.complex(b_re, b_im)
    c_mat = lax.complex(c_re, c_im)
    kpow = jnp.arange(t + 1, dtype=F32)[:, None, None]
    pows = jnp.exp((lam * dt)[None] * kpow)
    lane_c = jnp.arange(LANES) % cg
    lane_g = jnp.arange(LANES) // cg
    tile = (lane_c[None, :] == jnp.arange(cg)[:, None]).astype(F32)
    diag = lane_g[:, None] == lane_g[None, :]
    kern = jnp.real(jnp.einsum('gcp,kgp,gpd->kgdc', c_mat, pows[:t], b_bar, precision=HIGHEST))
    kfull = jnp.where(diag, jnp.dot(kern.reshape(t, npack, LANES, cg), tile, precision=HIGHEST), 0.0)
    kft = kfull.transpose(1, 2, 0, 3)
    wt = jnp.stack([jnp.pad(kft[:, :, :t - s].reshape(npack, LANES, (t - s) * LANES),
                            ((0, 0), (0, 0), (s * LANES, 0))) for s in range(t)], axis=1)
    wt = wt.reshape(npack, t * LANES, t * LANES)
    ns = SSM_PACK * p
    col_g = jnp.arange(ns) // p
    tile_p = (jnp.arange(ns)[None, :] % p == jnp.arange(p)[:, None]).astype(F32)
    wa_c = pows[t - 1 - jnp.arange(t)][:, :, :, None] * b_bar[None]
    wa_c = wa_c.transpose(0, 1, 3, 2).reshape(t, npack, LANES, p)
    own_a = lane_g[:, None] == col_g[None, :]
    wa = jnp.concatenate([jnp.where(own_a, jnp.dot(part, tile_p, precision=HIGHEST), 0.0)
                          for part in (jnp.real(wa_c), jnp.imag(wa_c))], axis=-1)
    wa = wa.transpose(1, 0, 2, 3).reshape(npack, t * LANES, ns2)
    wc_c = c_mat[None] * pows[1:t + 1][:, :, None, :]
    wc_c = wc_c.transpose(0, 1, 3, 2).reshape(t, npack, ns, cg)
    cc = jnp.concatenate([jnp.real(wc_c), -jnp.imag(wc_c)], axis=2)
    own_c = (jnp.arange(ns2) % ns // p)[:, None] == lane_g[None, :]
    wcf = jnp.where(own_c, jnp.dot(cc, tile, precision=HIGHEST), 0.0)
    wc = wcf.transpose(1, 2, 0, 3).reshape(npack, ns2, t * LANES)
    lam_t = pows[t].reshape(npack, 1, ns)
    lam_t = jnp.concatenate([jnp.real(lam_t), jnp.imag(lam_t)], axis=-1)
    dpack = dskip.reshape(npack, 1, LANES)
    return wt.astype(BF16), wa.astype(BF16), wc.astype(BF16), lam_t, dpack


def _glu_out_body(y_ref, wg_ref, bg_ref, wo_ref, h_ref, gm_ref, lg_ref, lb_ref, o_ref, ytm_ref):
    cb = y_ref.shape[1]
    ncol = y_ref.shape[2] // LANES
    for t in range(SSM_T):
        for c in range(ncol):
            ytm_ref[c, pl.ds(t, cb, stride=SSM_T), :] = y_ref[t, :, c * LANES:(c + 1) * LANES]
    y = jnp.concatenate([ytm_ref[c] for c in range(ncol)], axis=1)
    y_bf = y.astype(BF16)
    gate = jnp.dot(y_bf, wg_ref[...], preferred_element_type=F32) + bg_ref[...]
    z = (y * jax.nn.sigmoid(gate)).astype(BF16)
    out = jnp.dot(z, wo_ref[...], preferred_element_type=F32)
    zz = DN_ALPHA * h_ref[...] + (1.0 + gm_ref[0]) * out
    o_ref[...] = _layer_norm(zz, lg_ref[...], lb_ref[...])


def glu_out(y2, wg_bf, bg, wo_bf, h, gm, ln_g, ln_b, seq):
    n, d = h.shape
    width = y2.shape[2]
    tm = min(256, seq)
    cb = tm // SSM_T
    per_b = seq // tm
    row = lambda i: (i, 0)
    const = lambda i: (0, 0)
    return pl.pallas_call(
        _glu_out_body,
        grid=(n // tm,),
        in_specs=[pl.BlockSpec((SSM_T, cb, width), lambda i: (0, i, 0)),
                  _resident((width, width), const),
                  pl.BlockSpec((1, width), const),
                  _resident((width, d), const),
                  pl.BlockSpec((tm, d), row),
                  pl.BlockSpec((1, 1, d), lambda i: (i // per_b, 0, 0)),
                  pl.BlockSpec((1, d), const),
                  pl.BlockSpec((1, d), const)],
        out_specs=pl.BlockSpec((tm, d), row),
        out_shape=jax.ShapeDtypeStruct((n, d), F32),
        scratch_shapes=[pltpu.VMEM((width // LANES, tm, LANES), F32)],
        compiler_params=_cparams(("arbitrary",)),
        name="glu_out",
    )(y2, wg_bf, bg.reshape(1, width), wo_bf, h, gm, ln_g.reshape(1, d), ln_b.reshape(1, d))


def kernel(x, c, positions, ada_w, ada_b, ln_g, ln_b, ab_in_w, ab_in_b, ab_sinks, ab_gnorm_w, ab_out_w, ab_out_b, hgrn_lb_logits, c_in_w, c_A_re, c_A_im, c_log_dt, c_B_re, c_B_im, c_C_re, c_C_im, c_D, c_glu_w, c_glu_b, c_out_w, router_w, router_b, exp_w1, exp_b1, exp_w2, exp_b2):
    bsz, seq, d = x.shape
    n = bsz * seq
    depth = ada_w.shape[0]
    mod = ada_mod(c, ada_w, ada_b)
    lb_all = jnp.cumsum(jax.nn.softmax(hgrn_lb_logits.astype(F32), axis=0), axis=0)
    half = A_HEAD_DIM // 2
    inv_freq = ROPE_THETA ** (-jnp.arange(half, dtype=F32) / half)
    invf = jnp.tile(inv_freq, LANES // half).reshape(1, LANES)
    pos_col = positions.reshape(n, 1)
    ne_all = depth * exp_w1.shape[1]
    w1_all = exp_w1.reshape(ne_all, d, exp_w1.shape[3])
    w2_all = exp_w2.reshape(ne_all, exp_w2.shape[2], d)
    b1_all = exp_b1.reshape(ne_all, exp_b1.shape[2])
    b2_all = exp_b2.reshape(ne_all, d)
    h = x.reshape(n, d)
    for layer in range(depth):
        m6 = mod[layer].reshape(bsz, 6, 1, d)
        sh_m, sc_m, g_m, sh_f, sc_f, g_f = [m6[:, k] for k in range(6)]
        i = layer // 2
        if layer % 2 == 0:
            proj = inproj_ab(h, sc_m, sh_m, ab_in_w[i].astype(BF16), ab_in_b[i], seq)
            oa = swa_attention(proj, pos_col, invf, ab_sinks[i], bsz, seq)
            ob = hgrn2(proj, lb_all[layer], ab_gnorm_w[i], bsz, seq)
            h = outproj_ab(oa, ob, ab_out_w[i].astype(BF16), ab_out_b[i], h, g_m,
                           ln_g[layer, 0], ln_b[layer, 0], seq)
        else:
            wt, wa, wc, lam_t, dpack = s5_operators(c_A_re[i], c_A_im[i], c_log_dt[i], c_B_re[i], c_B_im[i],
                                                    c_C_re[i], c_C_im[i], c_D[i])
            u2 = inproj_c(h, sc_m, sh_m, c_in_w[i].astype(BF16), seq)
            y2 = s5_mixer(u2, wt, wa, wc, lam_t, dpack, bsz, seq)
            h = glu_out(y2, c_glu_w[i].astype(BF16), c_glu_b[i], c_out_w[i].astype(BF16), h, g_m,
                        ln_g[layer, 0], ln_b[layer, 0], seq)
        h = moe_layer(h, sc_f, sh_f, g_f, ln_g[layer, 1], ln_b[layer, 1], router_w[layer], router_b[layer],
                      layer, w1_all, b1_all, w2_all, b2_all, seq)
    return h.reshape(bsz, seq, d)
```

```python
import functools

import jax
import jax.numpy as jnp
from jax import lax
from jax.experimental import pallas as pl
from jax.experimental.pallas import tpu as pltpu

F32 = jnp.float32
BF16 = jnp.bfloat16
I32 = jnp.int32
U32 = jnp.uint32
HIGHEST = lax.Precision.HIGHEST

DEPTH = 2
A_HEADS = 16
A_KV_HEADS = 2
A_HEAD_DIM = 64
WINDOW = 128
ROPE_THETA = 10000.0
B_HEADS = 8
B_DIM = 128
GLA_CHUNK = 64
C_GROUP = 16
C_STATE = 64
N_EXPERTS = 32
TOP_K = 4
SWIGLU_LIMIT = 7.0
SWIGLU_ALPHA = 1.702
DN_ALPHA = (2 * DEPTH) ** 0.25
LN_EPS = 1e-5
RMS_EPS = 1e-6

LANES = 128
VMEM_LIMIT = 56 * 1024 * 1024

ATT_BLK = 128
HGRN_T = 512
MOE_BLK = 256
SSM_T = 16
SSM_PACK = LANES // C_GROUP
S5_COL_BLOCKS = 4
DMA_GROUP = 8
COMBINE_GROUP = 32
ROW_SUB = 256
W_CHUNK = 256
NEG = -1e30


def _packed_pitch(d):
    return d // (2 * LANES) + 1


def _row_pitch(d):
    return d // LANES + 1


def _cparams(sem):
    return pltpu.CompilerParams(dimension_semantics=sem, vmem_limit_bytes=VMEM_LIMIT)


def _resident(block_shape, index_map):
    return pl.BlockSpec(block_shape, index_map, pipeline_mode=pl.Buffered(1))


def _layer_norm(z, g, b):
    mu = jnp.mean(z, axis=-1, keepdims=True)
    zc = z - mu
    var = jnp.mean(zc * zc, axis=-1, keepdims=True)
    return zc * lax.rsqrt(var + LN_EPS) * g + b


def _ada_body(ct_ref, w_ref, b_ref, o_ref):
    ct = ct_ref[...]
    cond = ct * jax.nn.sigmoid(ct)
    w = w_ref[0]
    rows = [jnp.sum(w * cond[:, b:b + 1], axis=0, keepdims=True) for b in range(ct.shape[1])]
    o_ref[0] = jnp.concatenate(rows, axis=0) + b_ref[0]


def ada_mod(c, ada_w, ada_b):
    depth, d, n6 = ada_w.shape
    bsz = c.shape[0]
    tn = 1536
    return pl.pallas_call(
        _ada_body,
        grid=(depth, n6 // tn),
        in_specs=[pl.BlockSpec((d, bsz), lambda l, j: (0, 0)),
                  pl.BlockSpec((1, d, tn), lambda l, j: (l, 0, j)),
                  pl.BlockSpec((1, 1, tn), lambda l, j: (l, 0, j))],
        out_specs=pl.BlockSpec((1, bsz, tn), lambda l, j: (l, 0, j)),
        out_shape=jax.ShapeDtypeStruct((depth, bsz, n6), F32),
        compiler_params=_cparams(("arbitrary", "arbitrary")),
        name="ada_mod",
    )(c.T, ada_w, ada_b.reshape(depth, 1, n6))


def _inproj_ab_body(x_ref, sc_ref, sh_ref, w_ref, b_ref, o_ref, xb_ref):
    @pl.when(pl.program_id(1) == 0)
    def _():
        xb_ref[...] = (x_ref[...] * (1.0 + sc_ref[0]) + sh_ref[0]).astype(BF16)

    o_ref[...] = jnp.dot(xb_ref[...], w_ref[...], preferred_element_type=F32) + b_ref[...]


def inproj_ab(h, sc, sh, w_bf, b, seq):
    n, d = h.shape
    nc = w_bf.shape[1]
    tm = min(1024, seq)
    tn = 1792
    per_b = seq // tm
    return pl.pallas_call(
        _inproj_ab_body,
        grid=(n // tm, nc // tn),
        in_specs=[pl.BlockSpec((tm, d), lambda i, j: (i, 0)),
                  pl.BlockSpec((1, 1, d), lambda i, j: (i // per_b, 0, 0)),
                  pl.BlockSpec((1, 1, d), lambda i, j: (i // per_b, 0, 0)),
                  pl.BlockSpec((d, tn), lambda i, j: (0, j)),
                  pl.BlockSpec((1, tn), lambda i, j: (0, j))],
        out_specs=pl.BlockSpec((tm, tn), lambda i, j: (i, j)),
        out_shape=jax.ShapeDtypeStruct((n, nc), F32),
        scratch_shapes=[pltpu.VMEM((tm, d), BF16)],
        compiler_params=_cparams(("arbitrary", "arbitrary")),
        name="inproj_ab",
    )(h, sc, sh, w_bf, b.reshape(1, nc))


def _attn_body(sink_ref, q_ref, kc_ref, vc_ref, vp_ref, pc_ref, invf_ref, o_ref, kprev_ref):
    blk = ATT_BLK
    half = A_HEAD_DIM // 2
    grp = A_HEADS // A_KV_HEADS
    i = pl.program_id(1)
    lane = lax.broadcasted_iota(I32, (blk, LANES), 1)
    first_half = (lane % A_HEAD_DIM) < half
    low_head = lane < A_HEAD_DIM
    invf = invf_ref[...]

    def cos_sin(pos_col):
        ang = pos_col.astype(F32) * invf
        s = jnp.sin(ang)
        return jnp.cos(ang), jnp.where(first_half, -s, s)

    def rope(x, cs):
        partner = jnp.where(first_half, pltpu.roll(x, LANES - half, 1), pltpu.roll(x, half, 1))
        return x * cs[0] + partner * cs[1]

    @pl.when(i == 0)
    def _():
        kprev_ref[...] = jnp.zeros_like(kprev_ref)

    cs_c = cos_sin(pc_ref[...])
    k_cur = rope(kc_ref[...], cs_c)
    k2 = jnp.concatenate([kprev_ref[...], k_cur], axis=0)
    kprev_ref[...] = k_cur
    v2 = jnp.concatenate([vp_ref[...], vc_ref[...]], axis=0)
    lane2 = lax.broadcasted_iota(I32, (2 * blk, LANES), 1)
    k2r = pltpu.roll(k2, A_HEAD_DIM, 1)
    v2r = pltpu.roll(v2, A_HEAD_DIM, 1)

    rows = lax.broadcasted_iota(I32, (grp * blk, 2 * blk), 0) % blk
    cols = lax.broadcasted_iota(I32, (grp * blk, 2 * blk), 1)
    valid = (cols > rows) & (cols <= rows + WINDOW) & ((cols >= blk) | (i > 0))

    q_chunks = [rope(q_ref[:, c * LANES:(c + 1) * LANES], cs_c) * (A_HEAD_DIM ** -0.5)
                for c in range(A_HEADS * A_HEAD_DIM // LANES)]

    for g in range(A_KV_HEADS):
        own = (lane2 < A_HEAD_DIM) if g == 0 else (lane2 >= A_HEAD_DIM)
        kg = jnp.where(own, k2, k2r).astype(BF16)
        vg = jnp.where(own, v2, v2r).astype(BF16)
        pieces = []
        sinks = []
        for r in range(grp):
            hd = g * grp + r
            qc = q_chunks[hd // 2]
            keep = low_head if hd % 2 == 0 else jnp.logical_not(low_head)
            pieces.append(jnp.where(keep, qc, 0.0).astype(BF16))
            sinks.append(jnp.full((blk, 1), sink_ref[hd], F32))
        qs = jnp.concatenate(pieces, axis=0)
        sink = jnp.concatenate(sinks, axis=0)
        s = lax.dot_general(qs, kg, (((1,), (1,)), ((), ())), preferred_element_type=F32)
        s = jnp.where(valid, s, NEG)
        m = jnp.maximum(jnp.max(s, axis=-1, keepdims=True), sink)
        p = jnp.exp(s - m)
        den = jnp.sum(p, axis=-1, keepdims=True) + jnp.exp(sink - m)
        o = jnp.dot(p.astype(BF16), vg, preferred_element_type=F32) / den
        for r2 in range(grp // 2):
            c = (g * grp) // 2 + r2
            ev = o[(2 * r2) * blk:(2 * r2 + 1) * blk]
            od = o[(2 * r2 + 1) * blk:(2 * r2 + 2) * blk]
            o_ref[:, c * LANES:(c + 1) * LANES] = jnp.where(low_head, ev, od).astype(o_ref.dtype)


def swa_attention(proj, pos_col, invf, sinks, bsz, seq):
    n = proj.shape[0]
    blk = ATT_BLK
    nb = seq // blk
    aq = A_HEADS * A_HEAD_DIM
    kcol = aq // LANES
    vcol = kcol + 1

    def cur(b, i, s):
        return b * nb + i

    def prev(b, i, s):
        return b * nb + jnp.maximum(i - 1, 0)

    grid_spec = pltpu.PrefetchScalarGridSpec(
        num_scalar_prefetch=1,
        grid=(bsz, nb),
        in_specs=[pl.BlockSpec((blk, aq), lambda b, i, s: (cur(b, i, s), 0)),
                  pl.BlockSpec((blk, LANES), lambda b, i, s: (cur(b, i, s), kcol)),
                  pl.BlockSpec((blk, LANES), lambda b, i, s: (cur(b, i, s), vcol)),
                  pl.BlockSpec((blk, LANES), lambda b, i, s: (prev(b, i, s), vcol)),
                  pl.BlockSpec((blk, 1), lambda b, i, s: (cur(b, i, s), 0)),
                  pl.BlockSpec((1, LANES), lambda b, i, s: (0, 0))],
        out_specs=pl.BlockSpec((blk, aq), lambda b, i, s: (cur(b, i, s), 0)),
        scratch_shapes=[pltpu.VMEM((blk, LANES), F32)],
    )
    return pl.pallas_call(
        _attn_body,
        grid_spec=grid_spec,
        out_shape=jax.ShapeDtypeStruct((n, aq), BF16),
        compiler_params=_cparams(("arbitrary", "arbitrary")),
        name="swa_attention",
    )(sinks, proj, proj, proj, proj, pos_col, invf)


def _hgrn_body(q_ref, f_ref, i_ref, g_ref, lb_ref, gw_ref, o_ref, st_ref):
    L = GLA_CHUNK

    @pl.when(pl.program_id(2) == 0)
    def _():
        st_ref[...] = jnp.zeros_like(st_ref)

    T = q_ref.shape[0]
    nch = T // L
    lb = lb_ref[0]
    gw = gw_ref[0]
    nt = (((1,), (1,)), ((), ()))

    f = lb + (1.0 - lb) * jax.nn.sigmoid(f_ref[...])
    kk = 1.0 - f
    qb = q_ref[...]
    q = qb * jax.nn.sigmoid(qb)
    v = i_ref[...]
    pos = lax.broadcasted_iota(I32, (T, B_DIM), 0) % L
    b = jnp.log(f)
    sh = 1
    while sh < L:
        b = b + jnp.where(pos >= sh, pltpu.roll(b, sh, 0), 0.0)
        sh *= 2

    def per_chunk(row):
        return jnp.concatenate(
            [jnp.broadcast_to(b[c * L + row:c * L + row + 1], (L, B_DIM)) for c in range(nch)], axis=0)

    b_mid = per_chunk(L // 2)
    b_last = per_chunk(L - 1)
    e_mid = jnp.exp(b - b_mid)
    qa = (q * e_mid).astype(BF16)
    ka = (kk * jnp.exp(b_mid - b)).astype(BF16)
    qe = (q * jnp.exp(b)).astype(BF16)
    kd = (kk * jnp.exp(b_last - b)).astype(BF16)
    att = lax.dot_general(qa, ka, nt, preferred_element_type=F32)
    r = lax.broadcasted_iota(I32, (T, T), 0)
    c = lax.broadcasted_iota(I32, (T, T), 1)
    att = jnp.where((r >= c) & (r // L == c // L), att, 0.0)
    o_intra = jnp.dot(att.astype(BF16), v.astype(BF16), preferred_element_type=F32)
    vt = v.T.astype(BF16)
    st = st_ref[...]
    o_inter = []
    for ci in range(nch):
        sl = slice(ci * L, (ci + 1) * L)
        o_inter.append(lax.dot_general(qe[sl], st.astype(BF16), nt, preferred_element_type=F32))
        u_t = jnp.dot(vt[:, sl], kd[sl], preferred_element_type=F32)
        st = jnp.exp(b[ci * L + L - 1:ci * L + L]) * st + u_t
    st_ref[...] = st
    o = o_intra + jnp.concatenate(o_inter, axis=0)
    o = o * lax.rsqrt(jnp.mean(o * o, axis=-1, keepdims=True) + RMS_EPS) * gw
    gb = g_ref[...]
    o_ref[...] = (o * (gb * jax.nn.sigmoid(gb))).astype(o_ref.dtype)


def hgrn2(proj, lb, gnorm_w, bsz, seq):
    n = proj.shape[0]
    t = min(HGRN_T, seq)
    nt = seq // t
    base = (A_HEADS + 2 * A_KV_HEADS) * A_HEAD_DIM // LANES
    h = B_HEADS

    def col(k):
        return lambda b, hd, i: (b * nt + i, base + k * h + hd)

    return pl.pallas_call(
        _hgrn_body,
        grid=(bsz, h, nt),
        in_specs=[pl.BlockSpec((t, LANES), col(0)),
                  pl.BlockSpec((t, LANES), col(1)),
                  pl.BlockSpec((t, LANES), col(2)),
                  pl.BlockSpec((t, LANES), col(3)),
                  pl.BlockSpec((1, 1, B_DIM), lambda b, hd, i: (hd, 0, 0)),
                  pl.BlockSpec((1, 1, B_DIM), lambda b, hd, i: (hd, 0, 0))],
        out_specs=pl.BlockSpec((t, LANES), lambda b, hd, i: (b * nt + i, hd)),
        out_shape=jax.ShapeDtypeStruct((n, h * B_DIM), BF16),
        scratch_shapes=[pltpu.VMEM((B_DIM, B_DIM), F32)],
        compiler_params=_cparams(("arbitrary", "arbitrary", "arbitrary")),
        name="hgrn2",
    )(proj, proj, proj, proj, lb.reshape(h, 1, B_DIM), gnorm_w.reshape(h, 1, B_DIM))


def _outproj_ab_body(oa_ref, ob_ref, w_ref, b_ref, h_ref, gm_ref, lg_ref, lb_ref, o_ref):
    ka = oa_ref.shape[1]
    tm = oa_ref.shape[0]
    sub = min(tm, ROW_SUB)
    for r0 in range(0, tm, sub):
        rs = slice(r0, r0 + sub)
        y = jnp.dot(oa_ref[rs, :], w_ref[:ka, :], preferred_element_type=F32)
        y = y + jnp.dot(ob_ref[rs, :], w_ref[ka:, :], preferred_element_type=F32) + b_ref[...]
        z = DN_ALPHA * h_ref[rs, :] + (1.0 + gm_ref[0]) * y
        o_ref[rs, :] = _layer_norm(z, lg_ref[...], lb_ref[...])


def outproj_ab(oa, ob, w_bf, b, h, gm, ln_g, ln_b, seq):
    n, d = h.shape
    tm = min(512, seq)
    per_b = seq // tm
    ka, kb = oa.shape[1], ob.shape[1]
    row = lambda i: (i, 0)
    const = lambda i: (0, 0)
    return pl.pallas_call(
        _outproj_ab_body,
        grid=(n // tm,),
        in_specs=[pl.BlockSpec((tm, ka), row),
                  pl.BlockSpec((tm, kb), row),
                  _resident((ka + kb, d), const),
                  pl.BlockSpec((1, d), const),
                  pl.BlockSpec((tm, d), row),
                  pl.BlockSpec((1, 1, d), lambda i: (i // per_b, 0, 0)),
                  pl.BlockSpec((1, d), const),
                  pl.BlockSpec((1, d), const)],
        out_specs=pl.BlockSpec((tm, d), row),
        out_shape=jax.ShapeDtypeStruct((n, d), F32),
        compiler_params=_cparams(("arbitrary",)),
        name="outproj_ab",
    )(oa, ob, w_bf, b.reshape(1, d), h, gm, ln_g.reshape(1, d), ln_b.reshape(1, d))


def _router_body(h_ref, sc_ref, sh_ref, rw_ref, rb_ref, meta_ref, cnt_ref, carry_ref, rwh_ref, rwl_ref):
    tm = h_ref.shape[0]

    @pl.when(pl.program_id(0) == 0)
    def _():
        carry_ref[...] = jnp.zeros_like(carry_ref)
        w = rw_ref[...]
        w_hi = w.astype(BF16)
        rwh_ref[...] = w_hi
        rwl_ref[...] = (w - w_hi.astype(F32)).astype(BF16)

    xin = h_ref[...] * (1.0 + sc_ref[0]) + sh_ref[0]
    x_hi = xin.astype(BF16)
    x_lo = (xin - x_hi.astype(F32)).astype(BF16)
    logits = (jnp.dot(x_hi, rwh_ref[...], preferred_element_type=F32)
              + (jnp.dot(x_hi, rwl_ref[...], preferred_element_type=F32)
                 + jnp.dot(x_lo, rwh_ref[...], preferred_element_type=F32))) + rb_ref[...]
    lane = lax.broadcasted_iota(I32, (tm, LANES), 1)
    l = logits
    vals, idxs = [], []
    sel = jnp.zeros((tm, LANES), F32)
    for _ in range(TOP_K):
        m = jnp.max(l, axis=-1, keepdims=True)
        idx = jnp.min(jnp.where(l == m, lane, LANES), axis=-1, keepdims=True)
        hit = lane == idx
        vals.append(m)
        idxs.append(idx)
        sel = jnp.where(hit, 1.0, sel)
        l = jnp.where(hit, -jnp.inf, l)
    es = [jnp.exp(v - vals[0]) for v in vals]
    den = es[0] + es[1] + es[2] + es[3]
    r = lax.broadcasted_iota(I32, (tm, tm), 0)
    c = lax.broadcasted_iota(I32, (tm, tm), 1)
    tril = (r >= c).astype(BF16)
    incl = jnp.dot(tril, sel.astype(BF16), preferred_element_type=F32)
    carry = carry_ref[...]
    excl = incl - sel + carry
    meta = jnp.zeros((tm, LANES), F32)
    for k in range(TOP_K):
        rank = jnp.sum(jnp.where(lane == idxs[k], excl, 0.0), axis=-1, keepdims=True)
        meta = jnp.where(lane == k, idxs[k].astype(F32), meta)
        meta = jnp.where(lane == TOP_K + k, es[k] / den, meta)
        meta = jnp.where(lane == 2 * TOP_K + k, rank, meta)
    meta_ref[...] = meta
    carry = carry + incl[tm - 1:tm, :]
    carry_ref[...] = carry
    cnt_ref[...] = jnp.broadcast_to(carry, cnt_ref.shape)


def moe_router(h, sc, sh, rw_pad, rb_pad, seq):
    n, d = h.shape
    tm = min(512, seq)
    per_b = seq // tm
    return pl.pallas_call(
        _router_body,
        grid=(n // tm,),
        in_specs=[pl.BlockSpec((tm, d), lambda i: (i, 0)),
                  pl.BlockSpec((1, 1, d), lambda i: (i // per_b, 0, 0)),
                  pl.BlockSpec((1, 1, d), lambda i: (i // per_b, 0, 0)),
                  pl.BlockSpec((d, LANES), lambda i: (0, 0)),
                  pl.BlockSpec((1, LANES), lambda i: (0, 0))],
        out_specs=[pl.BlockSpec((tm, LANES), lambda i: (i, 0)),
                   pl.BlockSpec((8, LANES), lambda i: (0, 0))],
        out_shape=[jax.ShapeDtypeStruct((n, LANES), F32),
                   jax.ShapeDtypeStruct((8, LANES), F32)],
        scratch_shapes=[pltpu.VMEM((1, LANES), F32), pltpu.VMEM((d, LANES), BF16), pltpu.VMEM((d, LANES), BF16)],
        compiler_params=_cparams(("arbitrary",)),
        name="moe_router",
    )(h, sc, sh, rw_pad, rb_pad)


def _dispatch_body(pad_ref, dest_ref, h_ref, sc_ref, sh_ref, xs_hbm, buf_ref, zero_ref, sem, zsem):
    tm, d = h_ref.shape
    half = d // 2
    pitch = _packed_pitch(d)
    i = pl.program_id(0)
    last = pl.num_programs(0) - 1
    slot = i % 2
    n_pad = pad_ref.shape[0]

    def wait_slot(s):
        for _ in range(TOP_K):
            pltpu.make_async_copy(buf_ref.at[s], xs_hbm.at[pl.ds(0, tm * pitch)], sem.at[s]).wait()

    @pl.when(i == 0)
    def _():
        zero_ref[...] = jnp.zeros_like(zero_ref)
        buf_ref[...] = jnp.zeros_like(buf_ref)

        def zfill(q2, carry):
            for pr in range(2):
                dst = pad_ref[q2 * 2 + pr] * pitch
                pltpu.make_async_copy(zero_ref, xs_hbm.at[pl.ds(dst, pitch)], zsem.at[0]).start(priority=pr)
            return carry

        lax.fori_loop(0, n_pad // 2, zfill, 0)

    @pl.when(i >= 2)
    def _():
        wait_slot(slot)

    xin = h_ref[...] * (1.0 + sc_ref[0]) + sh_ref[0]
    lo = lax.bitcast_convert_type(xin[:, :half].astype(BF16).astype(F32), U32)
    hi = lax.bitcast_convert_type(xin[:, half:].astype(BF16).astype(F32), U32)
    word = lax.shift_right_logical(lo, jnp.uint32(16)) | (hi & jnp.uint32(0xFFFF0000))
    for s in range(half // LANES):
        buf_ref[slot, pl.ds(s, tm, stride=pitch), :] = word[:, s * LANES:(s + 1) * LANES]

    def issue(g, carry):
        for rr in range(DMA_GROUP):
            r = g * DMA_GROUP + rr
            src = buf_ref.at[slot, pl.ds(r * pitch, pitch)]
            for k in range(TOP_K):
                dst = dest_ref[0, 0, r * TOP_K + k] * pitch
                pltpu.make_async_copy(src, xs_hbm.at[pl.ds(dst, pitch)], sem.at[slot]).start(priority=k % 2)
        return carry

    lax.fori_loop(0, tm // DMA_GROUP, issue, 0)

    @pl.when(i == last)
    def _():
        wait_slot(slot)

        @pl.when(i >= 1)
        def _():
            wait_slot(1 - slot)

        for _ in range(n_pad // tm):
            pltpu.make_async_copy(buf_ref.at[0], xs_hbm.at[pl.ds(0, tm * pitch)], zsem.at[0]).wait()


def moe_dispatch(h, sc, sh, dest, pad_dest, n_rows, seq):
    n, d = h.shape
    tm = min(256, seq)
    per_b = seq // tm
    pitch = _packed_pitch(d)
    assert pad_dest.shape[0] % tm == 0
    grid_spec = pltpu.PrefetchScalarGridSpec(
        num_scalar_prefetch=1,
        grid=(n // tm,),
        in_specs=[pl.BlockSpec((1, 1, tm * TOP_K), lambda i, p: (i, 0, 0), memory_space=pltpu.SMEM),
                  pl.BlockSpec((tm, d), lambda i, p: (i, 0)),
                  pl.BlockSpec((1, 1, d), lambda i, p: (i // per_b, 0, 0)),
                  pl.BlockSpec((1, 1, d), lambda i, p: (i // per_b, 0, 0))],
        out_specs=pl.BlockSpec(memory_space=pl.ANY),
        scratch_shapes=[pltpu.VMEM((2, tm * pitch, LANES), U32), pltpu.VMEM((pitch, LANES), U32),
                        pltpu.SemaphoreType.DMA((2,)), pltpu.SemaphoreType.DMA((1,))],
    )
    return pl.pallas_call(
        _dispatch_body,
        grid_spec=grid_spec,
        out_shape=jax.ShapeDtypeStruct((n_rows * pitch, LANES), U32),
        compiler_params=_cparams(("arbitrary",)),
        name="moe_dispatch",
    )(pad_dest, dest.reshape(n // tm, 1, tm * TOP_K), h, sc, sh)


SCHED_MODE, SCHED_BLOCK, SCHED_EXPERT, SCHED_SLOT, SCHED_DL_EXPERT, SCHED_DL_C1, SCHED_DL_C2, SCHED_DL_SLOT = range(8)
MODE_IDLE, MODE_COMPUTE, MODE_ZERO = 0, 1, 2


def _expert_body(sched_ref, xs_ref, w1c_ref, w2c_ref, b1_ref, b2_ref, y_ref, wb1a_ref, wb1b_ref, wb2a_ref, wb2b_ref):
    j = pl.program_id(0)
    nstep = pl.num_programs(0)
    d = wb1a_ref.shape[0]
    ff = wb2a_ref.shape[0]
    rt = d // LANES
    pitch = _row_pitch(d)
    xpitch = _packed_pitch(d)
    blk = xs_ref.shape[0] // xpitch
    cache = ((wb1a_ref, wb2a_ref), (wb1b_ref, wb2b_ref))

    def sched(field):
        return sched_ref[field * nstep + j]

    def cast_chunks(wb1_ref, wb2_ref):
        r1 = pl.multiple_of(sched(SCHED_DL_C1) * W_CHUNK, W_CHUNK)
        r2 = pl.multiple_of(sched(SCHED_DL_C2) * W_CHUNK, W_CHUNK)
        wb1_ref[pl.ds(r1, W_CHUNK), :] = w1c_ref[0].astype(BF16)
        wb2_ref[pl.ds(r2, W_CHUNK), :] = w2c_ref[0].astype(BF16)

    def compute(wb1_ref, wb2_ref):
        words = [xs_ref[pl.ds(s, blk, stride=xpitch), :] for s in range(rt // 2)]
        lo = [lax.bitcast_convert_type(lax.shift_left(w, jnp.uint32(16)), F32).astype(BF16) for w in words]
        hi = [lax.bitcast_convert_type(w & jnp.uint32(0xFFFF0000), F32).astype(BF16) for w in words]
        x = jnp.concatenate(lo + hi, axis=1)
        hid = jnp.dot(x, wb1_ref[...], preferred_element_type=F32) + b1_ref[0]
        gate = jnp.minimum(hid[:, :ff], SWIGLU_LIMIT)
        lin = jnp.clip(hid[:, ff:], -SWIGLU_LIMIT, SWIGLU_LIMIT)
        act = gate * jax.nn.sigmoid(SWIGLU_ALPHA * gate) * (lin + 1.0)
        y = jnp.dot(act.astype(BF16), wb2_ref[...], preferred_element_type=F32) + b2_ref[0]
        for s in range(rt):
            y_ref[pl.ds(s, blk, stride=pitch), :] = y[:, s * LANES:(s + 1) * LANES]
        for s in range(rt, pitch):
            y_ref[pl.ds(s, blk, stride=pitch), :] = jnp.zeros((blk, LANES), F32)

    mode = sched(SCHED_MODE)
    slot = sched(SCHED_SLOT)
    dl_slot = sched(SCHED_DL_SLOT)
    for par in range(2):
        @pl.when((mode == MODE_COMPUTE) & (slot == par))
        def _(par=par):
            cast_chunks(*cache[1 - par])
            compute(*cache[par])

        @pl.when((mode != MODE_COMPUTE) & (dl_slot == par))
        def _(par=par):
            cast_chunks(*cache[par])

    @pl.when(mode == MODE_ZERO)
    def _():
        y_ref[...] = jnp.zeros_like(y_ref)


def expert_schedule(padded, n_used, nblk, layer, ne, need, need2):
    nb = padded // MOE_BLK
    seg_len = jnp.concatenate([jnp.full((1,), need, I32), jnp.maximum(nb[:-1], need), nb[-1:]])
    seg_end = jnp.cumsum(seg_len)
    seg_start = seg_end - seg_len
    nstep = nblk + (ne + 1) * need
    j = jnp.arange(nstep, dtype=I32)
    seg = jnp.sum(seg_end[None, :] <= j[:, None], axis=-1).astype(I32)
    in_seg = seg <= ne
    segc = jnp.minimum(seg, ne)
    loc = j - seg_start[segc]
    expert = jnp.clip(segc - 1, 0, ne - 1)
    compute = in_seg & (segc >= 1) & (loc < nb[expert])
    done = jnp.cumsum(compute.astype(I32)) - compute.astype(I32)
    tail = j - seg_end[-1]
    ntail = nblk - n_used
    zero = jnp.logical_not(in_seg) & (tail < ntail)
    parked = jnp.where(jnp.logical_not(in_seg) & (ntail > 0), nblk - 1, jnp.maximum(done - 1, 0))
    block = jnp.where(compute, done, jnp.where(zero, n_used + tail, parked))
    mode = jnp.where(compute, MODE_COMPUTE, jnp.where(zero, MODE_ZERO, MODE_IDLE))
    deliver = in_seg & (segc <= ne - 1) & (loc < need)
    last = lax.cummax(jnp.where(deliver, j, -1), axis=0)
    dl_expert = segc[last]
    dl_c1 = loc[last]
    dl_c2 = jnp.minimum(loc[last], need2 - 1)
    table = jnp.stack([mode, block, expert + layer * ne, expert % 2,
                       dl_expert + layer * ne, dl_c1, dl_c2, dl_expert % 2])
    return table.astype(I32).reshape(-1), nstep


def moe_experts(xs, sched, nstep, w1, b1, w2, b2):
    ne, d, ff2 = w1.shape
    pitch = _row_pitch(d)
    xpitch = _packed_pitch(d)
    n_rows = xs.shape[0] // xpitch
    ff = ff2 // 2

    def fld(s, j, field):
        return s[field * nstep + j]

    grid_spec = pltpu.PrefetchScalarGridSpec(
        num_scalar_prefetch=1,
        grid=(nstep,),
        in_specs=[pl.BlockSpec((MOE_BLK * xpitch, LANES), lambda j, s: (fld(s, j, SCHED_BLOCK), 0)),
                  pl.BlockSpec((1, W_CHUNK, ff2),
                               lambda j, s: (fld(s, j, SCHED_DL_EXPERT), fld(s, j, SCHED_DL_C1), 0)),
                  pl.BlockSpec((1, W_CHUNK, d),
                               lambda j, s: (fld(s, j, SCHED_DL_EXPERT), fld(s, j, SCHED_DL_C2), 0)),
                  pl.BlockSpec((1, 1, ff2), lambda j, s: (fld(s, j, SCHED_EXPERT), 0, 0)),
                  pl.BlockSpec((1, 1, d), lambda j, s: (fld(s, j, SCHED_EXPERT), 0, 0))],
        out_specs=pl.BlockSpec((MOE_BLK * pitch, LANES), lambda j, s: (fld(s, j, SCHED_BLOCK), 0)),
        scratch_shapes=[pltpu.VMEM((d, ff2), BF16), pltpu.VMEM((d, ff2), BF16),
                        pltpu.VMEM((ff, d), BF16), pltpu.VMEM((ff, d), BF16)],
    )
    return pl.pallas_call(
        _expert_body,
        grid_spec=grid_spec,
        out_shape=jax.ShapeDtypeStruct((n_rows * pitch, LANES), F32),
        compiler_params=_cparams(("arbitrary",)),
        name="moe_experts",
    )(sched, xs, w1, w2, b1.reshape(ne, 1, ff2), b2.reshape(ne, 1, d))


def _combine_body(dc_ref, dn_ref, meta_ref, h_ref, gf_ref, lg_ref, lb_ref, y_hbm, o_ref, buf_ref, sem):
    tm, d = h_ref.shape
    rt = d // LANES
    pitch = _row_pitch(d)
    grp = COMBINE_GROUP
    i = pl.program_id(0)
    n = pl.num_programs(0)
    slot = i % 2

    def issue_rows(dref, s, r0):
        for rr in range(grp):
            r = r0 + rr
            for k in range(TOP_K):
                src = dref[0, 0, r * TOP_K + k] * pitch
                pltpu.make_async_copy(y_hbm.at[pl.ds(src, rt)], buf_ref.at[s, k, pl.ds(r * pitch, rt)],
                                      sem.at[s]).start(priority=k % 2)

    @pl.when(i == 0)
    def _():
        def body(g, carry):
            issue_rows(dc_ref, 0, g * grp)
            return carry
        lax.fori_loop(0, tm // grp, body, 0)

    for k in range(TOP_K):
        pltpu.make_async_copy(y_hbm.at[pl.ds(0, tm * rt)], buf_ref.at[slot, k, pl.ds(0, tm * rt)],
                              sem.at[slot]).wait()

    def group(g, carry, prefetch):
        r0 = pl.multiple_of(g * grp, grp)
        if prefetch:
            issue_rows(dn_ref, 1 - slot, r0)
        meta = meta_ref[pl.ds(r0, grp), :]
        gates = [jnp.broadcast_to(meta[:, TOP_K + k:TOP_K + k + 1], (grp, LANES)) for k in range(TOP_K)]
        for s in range(rt):
            a = gates[0] * buf_ref[slot, 0, pl.ds(r0 * pitch + s, grp, stride=pitch), :]
            for k in range(1, TOP_K):
                a = a + gates[k] * buf_ref[slot, k, pl.ds(r0 * pitch + s, grp, stride=pitch), :]
            cs = slice(s * LANES, (s + 1) * LANES)
            o_ref[pl.ds(r0, grp), cs] = DN_ALPHA * h_ref[pl.ds(r0, grp), cs] + (1.0 + gf_ref[0, :, cs]) * a
        return carry

    @pl.when(i + 1 < n)
    def _():
        lax.fori_loop(0, tm // grp, functools.partial(group, prefetch=True), 0)

    @pl.when(i + 1 == n)
    def _():
        lax.fori_loop(0, tm // grp, functools.partial(group, prefetch=False), 0)

    o_ref[...] = _layer_norm(o_ref[...], lg_ref[...], lb_ref[...])


def moe_combine(y, dest, meta, h, gf, ln_g, ln_b, seq):
    n, d = h.shape
    tm = min(256, seq)
    per_b = seq // tm
    nblk = n // tm
    dest3 = dest.reshape(nblk, 1, tm * TOP_K)
    row = lambda i: (i, 0)
    const = lambda i: (0, 0)
    return pl.pallas_call(
        _combine_body,
        grid=(nblk,),
        in_specs=[pl.BlockSpec((1, 1, tm * TOP_K), lambda i: (i, 0, 0), memory_space=pltpu.SMEM),
                  pl.BlockSpec((1, 1, tm * TOP_K), lambda i: (jnp.minimum(i + 1, nblk - 1), 0, 0),
                               memory_space=pltpu.SMEM),
                  pl.BlockSpec((tm, LANES), row),
                  pl.BlockSpec((tm, d), row),
                  pl.BlockSpec((1, 1, d), lambda i: (i // per_b, 0, 0)),
                  pl.BlockSpec((1, d), const),
                  pl.BlockSpec((1, d), const),
                  pl.BlockSpec(memory_space=pl.ANY)],
        out_specs=pl.BlockSpec((tm, d), row),
        out_shape=jax.ShapeDtypeStruct((n, d), F32),
        scratch_shapes=[pltpu.VMEM((2, TOP_K, tm * _row_pitch(d), LANES), F32), pltpu.SemaphoreType.DMA((2,))],
        compiler_params=_cparams(("arbitrary",)),
        name="moe_combine",
    )(dest3, dest3, meta, h, gf, ln_g.reshape(1, d), ln_b.reshape(1, d), y)


def moe_layer(h, sc, sh, gf, ln_g, ln_b, router_w, router_b, layer, w1, b1, w2, b2, seq):
    n, d = h.shape
    ne = router_w.shape[1]
    rw_pad = jnp.pad(router_w, ((0, 0), (0, LANES - ne)))
    rb_pad = jnp.pad(router_b, (0, LANES - ne), constant_values=NEG).reshape(1, LANES)
    meta, cnt = moe_router(h, sc, sh, rw_pad, rb_pad, seq)
    idx4 = meta[:, 0:TOP_K].astype(I32)
    rank4 = meta[:, 2 * TOP_K:3 * TOP_K].astype(I32)
    counts = cnt[0, :ne].astype(I32)
    padded = (counts + MOE_BLK - 1) // MOE_BLK * MOE_BLK
    end = jnp.cumsum(padded)
    start = end - padded
    onehot = idx4[:, :, None] == jnp.arange(ne, dtype=I32)[None, None, :]
    dest = jnp.sum(jnp.where(onehot, start[None, None, :], 0), axis=-1) + rank4
    n_rows = -(-(n * TOP_K + ne * (MOE_BLK - 1)) // MOE_BLK) * MOE_BLK
    nblk = n_rows // MOE_BLK
    blk0 = jnp.arange(nblk, dtype=I32) * MOE_BLK
    n_used = (end[-1] // MOE_BLK).astype(I32)
    n_pad = n_rows - n * TOP_K
    seg_cnt = jnp.concatenate([padded - counts, (n_rows - end[-1])[None]])
    seg_base = jnp.concatenate([start + counts, end[-1:]])
    seg_end = jnp.cumsum(seg_cnt)
    q = jnp.arange(n_pad, dtype=I32)
    seg = jnp.sum(seg_end[None, :] <= q[:, None], axis=-1)
    pad_dest = (seg_base[seg] + q - (seg_end - seg_cnt)[seg]).astype(I32)
    xs = moe_dispatch(h, sc, sh, dest.reshape(-1), pad_dest, n_rows, seq)
    sched, nstep = expert_schedule(padded, n_used, nblk, layer, ne, w1.shape[1] // W_CHUNK, w2.shape[1] // W_CHUNK)
    y = moe_experts(xs, sched, nstep, w1, b1, w2, b2)
    return moe_combine(y, dest.reshape(-1), meta, h, gf, ln_g, ln_b, seq)


def _inproj_c_body(x_ref, sc_ref, sh_ref, w_ref, o_ref, acc_ref):
    tm = x_ref.shape[0]
    xb = (x_ref[...] * (1.0 + sc_ref[0]) + sh_ref[0]).astype(BF16)
    acc = jnp.dot(xb, w_ref[...], preferred_element_type=F32)
    ncol = acc.shape[1] // LANES
    for c in range(ncol):
        acc_ref[c] = acc[:, c * LANES:(c + 1) * LANES]
    for t in range(SSM_T):
        for c in range(ncol):
            o_ref[t, :, c * LANES:(c + 1) * LANES] = acc_ref[c, pl.ds(t, tm // SSM_T, stride=SSM_T), :]


def inproj_c(h, sc, sh, w_bf, seq):
    n, d = h.shape
    width = w_bf.shape[1]
    tm = min(512, seq)
    per_b = seq // tm
    return pl.pallas_call(
        _inproj_c_body,
        grid=(n // tm,),
        in_specs=[pl.BlockSpec((tm, d), lambda i: (i, 0)),
                  pl.BlockSpec((1, 1, d), lambda i: (i // per_b, 0, 0)),
                  pl.BlockSpec((1, 1, d), lambda i: (i // per_b, 0, 0)),
                  _resident((d, width), lambda i: (0, 0))],
        out_specs=pl.BlockSpec((SSM_T, tm // SSM_T, width), lambda i: (0, i, 0)),
        out_shape=jax.ShapeDtypeStruct((SSM_T, n // SSM_T, width), F32),
        scratch_shapes=[pltpu.VMEM((width // LANES, tm, LANES), F32)],
        compiler_params=_cparams(("arbitrary",)),
        name="inproj_c",
    )(h, sc, sh, w_bf)


def _s5_body(u_ref, wt_ref, wa_ref, wc_ref, lam_ref, d_ref, o_ref, st_ref, v_ref, xs_ref):
    nb, cb = u_ref.shape[1], u_ref.shape[2]
    ns = st_ref.shape[1] // 2
    kdim = SSM_T * LANES

    @pl.when(pl.program_id(1) == 0)
    def _():
        st_ref[...] = jnp.zeros_like(st_ref)

    xcat = jnp.concatenate(
        [jnp.concatenate([u_ref[t, b].astype(BF16) for t in range(SSM_T)], axis=1) for b in range(nb)], axis=0)
    v = jnp.dot(xcat, wa_ref[0], preferred_element_type=F32)
    nslab = 2 * ns // LANES
    for c in range(nslab):
        v_ref[c] = v[:, c * LANES:(c + 1) * LANES]
    lam = lam_ref[0]
    a_re, a_im = lam[:, :ns], lam[:, ns:]

    def step(r, st):
        for c in range(nslab):
            xs_ref[c, pl.ds(r, nb, stride=cb), :] = st[:, c * LANES:(c + 1) * LANES]
        v = jnp.concatenate([v_ref[c, pl.ds(r, nb, stride=cb), :] for c in range(nslab)], axis=1)
        s_re, s_im = st[:, :ns], st[:, ns:]
        n_re = a_re * s_re - a_im * s_im + v[:, :ns]
        n_im = a_re * s_im + a_im * s_re + v[:, ns:]
        return jnp.concatenate([n_re, n_im], axis=1)

    st_ref[...] = lax.fori_loop(0, cb, step, st_ref[...])
    xs_bf = jnp.concatenate([xs_ref[c].astype(BF16) for c in range(nslab)], axis=1)
    dvec = d_ref[0]
    for q in range(S5_COL_BLOCKS):
        c0, c1 = q * kdim // S5_COL_BLOCKS, (q + 1) * kdim // S5_COL_BLOCKS
        y = jnp.dot(xcat[:, :c1], wt_ref[0, :c1, c0:c1], preferred_element_type=F32)
        y = y + jnp.dot(xs_bf, wc_ref[0, :, c0:c1], preferred_element_type=F32)
        for tt in range((c1 - c0) // LANES):
            t = c0 // LANES + tt
            for b in range(nb):
                yt = y[b * cb:(b + 1) * cb, tt * LANES:(tt + 1) * LANES] + dvec * u_ref[t, b]
                gel = 0.5 * yt * (1.0 + jnp.tanh(0.7978845608028654 * (yt + 0.044715 * yt * yt * yt)))
                o_ref[t, b] = gel.astype(o_ref.dtype)


def s5_mixer(u2, wt, wa, wc, lam_t, dpack, bsz, seq):
    _, nchunk, width = u2.shape
    npack = width // LANES
    per_b = seq // SSM_T
    cb = min(256, per_b)
    nblk = per_b // cb
    ns2 = wa.shape[2]
    kdim = SSM_T * LANES
    pk = lambda j, i: (j, 0, 0)
    u4 = u2.reshape(SSM_T, bsz, per_b, width)
    y4 = pl.pallas_call(
        _s5_body,
        grid=(npack, nblk),
        in_specs=[pl.BlockSpec((SSM_T, bsz, cb, LANES), lambda j, i: (0, 0, i, j)),
                  _resident((1, kdim, kdim), pk),
                  _resident((1, kdim, ns2), pk),
                  _resident((1, ns2, kdim), pk),
                  pl.BlockSpec((1, 1, ns2), pk),
                  pl.BlockSpec((1, 1, LANES), pk)],
        out_specs=pl.BlockSpec((SSM_T, bsz, cb, LANES), lambda j, i: (0, 0, i, j)),
        out_shape=jax.ShapeDtypeStruct((SSM_T, bsz, per_b, width), F32),
        scratch_shapes=[pltpu.VMEM((bsz, ns2), F32), pltpu.VMEM((ns2 // LANES, bsz * cb, LANES), F32),
                        pltpu.VMEM((ns2 // LANES, bsz * cb, LANES), F32)],
        compiler_params=_cparams(("arbitrary", "arbitrary")),
        name="s5_mixer",
    )(u4, wt, wa, wc, lam_t, dpack)
    return y4.reshape(SSM_T, nchunk, width)


def s5_operators(a_re, a_im, log_dt, b_re, b_im, c_re, c_im, dskip):
    g, p = a_re.shape
    cg = b_re.shape[2]
    npack = g // SSM_PACK
    ns2 = SSM_PACK * 2 * p
    t = SSM_T
    lam = lax.complex(a_re, a_im)
    dt = jnp.exp(log_dt)[:, None]
    lam_bar = jnp.exp(lam * dt)
    b_bar = ((lam_bar - 1.0) / lam)[..., None] * lax.complex(b_re, b_im)
    c_mat = lax.complex(c_re, c_im)
    kpow = jnp.arange(t + 1, dtype=F32)[:, None, None]
    pows = jnp.exp((lam * dt)[None] * kpow)
    lane_c = jnp.arange(LANES) % cg
    lane_g = jnp.arange(LANES) // cg
    tile = (lane_c[None, :] == jnp.arange(cg)[:, None]).astype(F32)
    diag = lane_g[:, None] == lane_g[None, :]
    kern = jnp.real(jnp.einsum('gcp,kgp,gpd->kgdc', c_mat, pows[:t], b_bar, precision=HIGHEST))
    kfull = jnp.where(diag, jnp.dot(kern.reshape(t, npack, LANES, cg), tile, precision=HIGHEST), 0.0)
    kfull = kfull.astype(BF16)
    kcat = jnp.concatenate([kfull[tau] for tau in range(t)], axis=-1)
    wt = jnp.stack([jnp.pad(kcat[:, :, :(t - s) * LANES], ((0, 0), (0, 0), (s * LANES, 0)))
                    for s in range(t)], axis=1)
    wt = wt.reshape(npack, t * LANES, t * LANES)
    ns = SSM_PACK * p
    col_g = jnp.arange(ns) // p
    tile_p = (jnp.arange(ns)[None, :] % p == jnp.arange(p)[:, None]).astype(F32)
    wa_c = pows[t - 1 - jnp.arange(t)][:, :, :, None] * b_bar[None]
    wa_c = wa_c.reshape(t, npack, SSM_PACK, p, cg).transpose(1, 0, 2, 4, 3)
    wa_c = wa_c.reshape(npack, t * LANES, p)
    own_a = jnp.tile(lane_g, t)[:, None] == col_g[None, :]
    wa = jnp.concatenate([jnp.where(own_a, jnp.dot(part, tile_p, precision=HIGHEST), 0.0).astype(BF16)
                          for part in (jnp.real(wa_c), jnp.imag(wa_c))], axis=-1)
    wc_c = c_mat[None] * pows[1:t + 1][:, :, None, :]
    wc_c = wc_c.transpose(0, 1, 3, 2).reshape(t, npack, ns, cg)
    cc = jnp.concatenate([jnp.real(wc_c), -jnp.imag(wc_c)], axis=2)
    own_c = (jnp.arange(ns2) % ns // p)[:, None] == lane_g[None, :]
    wcf = jnp.where(own_c, jnp.dot(cc, tile, precision=HIGHEST), 0.0).astype(BF16)
    wc = jnp.concatenate([wcf[tt] for tt in range(t)], axis=-1)
    lam_t = pows[t].reshape(npack, 1, ns)
    lam_t = jnp.concatenate([jnp.real(lam_t), jnp.imag(lam_t)], axis=-1)
    dpack = dskip.reshape(npack, 1, LANES)
    return wt, wa, wc, lam_t, dpack


def _glu_out_body(y_ref, wg_ref, bg_ref, wo_ref, h_ref, gm_ref, lg_ref, lb_ref, o_ref, ytm_ref):
    cb = y_ref.shape[1]
    ncol = y_ref.shape[2] // LANES
    for t in range(SSM_T):
        for c in range(ncol):
            ytm_ref[c, pl.ds(t, cb, stride=SSM_T), :] = y_ref[t, :, c * LANES:(c + 1) * LANES]
    y = jnp.concatenate([ytm_ref[c] for c in range(ncol)], axis=1)
    y_bf = y.astype(BF16)
    gate = jnp.dot(y_bf, wg_ref[...], preferred_element_type=F32) + bg_ref[...]
    z = (y * jax.nn.sigmoid(gate)).astype(BF16)
    out = jnp.dot(z, wo_ref[...], preferred_element_type=F32)
    zz = DN_ALPHA * h_ref[...] + (1.0 + gm_ref[0]) * out
    o_ref[...] = _layer_norm(zz, lg_ref[...], lb_ref[...])


def glu_out(y2, wg_bf, bg, wo_bf, h, gm, ln_g, ln_b, seq):
    n, d = h.shape
    width = y2.shape[2]
    tm = min(256, seq)
    cb = tm // SSM_T
    per_b = seq // tm
    row = lambda i: (i, 0)
    const = lambda i: (0, 0)
    return pl.pallas_call(
        _glu_out_body,
        grid=(n // tm,),
        in_specs=[pl.BlockSpec((SSM_T, cb, width), lambda i: (0, i, 0)),
                  _resident((width, width), const),
                  pl.BlockSpec((1, width), const),
                  _resident((width, d), const),
                  pl.BlockSpec((tm, d), row),
                  pl.BlockSpec((1, 1, d), lambda i: (i // per_b, 0, 0)),
                  pl.BlockSpec((1, d), const),
                  pl.BlockSpec((1, d), const)],
        out_specs=pl.BlockSpec((tm, d), row),
        out_shape=jax.ShapeDtypeStruct((n, d), F32),
        scratch_shapes=[pltpu.VMEM((width // LANES, tm, LANES), F32)],
        compiler_params=_cparams(("arbitrary",)),
        name="glu_out",
    )(y2, wg_bf, bg.reshape(1, width), wo_bf, h, gm, ln_g.reshape(1, d), ln_b.reshape(1, d))


def kernel(x, c, positions, ada_w, ada_b, ln_g, ln_b, ab_in_w, ab_in_b, ab_sinks, ab_gnorm_w, ab_out_w, ab_out_b, hgrn_lb_logits, c_in_w, c_A_re, c_A_im, c_log_dt, c_B_re, c_B_im, c_C_re, c_C_im, c_D, c_glu_w, c_glu_b, c_out_w, router_w, router_b, exp_w1, exp_b1, exp_w2, exp_b2):
    bsz, seq, d = x.shape
    n = bsz * seq
    depth = ada_w.shape[0]
    mod = ada_mod(c, ada_w, ada_b)
    lb_all = jnp.cumsum(jax.nn.softmax(hgrn_lb_logits.astype(F32), axis=0), axis=0)
    half = A_HEAD_DIM // 2
    inv_freq = ROPE_THETA ** (-jnp.arange(half, dtype=F32) / half)
    invf = jnp.tile(inv_freq, LANES // half).reshape(1, LANES)
    pos_col = positions.reshape(n, 1)
    ne_all = depth * exp_w1.shape[1]
    w1_all = exp_w1.reshape(ne_all, d, exp_w1.shape[3])
    w2_all = exp_w2.reshape(ne_all, exp_w2.shape[2], d)
    b1_all = exp_b1.reshape(ne_all, exp_b1.shape[2])
    b2_all = exp_b2.reshape(ne_all, d)
    h = x.reshape(n, d)
    for layer in range(depth):
        m6 = mod[layer].reshape(bsz, 6, 1, d)
        sh_m, sc_m, g_m, sh_f, sc_f, g_f = [m6[:, k] for k in range(6)]
        i = layer // 2
        if layer % 2 == 0:
            proj = inproj_ab(h, sc_m, sh_m, ab_in_w[i].astype(BF16), ab_in_b[i], seq)
            oa = swa_attention(proj, pos_col, invf, ab_sinks[i], bsz, seq)
            ob = hgrn2(proj, lb_all[layer], ab_gnorm_w[i], bsz, seq)
            h = outproj_ab(oa, ob, ab_out_w[i].astype(BF16), ab_out_b[i], h, g_m,
                           ln_g[layer, 0], ln_b[layer, 0], seq)
        else:
            wt, wa, wc, lam_t, dpack = s5_operators(c_A_re[i], c_A_im[i], c_log_dt[i], c_B_re[i], c_B_im[i],
                                                    c_C_re[i], c_C_im[i], c_D[i])
            u2 = inproj_c(h, sc_m, sh_m, c_in_w[i].astype(BF16), seq)
            y2 = s5_mixer(u2, wt, wa, wc, lam_t, dpack, bsz, seq)
            h = glu_out(y2, c_glu_w[i].astype(BF16), c_glu_b[i], c_out_w[i].astype(BF16), h, g_m,
                        ln_g[layer, 0], ln_b[layer, 0], seq)
        h = moe_layer(h, sc_f, sh_f, g_f, ln_g[layer, 1], ln_b[layer, 1], router_w[layer], router_b[layer],
                      layer, w1_all, b1_all, w2_all, b2_all, seq)
    return h.reshape(bsz, seq, d)
```

```python
import functools

import jax
import jax.numpy as jnp
from jax import lax
from jax.experimental import pallas as pl
from jax.experimental.pallas import tpu as pltpu

F32 = jnp.float32
BF16 = jnp.bfloat16
I32 = jnp.int32
U32 = jnp.uint32
HIGHEST = lax.Precision.HIGHEST

DEPTH = 2
A_HEADS = 16
A_KV_HEADS = 2
A_HEAD_DIM = 64
WINDOW = 128
ROPE_THETA = 10000.0
B_HEADS = 8
B_DIM = 128
GLA_CHUNK = 64
C_GROUP = 16
C_STATE = 64
N_EXPERTS = 32
TOP_K = 4
SWIGLU_LIMIT = 7.0
SWIGLU_ALPHA = 1.702
DN_ALPHA = (2 * DEPTH) ** 0.25
LN_EPS = 1e-5
RMS_EPS = 1e-6

LANES = 128
VMEM_LIMIT = 56 * 1024 * 1024

ATT_BLK = 128
HGRN_T = 512
HGRN_HEADS_PER_STEP = 2
MOE_BLK = 256
SSM_T = 16
SSM_PACK = LANES // C_GROUP
S5_COL_BLOCKS = 4
DMA_GROUP = 8
COMBINE_GROUP = 32
ROW_SUB = 256
W_CHUNK = 256
NEG = -1e30


def _packed_pitch(d):
    return d // (2 * LANES) + 1


def _row_pitch(d):
    return d // LANES + 1


def _cparams(sem):
    return pltpu.CompilerParams(dimension_semantics=sem, vmem_limit_bytes=VMEM_LIMIT)


def _resident(block_shape, index_map):
    return pl.BlockSpec(block_shape, index_map, pipeline_mode=pl.Buffered(1))


def _layer_norm(z, g, b):
    mu = jnp.mean(z, axis=-1, keepdims=True)
    zc = z - mu
    var = jnp.mean(zc * zc, axis=-1, keepdims=True)
    return zc * lax.rsqrt(var + LN_EPS) * g + b


def _ada_body(ct_ref, w_ref, b_ref, o_ref):
    ct = ct_ref[...]
    cond = ct * jax.nn.sigmoid(ct)
    w = w_ref[0]
    rows = [jnp.sum(w * cond[:, b:b + 1], axis=0, keepdims=True) for b in range(ct.shape[1])]
    o_ref[0] = jnp.concatenate(rows, axis=0) + b_ref[0]


def ada_mod(c, ada_w, ada_b):
    depth, d, n6 = ada_w.shape
    bsz = c.shape[0]
    tn = 1536
    return pl.pallas_call(
        _ada_body,
        grid=(depth, n6 // tn),
        in_specs=[pl.BlockSpec((d, bsz), lambda l, j: (0, 0)),
                  pl.BlockSpec((1, d, tn), lambda l, j: (l, 0, j)),
                  pl.BlockSpec((1, 1, tn), lambda l, j: (l, 0, j))],
        out_specs=pl.BlockSpec((1, bsz, tn), lambda l, j: (l, 0, j)),
        out_shape=jax.ShapeDtypeStruct((depth, bsz, n6), F32),
        compiler_params=_cparams(("arbitrary", "arbitrary")),
        name="ada_mod",
    )(c.T, ada_w, ada_b.reshape(depth, 1, n6))


def _inproj_ab_body(x_ref, sc_ref, sh_ref, w_ref, b_ref, o_ref, xb_ref):
    @pl.when(pl.program_id(1) == 0)
    def _():
        xb_ref[...] = (x_ref[...] * (1.0 + sc_ref[0]) + sh_ref[0]).astype(BF16)

    o_ref[...] = jnp.dot(xb_ref[...], w_ref[...], preferred_element_type=F32) + b_ref[...]


def inproj_ab(h, sc, sh, w_bf, b, seq):
    n, d = h.shape
    nc = w_bf.shape[1]
    tm = min(1024, seq)
    tn = 1792
    per_b = seq // tm
    return pl.pallas_call(
        _inproj_ab_body,
        grid=(n // tm, nc // tn),
        in_specs=[pl.BlockSpec((tm, d), lambda i, j: (i, 0)),
                  pl.BlockSpec((1, 1, d), lambda i, j: (i // per_b, 0, 0)),
                  pl.BlockSpec((1, 1, d), lambda i, j: (i // per_b, 0, 0)),
                  pl.BlockSpec((d, tn), lambda i, j: (0, j)),
                  pl.BlockSpec((1, tn), lambda i, j: (0, j))],
        out_specs=pl.BlockSpec((tm, tn), lambda i, j: (i, j)),
        out_shape=jax.ShapeDtypeStruct((n, nc), F32),
        scratch_shapes=[pltpu.VMEM((tm, d), BF16)],
        compiler_params=_cparams(("arbitrary", "arbitrary")),
        name="inproj_ab",
    )(h, sc, sh, w_bf, b.reshape(1, nc))


def _attn_body(sink_ref, q_ref, kc_ref, vc_ref, vp_ref, pc_ref, invf_ref, o_ref, kprev_ref):
    blk = ATT_BLK
    half = A_HEAD_DIM // 2
    grp = A_HEADS // A_KV_HEADS
    i = pl.program_id(1)
    lane = lax.broadcasted_iota(I32, (blk, LANES), 1)
    first_half = (lane % A_HEAD_DIM) < half
    low_head = lane < A_HEAD_DIM
    invf = invf_ref[...]

    def cos_sin(pos_col):
        ang = pos_col.astype(F32) * invf
        s = jnp.sin(ang)
        return jnp.cos(ang), jnp.where(first_half, -s, s)

    def rope(x, cs):
        partner = jnp.where(first_half, pltpu.roll(x, LANES - half, 1), pltpu.roll(x, half, 1))
        return x * cs[0] + partner * cs[1]

    @pl.when(i == 0)
    def _():
        kprev_ref[...] = jnp.zeros_like(kprev_ref)

    cs_c = cos_sin(pc_ref[...])
    k_cur = rope(kc_ref[...], cs_c)
    k2 = jnp.concatenate([kprev_ref[...], k_cur], axis=0)
    kprev_ref[...] = k_cur
    v2 = jnp.concatenate([vp_ref[...], vc_ref[...]], axis=0)
    lane2 = lax.broadcasted_iota(I32, (2 * blk, LANES), 1)
    k2r = pltpu.roll(k2, A_HEAD_DIM, 1)
    v2r = pltpu.roll(v2, A_HEAD_DIM, 1)

    rows = lax.broadcasted_iota(I32, (grp * blk, 2 * blk), 0) % blk
    cols = lax.broadcasted_iota(I32, (grp * blk, 2 * blk), 1)
    valid = (cols > rows) & (cols <= rows + WINDOW) & ((cols >= blk) | (i > 0))

    q_chunks = [rope(q_ref[:, c * LANES:(c + 1) * LANES], cs_c) * (A_HEAD_DIM ** -0.5)
                for c in range(A_HEADS * A_HEAD_DIM // LANES)]

    for g in range(A_KV_HEADS):
        own = (lane2 < A_HEAD_DIM) if g == 0 else (lane2 >= A_HEAD_DIM)
        kg = jnp.where(own, k2, k2r).astype(BF16)
        vg = jnp.where(own, v2, v2r).astype(BF16)
        pieces = []
        sinks = []
        for r in range(grp):
            hd = g * grp + r
            qc = q_chunks[hd // 2]
            keep = low_head if hd % 2 == 0 else jnp.logical_not(low_head)
            pieces.append(jnp.where(keep, qc, 0.0).astype(BF16))
            sinks.append(jnp.full((blk, 1), sink_ref[hd], F32))
        qs = jnp.concatenate(pieces, axis=0)
        sink = jnp.concatenate(sinks, axis=0)
        s = lax.dot_general(qs, kg, (((1,), (1,)), ((), ())), preferred_element_type=F32)
        s = jnp.where(valid, s, NEG)
        m = jnp.maximum(jnp.max(s, axis=-1, keepdims=True), sink)
        p = jnp.exp(s - m)
        den = jnp.sum(p, axis=-1, keepdims=True) + jnp.exp(sink - m)
        o = jnp.dot(p.astype(BF16), vg, preferred_element_type=F32) / den
        for r2 in range(grp // 2):
            c = (g * grp) // 2 + r2
            ev = o[(2 * r2) * blk:(2 * r2 + 1) * blk]
            od = o[(2 * r2 + 1) * blk:(2 * r2 + 2) * blk]
            o_ref[:, c * LANES:(c + 1) * LANES] = jnp.where(low_head, ev, od).astype(o_ref.dtype)


def swa_attention(proj, pos_col, invf, sinks, bsz, seq):
    n = proj.shape[0]
    blk = ATT_BLK
    nb = seq // blk
    aq = A_HEADS * A_HEAD_DIM
    kcol = aq // LANES
    vcol = kcol + 1

    def cur(b, i, s):
        return b * nb + i

    def prev(b, i, s):
        return b * nb + jnp.maximum(i - 1, 0)

    grid_spec = pltpu.PrefetchScalarGridSpec(
        num_scalar_prefetch=1,
        grid=(bsz, nb),
        in_specs=[pl.BlockSpec((blk, aq), lambda b, i, s: (cur(b, i, s), 0)),
                  pl.BlockSpec((blk, LANES), lambda b, i, s: (cur(b, i, s), kcol)),
                  pl.BlockSpec((blk, LANES), lambda b, i, s: (cur(b, i, s), vcol)),
                  pl.BlockSpec((blk, LANES), lambda b, i, s: (prev(b, i, s), vcol)),
                  pl.BlockSpec((blk, 1), lambda b, i, s: (cur(b, i, s), 0)),
                  pl.BlockSpec((1, LANES), lambda b, i, s: (0, 0))],
        out_specs=pl.BlockSpec((blk, aq), lambda b, i, s: (cur(b, i, s), 0)),
        scratch_shapes=[pltpu.VMEM((blk, LANES), F32)],
    )
    return pl.pallas_call(
        _attn_body,
        grid_spec=grid_spec,
        out_shape=jax.ShapeDtypeStruct((n, aq), BF16),
        compiler_params=_cparams(("arbitrary", "arbitrary")),
        name="swa_attention",
    )(sinks, proj, proj, proj, proj, pos_col, invf)


def _hgrn_body(q_ref, f_ref, i_ref, g_ref, lb_ref, gw_ref, o_ref, st_ref):
    L = GLA_CHUNK

    @pl.when(pl.program_id(2) == 0)
    def _():
        st_ref[...] = jnp.zeros_like(st_ref)

    T = q_ref.shape[0]
    nch = T // L
    nt = (((1,), (1,)), ((), ()))
    pos = lax.broadcasted_iota(I32, (T, B_DIM), 0) % L
    r = lax.broadcasted_iota(I32, (T, T), 0)
    c = lax.broadcasted_iota(I32, (T, T), 1)
    same_chunk_causal = (r >= c) & (r // L == c // L)

    for hh in range(q_ref.shape[1] // B_DIM):
        hs = slice(hh * B_DIM, (hh + 1) * B_DIM)
        lb = lb_ref[hh]
        gw = gw_ref[hh]
        f = lb + (1.0 - lb) * jax.nn.sigmoid(f_ref[:, hs])
        kk = 1.0 - f
        qb = q_ref[:, hs]
        q = qb * jax.nn.sigmoid(qb)
        v = i_ref[:, hs]
        b = jnp.log(f)
        sh = 1
        while sh < L:
            b = b + jnp.where(pos >= sh, pltpu.roll(b, sh, 0), 0.0)
            sh *= 2

        def per_chunk(row, b=b):
            return jnp.concatenate(
                [jnp.broadcast_to(b[ci * L + row:ci * L + row + 1], (L, B_DIM)) for ci in range(nch)], axis=0)

        b_mid = per_chunk(L // 2)
        b_last = per_chunk(L - 1)
        qa = (q * jnp.exp(b - b_mid)).astype(BF16)
        ka = (kk * jnp.exp(b_mid - b)).astype(BF16)
        qe = (q * jnp.exp(b)).astype(BF16)
        kd = (kk * jnp.exp(b_last - b)).astype(BF16)
        att = lax.dot_general(qa, ka, nt, preferred_element_type=F32)
        att = jnp.where(same_chunk_causal, att, 0.0)
        o_intra = jnp.dot(att.astype(BF16), v.astype(BF16), preferred_element_type=F32)
        vt = v.T.astype(BF16)
        st = st_ref[hh]
        o_inter = []
        for ci in range(nch):
            sl = slice(ci * L, (ci + 1) * L)
            o_inter.append(lax.dot_general(qe[sl], st.astype(BF16), nt, preferred_element_type=F32))
            u_t = jnp.dot(vt[:, sl], kd[sl], preferred_element_type=F32)
            st = jnp.exp(b[ci * L + L - 1:ci * L + L]) * st + u_t
        st_ref[hh] = st
        o = o_intra + jnp.concatenate(o_inter, axis=0)
        o = o * lax.rsqrt(jnp.mean(o * o, axis=-1, keepdims=True) + RMS_EPS) * gw
        gb = g_ref[:, hs]
        o_ref[:, hs] = (o * (gb * jax.nn.sigmoid(gb))).astype(o_ref.dtype)


def hgrn2(proj, lb, gnorm_w, bsz, seq):
    n = proj.shape[0]
    t = min(HGRN_T, seq)
    nt = seq // t
    hp = HGRN_HEADS_PER_STEP
    width = hp * B_DIM
    base = (A_HEADS + 2 * A_KV_HEADS) * A_HEAD_DIM // width
    h = B_HEADS
    assert h % hp == 0 and (A_HEADS + 2 * A_KV_HEADS) * A_HEAD_DIM % width == 0

    def col(k):
        return lambda b, hd, i: (b * nt + i, base + k * (h // hp) + hd)

    return pl.pallas_call(
        _hgrn_body,
        grid=(bsz, h // hp, nt),
        in_specs=[pl.BlockSpec((t, width), col(0)),
                  pl.BlockSpec((t, width), col(1)),
                  pl.BlockSpec((t, width), col(2)),
                  pl.BlockSpec((t, width), col(3)),
                  pl.BlockSpec((hp, 1, B_DIM), lambda b, hd, i: (hd, 0, 0)),
                  pl.BlockSpec((hp, 1, B_DIM), lambda b, hd, i: (hd, 0, 0))],
        out_specs=pl.BlockSpec((t, width), lambda b, hd, i: (b * nt + i, hd)),
        out_shape=jax.ShapeDtypeStruct((n, h * B_DIM), BF16),
        scratch_shapes=[pltpu.VMEM((hp, B_DIM, B_DIM), F32)],
        compiler_params=_cparams(("arbitrary", "arbitrary", "arbitrary")),
        name="hgrn2",
    )(proj, proj, proj, proj, lb.reshape(h, 1, B_DIM), gnorm_w.reshape(h, 1, B_DIM))


def _outproj_ab_body(oa_ref, ob_ref, w_ref, b_ref, h_ref, gm_ref, lg_ref, lb_ref, o_ref):
    ka = oa_ref.shape[1]
    tm = oa_ref.shape[0]
    sub = min(tm, ROW_SUB)
    for r0 in range(0, tm, sub):
        rs = slice(r0, r0 + sub)
        y = jnp.dot(oa_ref[rs, :], w_ref[:ka, :], preferred_element_type=F32)
        y = y + jnp.dot(ob_ref[rs, :], w_ref[ka:, :], preferred_element_type=F32) + b_ref[...]
        z = DN_ALPHA * h_ref[rs, :] + (1.0 + gm_ref[0]) * y
        o_ref[rs, :] = _layer_norm(z, lg_ref[...], lb_ref[...])


def outproj_ab(oa, ob, w_bf, b, h, gm, ln_g, ln_b, seq):
    n, d = h.shape
    tm = min(512, seq)
    per_b = seq // tm
    ka, kb = oa.shape[1], ob.shape[1]
    row = lambda i: (i, 0)
    const = lambda i: (0, 0)
    return pl.pallas_call(
        _outproj_ab_body,
        grid=(n // tm,),
        in_specs=[pl.BlockSpec((tm, ka), row),
                  pl.BlockSpec((tm, kb), row),
                  _resident((ka + kb, d), const),
                  pl.BlockSpec((1, d), const),
                  pl.BlockSpec((tm, d), row),
                  pl.BlockSpec((1, 1, d), lambda i: (i // per_b, 0, 0)),
                  pl.BlockSpec((1, d), const),
                  pl.BlockSpec((1, d), const)],
        out_specs=pl.BlockSpec((tm, d), row),
        out_shape=jax.ShapeDtypeStruct((n, d), F32),
        compiler_params=_cparams(("arbitrary",)),
        name="outproj_ab",
    )(oa, ob, w_bf, b.reshape(1, d), h, gm, ln_g.reshape(1, d), ln_b.reshape(1, d))


def _router_body(h_ref, sc_ref, sh_ref, rw_ref, rb_ref, meta_ref, cnt_ref, carry_ref, rwh_ref, rwl_ref):
    tm = h_ref.shape[0]

    @pl.when(pl.program_id(0) == 0)
    def _():
        carry_ref[...] = jnp.zeros_like(carry_ref)
        w = rw_ref[...]
        w_hi = w.astype(BF16)
        rwh_ref[...] = w_hi
        rwl_ref[...] = (w - w_hi.astype(F32)).astype(BF16)

    xin = h_ref[...] * (1.0 + sc_ref[0]) + sh_ref[0]
    x_hi = xin.astype(BF16)
    x_lo = (xin - x_hi.astype(F32)).astype(BF16)
    logits = (jnp.dot(x_hi, rwh_ref[...], preferred_element_type=F32)
              + (jnp.dot(x_hi, rwl_ref[...], preferred_element_type=F32)
                 + jnp.dot(x_lo, rwh_ref[...], preferred_element_type=F32))) + rb_ref[...]
    lane = lax.broadcasted_iota(I32, (tm, LANES), 1)
    l = logits
    vals, idxs = [], []
    sel = jnp.zeros((tm, LANES), F32)
    for _ in range(TOP_K):
        m = jnp.max(l, axis=-1, keepdims=True)
        idx = jnp.min(jnp.where(l == m, lane, LANES), axis=-1, keepdims=True)
        hit = lane == idx
        vals.append(m)
        idxs.append(idx)
        sel = jnp.where(hit, 1.0, sel)
        l = jnp.where(hit, -jnp.inf, l)
    es = [jnp.exp(v - vals[0]) for v in vals]
    den = es[0] + es[1] + es[2] + es[3]
    r = lax.broadcasted_iota(I32, (tm, tm), 0)
    c = lax.broadcasted_iota(I32, (tm, tm), 1)
    tril = (r >= c).astype(BF16)
    incl = jnp.dot(tril, sel.astype(BF16), preferred_element_type=F32)
    carry = carry_ref[...]
    excl = incl - sel + carry
    meta = jnp.zeros((tm, LANES), F32)
    for k in range(TOP_K):
        rank = jnp.sum(jnp.where(lane == idxs[k], excl, 0.0), axis=-1, keepdims=True)
        meta = jnp.where(lane == k, idxs[k].astype(F32), meta)
        meta = jnp.where(lane == TOP_K + k, es[k] / den, meta)
        meta = jnp.where(lane == 2 * TOP_K + k, rank, meta)
    meta_ref[...] = meta
    carry = carry + incl[tm - 1:tm, :]
    carry_ref[...] = carry
    cnt_ref[...] = jnp.broadcast_to(carry, cnt_ref.shape)


def moe_router(h, sc, sh, rw_pad, rb_pad, seq):
    n, d = h.shape
    tm = min(512, seq)
    per_b = seq // tm
    return pl.pallas_call(
        _router_body,
        grid=(n // tm,),
        in_specs=[pl.BlockSpec((tm, d), lambda i: (i, 0)),
                  pl.BlockSpec((1, 1, d), lambda i: (i // per_b, 0, 0)),
                  pl.BlockSpec((1, 1, d), lambda i: (i // per_b, 0, 0)),
                  pl.BlockSpec((d, LANES), lambda i: (0, 0)),
                  pl.BlockSpec((1, LANES), lambda i: (0, 0))],
        out_specs=[pl.BlockSpec((tm, LANES), lambda i: (i, 0)),
                   pl.BlockSpec((8, LANES), lambda i: (0, 0))],
        out_shape=[jax.ShapeDtypeStruct((n, LANES), F32),
                   jax.ShapeDtypeStruct((8, LANES), F32)],
        scratch_shapes=[pltpu.VMEM((1, LANES), F32), pltpu.VMEM((d, LANES), BF16), pltpu.VMEM((d, LANES), BF16)],
        compiler_params=_cparams(("arbitrary",)),
        name="moe_router",
    )(h, sc, sh, rw_pad, rb_pad)


def _dispatch_body(pad_ref, dest_ref, h_ref, sc_ref, sh_ref, xs_hbm, buf_ref, zero_ref, sem, zsem):
    tm, d = h_ref.shape
    half = d // 2
    pitch = _packed_pitch(d)
    i = pl.program_id(0)
    last = pl.num_programs(0) - 1
    slot = i % 2
    n_pad = pad_ref.shape[0]

    def wait_slot(s):
        for _ in range(TOP_K):
            pltpu.make_async_copy(buf_ref.at[s], xs_hbm.at[pl.ds(0, tm * pitch)], sem.at[s]).wait()

    @pl.when(i == 0)
    def _():
        zero_ref[...] = jnp.zeros_like(zero_ref)
        buf_ref[...] = jnp.zeros_like(buf_ref)

        def zfill(q2, carry):
            for pr in range(2):
                dst = pad_ref[q2 * 2 + pr] * pitch
                pltpu.make_async_copy(zero_ref, xs_hbm.at[pl.ds(dst, pitch)], zsem.at[0]).start(priority=pr)
            return carry

        lax.fori_loop(0, n_pad // 2, zfill, 0)

    @pl.when(i >= 2)
    def _():
        wait_slot(slot)

    xin = h_ref[...] * (1.0 + sc_ref[0]) + sh_ref[0]
    lo = lax.bitcast_convert_type(xin[:, :half].astype(BF16).astype(F32), U32)
    hi = lax.bitcast_convert_type(xin[:, half:].astype(BF16).astype(F32), U32)
    word = lax.shift_right_logical(lo, jnp.uint32(16)) | (hi & jnp.uint32(0xFFFF0000))
    for s in range(half // LANES):
        buf_ref[slot, pl.ds(s, tm, stride=pitch), :] = word[:, s * LANES:(s + 1) * LANES]

    def issue(g, carry):
        for rr in range(DMA_GROUP):
            r = g * DMA_GROUP + rr
            src = buf_ref.at[slot, pl.ds(r * pitch, pitch)]
            for k in range(TOP_K):
                dst = dest_ref[0, 0, r * TOP_K + k] * pitch
                pltpu.make_async_copy(src, xs_hbm.at[pl.ds(dst, pitch)], sem.at[slot]).start(priority=k % 2)
        return carry

    lax.fori_loop(0, tm // DMA_GROUP, issue, 0)

    @pl.when(i == last)
    def _():
        wait_slot(slot)

        @pl.when(i >= 1)
        def _():
            wait_slot(1 - slot)

        for _ in range(n_pad // tm):
            pltpu.make_async_copy(buf_ref.at[0], xs_hbm.at[pl.ds(0, tm * pitch)], zsem.at[0]).wait()


def moe_dispatch(h, sc, sh, dest, pad_dest, n_rows, seq):
    n, d = h.shape
    tm = min(256, seq)
    per_b = seq // tm
    pitch = _packed_pitch(d)
    assert pad_dest.shape[0] % tm == 0
    grid_spec = pltpu.PrefetchScalarGridSpec(
        num_scalar_prefetch=1,
        grid=(n // tm,),
        in_specs=[pl.BlockSpec((1, 1, tm * TOP_K), lambda i, p: (i, 0, 0), memory_space=pltpu.SMEM),
                  pl.BlockSpec((tm, d), lambda i, p: (i, 0)),
                  pl.BlockSpec((1, 1, d), lambda i, p: (i // per_b, 0, 0)),
                  pl.BlockSpec((1, 1, d), lambda i, p: (i // per_b, 0, 0))],
        out_specs=pl.BlockSpec(memory_space=pl.ANY),
        scratch_shapes=[pltpu.VMEM((2, tm * pitch, LANES), U32), pltpu.VMEM((pitch, LANES), U32),
                        pltpu.SemaphoreType.DMA((2,)), pltpu.SemaphoreType.DMA((1,))],
    )
    return pl.pallas_call(
        _dispatch_body,
        grid_spec=grid_spec,
        out_shape=jax.ShapeDtypeStruct((n_rows * pitch, LANES), U32),
        compiler_params=_cparams(("arbitrary",)),
        name="moe_dispatch",
    )(pad_dest, dest.reshape(n // tm, 1, tm * TOP_K), h, sc, sh)


SCHED_MODE, SCHED_BLOCK, SCHED_EXPERT, SCHED_SLOT, SCHED_DL_EXPERT, SCHED_DL_C1, SCHED_DL_C2, SCHED_DL_SLOT = range(8)
MODE_IDLE, MODE_COMPUTE, MODE_ZERO = 0, 1, 2


def _expert_body(sched_ref, xs_ref, w1c_ref, w2c_ref, b1_ref, b2_ref, y_ref, wb1a_ref, wb1b_ref, wb2a_ref, wb2b_ref):
    j = pl.program_id(0)
    nstep = pl.num_programs(0)
    d = wb1a_ref.shape[0]
    ff = wb2a_ref.shape[0]
    rt = d // LANES
    pitch = _row_pitch(d)
    xpitch = _packed_pitch(d)
    blk = xs_ref.shape[0] // xpitch
    cache = ((wb1a_ref, wb2a_ref), (wb1b_ref, wb2b_ref))

    def sched(field):
        return sched_ref[field * nstep + j]

    def cast_chunks(wb1_ref, wb2_ref):
        r1 = pl.multiple_of(sched(SCHED_DL_C1) * W_CHUNK, W_CHUNK)
        r2 = pl.multiple_of(sched(SCHED_DL_C2) * W_CHUNK, W_CHUNK)
        wb1_ref[pl.ds(r1, W_CHUNK), :] = w1c_ref[0].astype(BF16)
        wb2_ref[pl.ds(r2, W_CHUNK), :] = w2c_ref[0].astype(BF16)

    def compute(wb1_ref, wb2_ref):
        words = [xs_ref[pl.ds(s, blk, stride=xpitch), :] for s in range(rt // 2)]
        lo = [lax.bitcast_convert_type(lax.shift_left(w, jnp.uint32(16)), F32).astype(BF16) for w in words]
        hi = [lax.bitcast_convert_type(w & jnp.uint32(0xFFFF0000), F32).astype(BF16) for w in words]
        x = jnp.concatenate(lo + hi, axis=1)
        hid = jnp.dot(x, wb1_ref[...], preferred_element_type=F32) + b1_ref[0]
        gate = jnp.minimum(hid[:, :ff], SWIGLU_LIMIT)
        lin = jnp.clip(hid[:, ff:], -SWIGLU_LIMIT, SWIGLU_LIMIT)
        act = gate * jax.nn.sigmoid(SWIGLU_ALPHA * gate) * (lin + 1.0)
        y = jnp.dot(act.astype(BF16), wb2_ref[...], preferred_element_type=F32) + b2_ref[0]
        for s in range(rt):
            y_ref[pl.ds(s, blk, stride=pitch), :] = y[:, s * LANES:(s + 1) * LANES]
        for s in range(rt, pitch):
            y_ref[pl.ds(s, blk, stride=pitch), :] = jnp.zeros((blk, LANES), F32)

    mode = sched(SCHED_MODE)
    slot = sched(SCHED_SLOT)
    dl_slot = sched(SCHED_DL_SLOT)
    for par in range(2):
        @pl.when((mode == MODE_COMPUTE) & (slot == par))
        def _(par=par):
            cast_chunks(*cache[1 - par])
            compute(*cache[par])

        @pl.when((mode != MODE_COMPUTE) & (dl_slot == par))
        def _(par=par):
            cast_chunks(*cache[par])

    @pl.when(mode == MODE_ZERO)
    def _():
        y_ref[...] = jnp.zeros_like(y_ref)


def expert_schedule(padded, n_used, nblk, layer, ne, need, need2):
    nb = padded // MOE_BLK
    seg_len = jnp.concatenate([jnp.full((1,), need, I32), jnp.maximum(nb[:-1], need), nb[-1:]])
    seg_end = jnp.cumsum(seg_len)
    seg_start = seg_end - seg_len
    nstep = nblk + (ne + 1) * need
    j = jnp.arange(nstep, dtype=I32)
    seg = jnp.sum(seg_end[None, :] <= j[:, None], axis=-1).astype(I32)
    in_seg = seg <= ne
    segc = jnp.minimum(seg, ne)
    loc = j - seg_start[segc]
    expert = jnp.clip(segc - 1, 0, ne - 1)
    compute = in_seg & (segc >= 1) & (loc < nb[expert])
    done = jnp.cumsum(compute.astype(I32)) - compute.astype(I32)
    tail = j - seg_end[-1]
    ntail = nblk - n_used
    zero = jnp.logical_not(in_seg) & (tail < ntail)
    parked = jnp.where(jnp.logical_not(in_seg) & (ntail > 0), nblk - 1, jnp.maximum(done - 1, 0))
    block = jnp.where(compute, done, jnp.where(zero, n_used + tail, parked))
    mode = jnp.where(compute, MODE_COMPUTE, jnp.where(zero, MODE_ZERO, MODE_IDLE))
    deliver = in_seg & (segc <= ne - 1) & (loc < need)
    last = lax.cummax(jnp.where(deliver, j, -1), axis=0)
    dl_expert = segc[last]
    dl_c1 = loc[last]
    dl_c2 = jnp.minimum(loc[last], need2 - 1)
    table = jnp.stack([mode, block, expert + layer * ne, expert % 2,
                       dl_expert + layer * ne, dl_c1, dl_c2, dl_expert % 2])
    return table.astype(I32).reshape(-1), nstep


def moe_experts(xs, sched, nstep, w1, b1, w2, b2):
    ne, d, ff2 = w1.shape
    pitch = _row_pitch(d)
    xpitch = _packed_pitch(d)
    n_rows = xs.shape[0] // xpitch
    ff = ff2 // 2

    def fld(s, j, field):
        return s[field * nstep + j]

    grid_spec = pltpu.PrefetchScalarGridSpec(
        num_scalar_prefetch=1,
        grid=(nstep,),
        in_specs=[pl.BlockSpec((MOE_BLK * xpitch, LANES), lambda j, s: (fld(s, j, SCHED_BLOCK), 0)),
                  pl.BlockSpec((1, W_CHUNK, ff2),
                               lambda j, s: (fld(s, j, SCHED_DL_EXPERT), fld(s, j, SCHED_DL_C1), 0)),
                  pl.BlockSpec((1, W_CHUNK, d),
                               lambda j, s: (fld(s, j, SCHED_DL_EXPERT), fld(s, j, SCHED_DL_C2), 0)),
                  pl.BlockSpec((1, 1, ff2), lambda j, s: (fld(s, j, SCHED_EXPERT), 0, 0)),
                  pl.BlockSpec((1, 1, d), lambda j, s: (fld(s, j, SCHED_EXPERT), 0, 0))],
        out_specs=pl.BlockSpec((MOE_BLK * pitch, LANES), lambda j, s: (fld(s, j, SCHED_BLOCK), 0)),
        scratch_shapes=[pltpu.VMEM((d, ff2), BF16), pltpu.VMEM((d, ff2), BF16),
                        pltpu.VMEM((ff, d), BF16), pltpu.VMEM((ff, d), BF16)],
    )
    return pl.pallas_call(
        _expert_body,
        grid_spec=grid_spec,
        out_shape=jax.ShapeDtypeStruct((n_rows * pitch, LANES), F32),
        compiler_params=_cparams(("arbitrary",)),
        name="moe_experts",
    )(sched, xs, w1, w2, b1.reshape(ne, 1, ff2), b2.reshape(ne, 1, d))


def _combine_body(dc_ref, dn_ref, meta_ref, h_ref, gf_ref, lg_ref, lb_ref, y_hbm, o_ref, buf_ref, sem):
    tm, d = h_ref.shape
    rt = d // LANES
    pitch = _row_pitch(d)
    grp = COMBINE_GROUP
    i = pl.program_id(0)
    n = pl.num_programs(0)
    slot = i % 2

    def issue_rows(dref, s, r0):
        for rr in range(grp):
            r = r0 + rr
            for k in range(TOP_K):
                src = dref[0, 0, r * TOP_K + k] * pitch
                pltpu.make_async_copy(y_hbm.at[pl.ds(src, rt)], buf_ref.at[s, k, pl.ds(r * pitch, rt)],
                                      sem.at[s]).start(priority=k % 2)

    @pl.when(i == 0)
    def _():
        def body(g, carry):
            issue_rows(dc_ref, 0, g * grp)
            return carry
        lax.fori_loop(0, tm // grp, body, 0)

    for k in range(TOP_K):
        pltpu.make_async_copy(y_hbm.at[pl.ds(0, tm * rt)], buf_ref.at[slot, k, pl.ds(0, tm * rt)],
                              sem.at[slot]).wait()

    def group(g, carry, prefetch):
        r0 = pl.multiple_of(g * grp, grp)
        if prefetch:
            issue_rows(dn_ref, 1 - slot, r0)
        meta = meta_ref[pl.ds(r0, grp), :]
        gates = [jnp.broadcast_to(meta[:, TOP_K + k:TOP_K + k + 1], (grp, LANES)) for k in range(TOP_K)]
        for s in range(rt):
            a = gates[0] * buf_ref[slot, 0, pl.ds(r0 * pitch + s, grp, stride=pitch), :]
            for k in range(1, TOP_K):
                a = a + gates[k] * buf_ref[slot, k, pl.ds(r0 * pitch + s, grp, stride=pitch), :]
            cs = slice(s * LANES, (s + 1) * LANES)
            o_ref[pl.ds(r0, grp), cs] = DN_ALPHA * h_ref[pl.ds(r0, grp), cs] + (1.0 + gf_ref[0, :, cs]) * a
        return carry

    @pl.when(i + 1 < n)
    def _():
        lax.fori_loop(0, tm // grp, functools.partial(group, prefetch=True), 0)

    @pl.when(i + 1 == n)
    def _():
        lax.fori_loop(0, tm // grp, functools.partial(group, prefetch=False), 0)

    o_ref[...] = _layer_norm(o_ref[...], lg_ref[...], lb_ref[...])


def moe_combine(y, dest, meta, h, gf, ln_g, ln_b, seq):
    n, d = h.shape
    tm = min(256, seq)
    per_b = seq // tm
    nblk = n // tm
    dest3 = dest.reshape(nblk, 1, tm * TOP_K)
    row = lambda i: (i, 0)
    const = lambda i: (0, 0)
    return pl.pallas_call(
        _combine_body,
        grid=(nblk,),
        in_specs=[pl.BlockSpec((1, 1, tm * TOP_K), lambda i: (i, 0, 0), memory_space=pltpu.SMEM),
                  pl.BlockSpec((1, 1, tm * TOP_K), lambda i: (jnp.minimum(i + 1, nblk - 1), 0, 0),
                               memory_space=pltpu.SMEM),
                  pl.BlockSpec((tm, LANES), row),
                  pl.BlockSpec((tm, d), row),
                  pl.BlockSpec((1, 1, d), lambda i: (i // per_b, 0, 0)),
                  pl.BlockSpec((1, d), const),
                  pl.BlockSpec((1, d), const),
                  pl.BlockSpec(memory_space=pl.ANY)],
        out_specs=pl.BlockSpec((tm, d), row),
        out_shape=jax.ShapeDtypeStruct((n, d), F32),
        scratch_shapes=[pltpu.VMEM((2, TOP_K, tm * _row_pitch(d), LANES), F32), pltpu.SemaphoreType.DMA((2,))],
        compiler_params=_cparams(("arbitrary",)),
        name="moe_combine",
    )(dest3, dest3, meta, h, gf, ln_g.reshape(1, d), ln_b.reshape(1, d), y)


def moe_layer(h, sc, sh, gf, ln_g, ln_b, router_w, router_b, layer, w1, b1, w2, b2, seq):
    n, d = h.shape
    ne = router_w.shape[1]
    rw_pad = jnp.pad(router_w, ((0, 0), (0, LANES - ne)))
    rb_pad = jnp.pad(router_b, (0, LANES - ne), constant_values=NEG).reshape(1, LANES)
    meta, cnt = moe_router(h, sc, sh, rw_pad, rb_pad, seq)
    idx4 = meta[:, 0:TOP_K].astype(I32)
    rank4 = meta[:, 2 * TOP_K:3 * TOP_K].astype(I32)
    counts = cnt[0, :ne].astype(I32)
    padded = (counts + MOE_BLK - 1) // MOE_BLK * MOE_BLK
    end = jnp.cumsum(padded)
    start = end - padded
    onehot = idx4[:, :, None] == jnp.arange(ne, dtype=I32)[None, None, :]
    dest = jnp.sum(jnp.where(onehot, start[None, None, :], 0), axis=-1) + rank4
    n_rows = -(-(n * TOP_K + ne * (MOE_BLK - 1)) // MOE_BLK) * MOE_BLK
    nblk = n_rows // MOE_BLK
    blk0 = jnp.arange(nblk, dtype=I32) * MOE_BLK
    n_used = (end[-1] // MOE_BLK).astype(I32)
    n_pad = n_rows - n * TOP_K
    seg_cnt = jnp.concatenate([padded - counts, (n_rows - end[-1])[None]])
    seg_base = jnp.concatenate([start + counts, end[-1:]])
    seg_end = jnp.cumsum(seg_cnt)
    q = jnp.arange(n_pad, dtype=I32)
    seg = jnp.sum(seg_end[None, :] <= q[:, None], axis=-1)
    pad_dest = (seg_base[seg] + q - (seg_end - seg_cnt)[seg]).astype(I32)
    xs = moe_dispatch(h, sc, sh, dest.reshape(-1), pad_dest, n_rows, seq)
    sched, nstep = expert_schedule(padded, n_used, nblk, layer, ne, w1.shape[1] // W_CHUNK, w2.shape[1] // W_CHUNK)
    y = moe_experts(xs, sched, nstep, w1, b1, w2, b2)
    return moe_combine(y, dest.reshape(-1), meta, h, gf, ln_g, ln_b, seq)


def _inproj_c_body(x_ref, sc_ref, sh_ref, w_ref, o_ref, acc_ref):
    tm = x_ref.shape[0]
    ncol = w_ref.shape[1] // LANES
    sub = min(tm, ROW_SUB)
    for r0 in range(0, tm, sub):
        rs = slice(r0, r0 + sub)
        xb = (x_ref[rs, :] * (1.0 + sc_ref[0]) + sh_ref[0]).astype(BF16)
        acc = jnp.dot(xb, w_ref[...], preferred_element_type=F32)
        for c in range(ncol):
            acc_ref[c, rs, :] = acc[:, c * LANES:(c + 1) * LANES]
    for t in range(SSM_T):
        for c in range(ncol):
            o_ref[t, :, c * LANES:(c + 1) * LANES] = acc_ref[c, pl.ds(t, tm // SSM_T, stride=SSM_T), :]


def inproj_c(h, sc, sh, w_bf, seq):
    n, d = h.shape
    width = w_bf.shape[1]
    tm = min(1024, seq)
    per_b = seq // tm
    return pl.pallas_call(
        _inproj_c_body,
        grid=(n // tm,),
        in_specs=[pl.BlockSpec((tm, d), lambda i: (i, 0)),
                  pl.BlockSpec((1, 1, d), lambda i: (i // per_b, 0, 0)),
                  pl.BlockSpec((1, 1, d), lambda i: (i // per_b, 0, 0)),
                  _resident((d, width), lambda i: (0, 0))],
        out_specs=pl.BlockSpec((SSM_T, tm // SSM_T, width), lambda i: (0, i, 0)),
        out_shape=jax.ShapeDtypeStruct((SSM_T, n // SSM_T, width), F32),
        scratch_shapes=[pltpu.VMEM((width // LANES, tm, LANES), F32)],
        compiler_params=_cparams(("arbitrary",)),
        name="inproj_c",
    )(h, sc, sh, w_bf)


def _s5_body(u_ref, wt_ref, wa_ref, wc_ref, lam_ref, d_ref, o_ref, st_ref, v_ref, xs_ref):
    nb, cb = u_ref.shape[1], u_ref.shape[2]
    ns = st_ref.shape[1] // 2
    kdim = SSM_T * LANES

    @pl.when(pl.program_id(1) == 0)
    def _():
        st_ref[...] = jnp.zeros_like(st_ref)

    xcat = jnp.concatenate(
        [jnp.concatenate([u_ref[t, b].astype(BF16) for t in range(SSM_T)], axis=1) for b in range(nb)], axis=0)
    v = jnp.dot(xcat, wa_ref[0], preferred_element_type=F32)
    nslab = 2 * ns // LANES
    for c in range(nslab):
        v_ref[c] = v[:, c * LANES:(c + 1) * LANES]
    lam = lam_ref[0]
    a_re, a_im = lam[:, :ns], lam[:, ns:]

    def step(r, st):
        for c in range(nslab):
            xs_ref[c, pl.ds(r, nb, stride=cb), :] = st[:, c * LANES:(c + 1) * LANES]
        v = jnp.concatenate([v_ref[c, pl.ds(r, nb, stride=cb), :] for c in range(nslab)], axis=1)
        s_re, s_im = st[:, :ns], st[:, ns:]
        n_re = a_re * s_re - a_im * s_im + v[:, :ns]
        n_im = a_re * s_im + a_im * s_re + v[:, ns:]
        return jnp.concatenate([n_re, n_im], axis=1)

    st_ref[...] = lax.fori_loop(0, cb, step, st_ref[...])
    xs_bf = jnp.concatenate([xs_ref[c].astype(BF16) for c in range(nslab)], axis=1)
    dvec = d_ref[0]
    for q in range(S5_COL_BLOCKS):
        c0, c1 = q * kdim // S5_COL_BLOCKS, (q + 1) * kdim // S5_COL_BLOCKS
        y = jnp.dot(xcat[:, :c1], wt_ref[0, :c1, c0:c1], preferred_element_type=F32)
        y = y + jnp.dot(xs_bf, wc_ref[0, :, c0:c1], preferred_element_type=F32)
        for tt in range((c1 - c0) // LANES):
            t = c0 // LANES + tt
            for b in range(nb):
                yt = y[b * cb:(b + 1) * cb, tt * LANES:(tt + 1) * LANES] + dvec * u_ref[t, b]
                gel = 0.5 * yt * (1.0 + jnp.tanh(0.7978845608028654 * (yt + 0.044715 * yt * yt * yt)))
                o_ref[t, b] = gel.astype(o_ref.dtype)


def s5_mixer(u2, wt, wa, wc, lam_t, dpack, bsz, seq):
    _, nchunk, width = u2.shape
    npack = width // LANES
    per_b = seq // SSM_T
    cb = min(256, per_b)
    nblk = per_b // cb
    ns2 = wa.shape[2]
    kdim = SSM_T * LANES
    pk = lambda j, i: (j, 0, 0)
    u4 = u2.reshape(SSM_T, bsz, per_b, width)
    y4 = pl.pallas_call(
        _s5_body,
        grid=(npack, nblk),
        in_specs=[pl.BlockSpec((SSM_T, bsz, cb, LANES), lambda j, i: (0, 0, i, j)),
                  _resident((1, kdim, kdim), pk),
                  _resident((1, kdim, ns2), pk),
                  _resident((1, ns2, kdim), pk),
                  pl.BlockSpec((1, 1, ns2), pk),
                  pl.BlockSpec((1, 1, LANES), pk)],
        out_specs=pl.BlockSpec((SSM_T, bsz, cb, LANES), lambda j, i: (0, 0, i, j)),
        out_shape=jax.ShapeDtypeStruct((SSM_T, bsz, per_b, width), F32),
        scratch_shapes=[pltpu.VMEM((bsz, ns2), F32), pltpu.VMEM((ns2 // LANES, bsz * cb, LANES), F32),
                        pltpu.VMEM((ns2 // LANES, bsz * cb, LANES), F32)],
        compiler_params=_cparams(("arbitrary", "arbitrary")),
        name="s5_mixer",
    )(u4, wt, wa, wc, lam_t, dpack)
    return y4.reshape(SSM_T, nchunk, width)


def s5_operators(a_re, a_im, log_dt, b_re, b_im, c_re, c_im, dskip):
    g, p = a_re.shape
    cg = b_re.shape[2]
    npack = g // SSM_PACK
    ns2 = SSM_PACK * 2 * p
    t = SSM_T
    lam = lax.complex(a_re, a_im)
    dt = jnp.exp(log_dt)[:, None]
    lam_bar = jnp.exp(lam * dt)
    b_bar = ((lam_bar - 1.0) / lam)[..., None] * lax.complex(b_re, b_im)
    c_mat = lax.complex(c_re, c_im)
    kpow = jnp.arange(t + 1, dtype=F32)[:, None, None]
    pows = jnp.exp((lam * dt)[None] * kpow)
    lane_c = jnp.arange(LANES) % cg
    lane_g = jnp.arange(LANES) // cg
    tile = (lane_c[None, :] == jnp.arange(cg)[:, None]).astype(F32)
    diag = lane_g[:, None] == lane_g[None, :]
    kern = jnp.real(jnp.einsum('gcp,kgp,gpd->kgdc', c_mat, pows[:t], b_bar, precision=HIGHEST))
    kfull = jnp.where(diag, jnp.dot(kern.reshape(t, npack, LANES, cg), tile, precision=HIGHEST), 0.0)
    kfull = kfull.astype(BF16)
    kcat = jnp.concatenate([kfull[tau] for tau in range(t)], axis=-1)
    wt = jnp.stack([jnp.pad(kcat[:, :, :(t - s) * LANES], ((0, 0), (0, 0), (s * LANES, 0)))
                    for s in range(t)], axis=1)
    wt = wt.reshape(npack, t * LANES, t * LANES)
    ns = SSM_PACK * p
    col_g = jnp.arange(ns) // p
    tile_p = (jnp.arange(ns)[None, :] % p == jnp.arange(p)[:, None]).astype(F32)
    wa_c = pows[t - 1 - jnp.arange(t)][:, :, :, None] * b_bar[None]
    wa_c = wa_c.reshape(t, npack, SSM_PACK, p, cg).transpose(1, 0, 2, 4, 3)
    wa_c = wa_c.reshape(npack, t * LANES, p)
    own_a = jnp.tile(lane_g, t)[:, None] == col_g[None, :]
    wa = jnp.concatenate([jnp.where(own_a, jnp.dot(part, tile_p, precision=HIGHEST), 0.0).astype(BF16)
                          for part in (jnp.real(wa_c), jnp.imag(wa_c))], axis=-1)
    wc_c = c_mat[None] * pows[1:t + 1][:, :, None, :]
    wc_c = wc_c.transpose(0, 1, 3, 2).reshape(t, npack, ns, cg)
    cc = jnp.concatenate([jnp.real(wc_c), -jnp.imag(wc_c)], axis=2)
    own_c = (jnp.arange(ns2) % ns // p)[:, None] == lane_g[None, :]
    wcf = jnp.where(own_c, jnp.dot(cc, tile, precision=HIGHEST), 0.0).astype(BF16)
    wc = jnp.concatenate([wcf[tt] for tt in range(t)], axis=-1)
    lam_t = pows[t].reshape(npack, 1, ns)
    lam_t = jnp.concatenate([jnp.real(lam_t), jnp.imag(lam_t)], axis=-1)
    dpack = dskip.reshape(npack, 1, LANES)
    return wt, wa, wc, lam_t, dpack


def _glu_out_body(y_ref, wg_ref, bg_ref, wo_ref, h_ref, gm_ref, lg_ref, lb_ref, o_ref, ytm_ref):
    cb = y_ref.shape[1]
    ncol = y_ref.shape[2] // LANES
    for t in range(SSM_T):
        for c in range(ncol):
            ytm_ref[c, pl.ds(t, cb, stride=SSM_T), :] = y_ref[t, :, c * LANES:(c + 1) * LANES]
    tm = h_ref.shape[0]
    sub = min(tm, ROW_SUB)
    for r0 in range(0, tm, sub):
        rs = slice(r0, r0 + sub)
        y = jnp.concatenate([ytm_ref[c, rs, :] for c in range(ncol)], axis=1)
        gate = jnp.dot(y.astype(BF16), wg_ref[...], preferred_element_type=F32) + bg_ref[...]
        z = (y * jax.nn.sigmoid(gate)).astype(BF16)
        out = jnp.dot(z, wo_ref[...], preferred_element_type=F32)
        zz = DN_ALPHA * h_ref[rs, :] + (1.0 + gm_ref[0]) * out
        o_ref[rs, :] = _layer_norm(zz, lg_ref[...], lb_ref[...])


def glu_out(y2, wg_bf, bg, wo_bf, h, gm, ln_g, ln_b, seq):
    n, d = h.shape
    width = y2.shape[2]
    tm = min(512, seq)
    cb = tm // SSM_T
    per_b = seq // tm
    row = lambda i: (i, 0)
    const = lambda i: (0, 0)
    return pl.pallas_call(
        _glu_out_body,
        grid=(n // tm,),
        in_specs=[pl.BlockSpec((SSM_T, cb, width), lambda i: (0, i, 0)),
                  _resident((width, width), const),
                  pl.BlockSpec((1, width), const),
                  _resident((width, d), const),
                  pl.BlockSpec((tm, d), row),
                  pl.BlockSpec((1, 1, d), lambda i: (i // per_b, 0, 0)),
                  pl.BlockSpec((1, d), const),
                  pl.BlockSpec((1, d), const)],
        out_specs=pl.BlockSpec((tm, d), row),
        out_shape=jax.ShapeDtypeStruct((n, d), F32),
        scratch_shapes=[pltpu.VMEM((width // LANES, tm, LANES), F32)],
        compiler_params=_cparams(("arbitrary",)),
        name="glu_out",
    )(y2, wg_bf, bg.reshape(1, width), wo_bf, h, gm, ln_g.reshape(1, d), ln_b.reshape(1, d))


def kernel(x, c, positions, ada_w, ada_b, ln_g, ln_b, ab_in_w, ab_in_b, ab_sinks, ab_gnorm_w, ab_out_w, ab_out_b, hgrn_lb_logits, c_in_w, c_A_re, c_A_im, c_log_dt, c_B_re, c_B_im, c_C_re, c_C_im, c_D, c_glu_w, c_glu_b, c_out_w, router_w, router_b, exp_w1, exp_b1, exp_w2, exp_b2):
    bsz, seq, d = x.shape
    n = bsz * seq
    depth = ada_w.shape[0]
    mod = ada_mod(c, ada_w, ada_b)
    lb_all = jnp.cumsum(jax.nn.softmax(hgrn_lb_logits.astype(F32), axis=0), axis=0)
    half = A_HEAD_DIM // 2
    inv_freq = ROPE_THETA ** (-jnp.arange(half, dtype=F32) / half)
    invf = jnp.tile(inv_freq, LANES // half).reshape(1, LANES)
    pos_col = positions.reshape(n, 1)
    ne_all = depth * exp_w1.shape[1]
    w1_all = exp_w1.reshape(ne_all, d, exp_w1.shape[3])
    w2_all = exp_w2.reshape(ne_all, exp_w2.shape[2], d)
    b1_all = exp_b1.reshape(ne_all, exp_b1.shape[2])
    b2_all = exp_b2.reshape(ne_all, d)
    h = x.reshape(n, d)
    for layer in range(depth):
        m6 = mod[layer].reshape(bsz, 6, 1, d)
        sh_m, sc_m, g_m, sh_f, sc_f, g_f = [m6[:, k] for k in range(6)]
        i = layer // 2
        if layer % 2 == 0:
            proj = inproj_ab(h, sc_m, sh_m, ab_in_w[i].astype(BF16), ab_in_b[i], seq)
            oa = swa_attention(proj, pos_col, invf, ab_sinks[i], bsz, seq)
            ob = hgrn2(proj, lb_all[layer], ab_gnorm_w[i], bsz, seq)
            h = outproj_ab(oa, ob, ab_out_w[i].astype(BF16), ab_out_b[i], h, g_m,
                           ln_g[layer, 0], ln_b[layer, 0], seq)
        else:
            wt, wa, wc, lam_t, dpack = s5_operators(c_A_re[i], c_A_im[i], c_log_dt[i], c_B_re[i], c_B_im[i],
                                                    c_C_re[i], c_C_im[i], c_D[i])
            u2 = inproj_c(h, sc_m, sh_m, c_in_w[i].astype(BF16), seq)
            y2 = s5_mixer(u2, wt, wa, wc, lam_t, dpack, bsz, seq)
            h = glu_out(y2, c_glu_w[i].astype(BF16), c_glu_b[i], c_out_w[i].astype(BF16), h, g_m,
                        ln_g[layer, 0], ln_b[layer, 0], seq)
        h = moe_layer(h, sc_f, sh_f, g_f, ln_g[layer, 1], ln_b[layer, 1], router_w[layer], router_b[layer],
                      layer, w1_all, b1_all, w2_all, b2_all, seq)
    return h.reshape(bsz, seq, d)
```

```python
import functools

import jax
import jax.numpy as jnp
from jax import lax
from jax.experimental import pallas as pl
from jax.experimental.pallas import tpu as pltpu

F32 = jnp.float32
BF16 = jnp.bfloat16
I32 = jnp.int32
U32 = jnp.uint32
HIGHEST = lax.Precision.HIGHEST

DEPTH = 2
A_HEADS = 16
A_KV_HEADS = 2
A_HEAD_DIM = 64
WINDOW = 128
ROPE_THETA = 10000.0
B_HEADS = 8
B_DIM = 128
GLA_CHUNK = 64
C_GROUP = 16
C_STATE = 64
N_EXPERTS = 32
TOP_K = 4
SWIGLU_LIMIT = 7.0
SWIGLU_ALPHA = 1.702
DN_ALPHA = (2 * DEPTH) ** 0.25
LN_EPS = 1e-5
RMS_EPS = 1e-6

LANES = 128
VMEM_LIMIT = 56 * 1024 * 1024

ATT_BLK = 128
HGRN_T = 512
HGRN_HEADS_PER_STEP = 2
MOE_BLK = 256
SSM_T = 16
SSM_PACK = LANES // C_GROUP
S5_COL_BLOCKS = 4
DMA_GROUP = 8
COMBINE_GROUP = 32
ROW_SUB = 256
W_CHUNK = 256
EXPERT_SUB_BLOCKS = 1
NEG = -1e30


def _packed_pitch(d):
    return d // (2 * LANES) + 1


def _row_pitch(d):
    return d // LANES + 1


def _cparams(sem):
    return pltpu.CompilerParams(dimension_semantics=sem, vmem_limit_bytes=VMEM_LIMIT)


def _resident(block_shape, index_map):
    return pl.BlockSpec(block_shape, index_map, pipeline_mode=pl.Buffered(1))


def _layer_norm(z, g, b):
    mu = jnp.mean(z, axis=-1, keepdims=True)
    zc = z - mu
    var = jnp.mean(zc * zc, axis=-1, keepdims=True)
    return zc * lax.rsqrt(var + LN_EPS) * g + b


def _ada_body(ct_ref, w_ref, b_ref, o_ref):
    ct = ct_ref[...]
    cond = ct * jax.nn.sigmoid(ct)
    w = w_ref[0]
    rows = [jnp.sum(w * cond[:, b:b + 1], axis=0, keepdims=True) for b in range(ct.shape[1])]
    o_ref[0] = jnp.concatenate(rows, axis=0) + b_ref[0]


def ada_mod(c, ada_w, ada_b):
    depth, d, n6 = ada_w.shape
    bsz = c.shape[0]
    tn = 1536
    return pl.pallas_call(
        _ada_body,
        grid=(depth, n6 // tn),
        in_specs=[pl.BlockSpec((d, bsz), lambda l, j: (0, 0)),
                  pl.BlockSpec((1, d, tn), lambda l, j: (l, 0, j)),
                  pl.BlockSpec((1, 1, tn), lambda l, j: (l, 0, j))],
        out_specs=pl.BlockSpec((1, bsz, tn), lambda l, j: (l, 0, j)),
        out_shape=jax.ShapeDtypeStruct((depth, bsz, n6), F32),
        compiler_params=_cparams(("arbitrary", "arbitrary")),
        name="ada_mod",
    )(c.T, ada_w, ada_b.reshape(depth, 1, n6))


def _inproj_ab_body(x_ref, sc_ref, sh_ref, w_ref, b_ref, o_ref, xb_ref):
    @pl.when(pl.program_id(1) == 0)
    def _():
        xb_ref[...] = (x_ref[...] * (1.0 + sc_ref[0]) + sh_ref[0]).astype(BF16)

    tm = x_ref.shape[0]
    sub = min(tm, ROW_SUB)
    for r0 in range(0, tm, sub):
        rs = slice(r0, r0 + sub)
        o_ref[rs, :] = jnp.dot(xb_ref[rs, :], w_ref[...], preferred_element_type=F32) + b_ref[...]


def inproj_ab(h, sc, sh, w_bf, b, seq):
    n, d = h.shape
    nc = w_bf.shape[1]
    tm = min(1024, seq)
    tn = 1792
    per_b = seq // tm
    return pl.pallas_call(
        _inproj_ab_body,
        grid=(n // tm, nc // tn),
        in_specs=[pl.BlockSpec((tm, d), lambda i, j: (i, 0)),
                  pl.BlockSpec((1, 1, d), lambda i, j: (i // per_b, 0, 0)),
                  pl.BlockSpec((1, 1, d), lambda i, j: (i // per_b, 0, 0)),
                  pl.BlockSpec((d, tn), lambda i, j: (0, j)),
                  pl.BlockSpec((1, tn), lambda i, j: (0, j))],
        out_specs=pl.BlockSpec((tm, tn), lambda i, j: (i, j)),
        out_shape=jax.ShapeDtypeStruct((n, nc), F32),
        scratch_shapes=[pltpu.VMEM((tm, d), BF16)],
        compiler_params=_cparams(("arbitrary", "arbitrary")),
        name="inproj_ab",
    )(h, sc, sh, w_bf, b.reshape(1, nc))


def _attn_body(sink_ref, q_ref, kc_ref, vc_ref, vp_ref, pc_ref, invf_ref, o_ref, kprev_ref):
    blk = ATT_BLK
    half = A_HEAD_DIM // 2
    grp = A_HEADS // A_KV_HEADS
    i = pl.program_id(1)
    lane = lax.broadcasted_iota(I32, (blk, LANES), 1)
    first_half = (lane % A_HEAD_DIM) < half
    low_head = lane < A_HEAD_DIM
    invf = invf_ref[...]

    def cos_sin(pos_col):
        ang = pos_col.astype(F32) * invf
        s = jnp.sin(ang)
        return jnp.cos(ang), jnp.where(first_half, -s, s)

    def rope(x, cs):
        partner = jnp.where(first_half, pltpu.roll(x, LANES - half, 1), pltpu.roll(x, half, 1))
        return x * cs[0] + partner * cs[1]

    @pl.when(i == 0)
    def _():
        kprev_ref[...] = jnp.zeros_like(kprev_ref)

    cs_c = cos_sin(pc_ref[...])
    k_cur = rope(kc_ref[...], cs_c)
    k2 = jnp.concatenate([kprev_ref[...], k_cur], axis=0)
    kprev_ref[...] = k_cur
    v2 = jnp.concatenate([vp_ref[...], vc_ref[...]], axis=0)
    lane2 = lax.broadcasted_iota(I32, (2 * blk, LANES), 1)
    k2r = pltpu.roll(k2, A_HEAD_DIM, 1)
    v2r = pltpu.roll(v2, A_HEAD_DIM, 1)

    rows = lax.broadcasted_iota(I32, (grp * blk, 2 * blk), 0) % blk
    cols = lax.broadcasted_iota(I32, (grp * blk, 2 * blk), 1)
    valid = (cols > rows) & (cols <= rows + WINDOW) & ((cols >= blk) | (i > 0))

    q_chunks = [rope(q_ref[:, c * LANES:(c + 1) * LANES], cs_c) * (A_HEAD_DIM ** -0.5)
                for c in range(A_HEADS * A_HEAD_DIM // LANES)]

    for g in range(A_KV_HEADS):
        own = (lane2 < A_HEAD_DIM) if g == 0 else (lane2 >= A_HEAD_DIM)
        kg = jnp.where(own, k2, k2r).astype(BF16)
        vg = jnp.where(own, v2, v2r).astype(BF16)
        pieces = []
        sinks = []
        for r in range(grp):
            hd = g * grp + r
            qc = q_chunks[hd // 2]
            keep = low_head if hd % 2 == 0 else jnp.logical_not(low_head)
            pieces.append(jnp.where(keep, qc, 0.0).astype(BF16))
            sinks.append(jnp.full((blk, 1), sink_ref[hd], F32))
        qs = jnp.concatenate(pieces, axis=0)
        sink = jnp.concatenate(sinks, axis=0)
        s = lax.dot_general(qs, kg, (((1,), (1,)), ((), ())), preferred_element_type=F32)
        s = jnp.where(valid, s, NEG)
        m = jnp.maximum(jnp.max(s, axis=-1, keepdims=True), sink)
        p = jnp.exp(s - m)
        den = jnp.sum(p, axis=-1, keepdims=True) + jnp.exp(sink - m)
        o = jnp.dot(p.astype(BF16), vg, preferred_element_type=F32) / den
        for r2 in range(grp // 2):
            c = (g * grp) // 2 + r2
            ev = o[(2 * r2) * blk:(2 * r2 + 1) * blk]
            od = o[(2 * r2 + 1) * blk:(2 * r2 + 2) * blk]
            o_ref[:, c * LANES:(c + 1) * LANES] = jnp.where(low_head, ev, od).astype(o_ref.dtype)


def swa_attention(proj, pos_col, invf, sinks, bsz, seq):
    n = proj.shape[0]
    blk = ATT_BLK
    nb = seq // blk
    aq = A_HEADS * A_HEAD_DIM
    kcol = aq // LANES
    vcol = kcol + 1

    def cur(b, i, s):
        return b * nb + i

    def prev(b, i, s):
        return b * nb + jnp.maximum(i - 1, 0)

    grid_spec = pltpu.PrefetchScalarGridSpec(
        num_scalar_prefetch=1,
        grid=(bsz, nb),
        in_specs=[pl.BlockSpec((blk, aq), lambda b, i, s: (cur(b, i, s), 0)),
                  pl.BlockSpec((blk, LANES), lambda b, i, s: (cur(b, i, s), kcol)),
                  pl.BlockSpec((blk, LANES), lambda b, i, s: (cur(b, i, s), vcol)),
                  pl.BlockSpec((blk, LANES), lambda b, i, s: (prev(b, i, s), vcol)),
                  pl.BlockSpec((blk, 1), lambda b, i, s: (cur(b, i, s), 0)),
                  pl.BlockSpec((1, LANES), lambda b, i, s: (0, 0))],
        out_specs=pl.BlockSpec((blk, aq), lambda b, i, s: (cur(b, i, s), 0)),
        scratch_shapes=[pltpu.VMEM((blk, LANES), F32)],
    )
    return pl.pallas_call(
        _attn_body,
        grid_spec=grid_spec,
        out_shape=jax.ShapeDtypeStruct((n, aq), BF16),
        compiler_params=_cparams(("arbitrary", "arbitrary")),
        name="swa_attention",
    )(sinks, proj, proj, proj, proj, pos_col, invf)


def _hgrn_body(q_ref, f_ref, i_ref, g_ref, lb_ref, gw_ref, o_ref, st_ref):
    L = GLA_CHUNK

    @pl.when(pl.program_id(2) == 0)
    def _():
        st_ref[...] = jnp.zeros_like(st_ref)

    T = q_ref.shape[0]
    nch = T // L
    nt = (((1,), (1,)), ((), ()))
    pos = lax.broadcasted_iota(I32, (T, B_DIM), 0) % L
    r = lax.broadcasted_iota(I32, (T, T), 0)
    c = lax.broadcasted_iota(I32, (T, T), 1)
    same_chunk_causal = (r >= c) & (r // L == c // L)

    for hh in range(q_ref.shape[1] // B_DIM):
        hs = slice(hh * B_DIM, (hh + 1) * B_DIM)
        lb = lb_ref[hh]
        gw = gw_ref[hh]
        f = lb + (1.0 - lb) * jax.nn.sigmoid(f_ref[:, hs])
        kk = 1.0 - f
        qb = q_ref[:, hs]
        q = qb * jax.nn.sigmoid(qb)
        v = i_ref[:, hs]
        b = jnp.log(f)
        sh = 1
        while sh < L:
            b = b + jnp.where(pos >= sh, pltpu.roll(b, sh, 0), 0.0)
            sh *= 2

        def per_chunk(row, b=b):
            return jnp.concatenate(
                [jnp.broadcast_to(b[ci * L + row:ci * L + row + 1], (L, B_DIM)) for ci in range(nch)], axis=0)

        b_mid = per_chunk(L // 2)
        b_last = per_chunk(L - 1)
        qa = (q * jnp.exp(b - b_mid)).astype(BF16)
        ka = (kk * jnp.exp(b_mid - b)).astype(BF16)
        qe = (q * jnp.exp(b)).astype(BF16)
        kd = (kk * jnp.exp(b_last - b)).astype(BF16)
        att = lax.dot_general(qa, ka, nt, preferred_element_type=F32)
        att = jnp.where(same_chunk_causal, att, 0.0)
        o_intra = jnp.dot(att.astype(BF16), v.astype(BF16), preferred_element_type=F32)
        vt = v.T.astype(BF16)
        st = st_ref[hh]
        o_inter = []
        for ci in range(nch):
            sl = slice(ci * L, (ci + 1) * L)
            o_inter.append(lax.dot_general(qe[sl], st.astype(BF16), nt, preferred_element_type=F32))
            u_t = jnp.dot(vt[:, sl], kd[sl], preferred_element_type=F32)
            st = jnp.exp(b[ci * L + L - 1:ci * L + L]) * st + u_t
        st_ref[hh] = st
        o = o_intra + jnp.concatenate(o_inter, axis=0)
        o = o * lax.rsqrt(jnp.mean(o * o, axis=-1, keepdims=True) + RMS_EPS) * gw
        gb = g_ref[:, hs]
        o_ref[:, hs] = (o * (gb * jax.nn.sigmoid(gb))).astype(o_ref.dtype)


def hgrn2(proj, lb, gnorm_w, bsz, seq):
    n = proj.shape[0]
    t = min(HGRN_T, seq)
    nt = seq // t
    hp = HGRN_HEADS_PER_STEP
    width = hp * B_DIM
    base = (A_HEADS + 2 * A_KV_HEADS) * A_HEAD_DIM // width
    h = B_HEADS
    assert h % hp == 0 and (A_HEADS + 2 * A_KV_HEADS) * A_HEAD_DIM % width == 0

    def col(k):
        return lambda b, hd, i: (b * nt + i, base + k * (h // hp) + hd)

    return pl.pallas_call(
        _hgrn_body,
        grid=(bsz, h // hp, nt),
        in_specs=[pl.BlockSpec((t, width), col(0)),
                  pl.BlockSpec((t, width), col(1)),
                  pl.BlockSpec((t, width), col(2)),
                  pl.BlockSpec((t, width), col(3)),
                  pl.BlockSpec((hp, 1, B_DIM), lambda b, hd, i: (hd, 0, 0)),
                  pl.BlockSpec((hp, 1, B_DIM), lambda b, hd, i: (hd, 0, 0))],
        out_specs=pl.BlockSpec((t, width), lambda b, hd, i: (b * nt + i, hd)),
        out_shape=jax.ShapeDtypeStruct((n, h * B_DIM), BF16),
        scratch_shapes=[pltpu.VMEM((hp, B_DIM, B_DIM), F32)],
        compiler_params=_cparams(("arbitrary", "arbitrary", "arbitrary")),
        name="hgrn2",
    )(proj, proj, proj, proj, lb.reshape(h, 1, B_DIM), gnorm_w.reshape(h, 1, B_DIM))


def _outproj_ab_body(oa_ref, ob_ref, w_ref, b_ref, h_ref, gm_ref, lg_ref, lb_ref, o_ref):
    ka = oa_ref.shape[1]
    tm = oa_ref.shape[0]
    sub = min(tm, ROW_SUB)
    for r0 in range(0, tm, sub):
        rs = slice(r0, r0 + sub)
        y = jnp.dot(oa_ref[rs, :], w_ref[:ka, :], preferred_element_type=F32)
        y = y + jnp.dot(ob_ref[rs, :], w_ref[ka:, :], preferred_element_type=F32) + b_ref[...]
        z = DN_ALPHA * h_ref[rs, :] + (1.0 + gm_ref[0]) * y
        o_ref[rs, :] = _layer_norm(z, lg_ref[...], lb_ref[...])


def outproj_ab(oa, ob, w_bf, b, h, gm, ln_g, ln_b, seq):
    n, d = h.shape
    tm = min(512, seq)
    per_b = seq // tm
    ka, kb = oa.shape[1], ob.shape[1]
    row = lambda i: (i, 0)
    const = lambda i: (0, 0)
    return pl.pallas_call(
        _outproj_ab_body,
        grid=(n // tm,),
        in_specs=[pl.BlockSpec((tm, ka), row),
                  pl.BlockSpec((tm, kb), row),
                  _resident((ka + kb, d), const),
                  pl.BlockSpec((1, d), const),
                  pl.BlockSpec((tm, d), row),
                  pl.BlockSpec((1, 1, d), lambda i: (i // per_b, 0, 0)),
                  pl.BlockSpec((1, d), const),
                  pl.BlockSpec((1, d), const)],
        out_specs=pl.BlockSpec((tm, d), row),
        out_shape=jax.ShapeDtypeStruct((n, d), F32),
        compiler_params=_cparams(("arbitrary",)),
        name="outproj_ab",
    )(oa, ob, w_bf, b.reshape(1, d), h, gm, ln_g.reshape(1, d), ln_b.reshape(1, d))


def _router_body(h_ref, sc_ref, sh_ref, rw_ref, rb_ref, meta_ref, cnt_ref, carry_ref, rwh_ref, rwl_ref):
    tm = h_ref.shape[0]

    @pl.when(pl.program_id(0) == 0)
    def _():
        carry_ref[...] = jnp.zeros_like(carry_ref)
        w = rw_ref[...]
        w_hi = w.astype(BF16)
        rwh_ref[...] = w_hi
        rwl_ref[...] = (w - w_hi.astype(F32)).astype(BF16)

    xin = h_ref[...] * (1.0 + sc_ref[0]) + sh_ref[0]
    x_hi = xin.astype(BF16)
    x_lo = (xin - x_hi.astype(F32)).astype(BF16)
    logits = (jnp.dot(x_hi, rwh_ref[...], preferred_element_type=F32)
              + (jnp.dot(x_hi, rwl_ref[...], preferred_element_type=F32)
                 + jnp.dot(x_lo, rwh_ref[...], preferred_element_type=F32))) + rb_ref[...]
    lane = lax.broadcasted_iota(I32, (tm, LANES), 1)
    l = logits
    vals, idxs = [], []
    sel = jnp.zeros((tm, LANES), F32)
    for _ in range(TOP_K):
        m = jnp.max(l, axis=-1, keepdims=True)
        idx = jnp.min(jnp.where(l == m, lane, LANES), axis=-1, keepdims=True)
        hit = lane == idx
        vals.append(m)
        idxs.append(idx)
        sel = jnp.where(hit, 1.0, sel)
        l = jnp.where(hit, -jnp.inf, l)
    es = [jnp.exp(v - vals[0]) for v in vals]
    den = es[0] + es[1] + es[2] + es[3]
    r = lax.broadcasted_iota(I32, (tm, tm), 0)
    c = lax.broadcasted_iota(I32, (tm, tm), 1)
    tril = (r >= c).astype(BF16)
    incl = jnp.dot(tril, sel.astype(BF16), preferred_element_type=F32)
    carry = carry_ref[...]
    excl = incl - sel + carry
    meta = jnp.zeros((tm, LANES), F32)
    for k in range(TOP_K):
        rank = jnp.sum(jnp.where(lane == idxs[k], excl, 0.0), axis=-1, keepdims=True)
        meta = jnp.where(lane == k, idxs[k].astype(F32), meta)
        meta = jnp.where(lane == TOP_K + k, es[k] / den, meta)
        meta = jnp.where(lane == 2 * TOP_K + k, rank, meta)
    meta_ref[...] = meta
    carry = carry + incl[tm - 1:tm, :]
    carry_ref[...] = carry
    cnt_ref[...] = jnp.broadcast_to(carry, cnt_ref.shape)


def moe_router(h, sc, sh, rw_pad, rb_pad, seq):
    n, d = h.shape
    tm = min(512, seq)
    per_b = seq // tm
    return pl.pallas_call(
        _router_body,
        grid=(n // tm,),
        in_specs=[pl.BlockSpec((tm, d), lambda i: (i, 0)),
                  pl.BlockSpec((1, 1, d), lambda i: (i // per_b, 0, 0)),
                  pl.BlockSpec((1, 1, d), lambda i: (i // per_b, 0, 0)),
                  pl.BlockSpec((d, LANES), lambda i: (0, 0)),
                  pl.BlockSpec((1, LANES), lambda i: (0, 0))],
        out_specs=[pl.BlockSpec((tm, LANES), lambda i: (i, 0)),
                   pl.BlockSpec((8, LANES), lambda i: (0, 0))],
        out_shape=[jax.ShapeDtypeStruct((n, LANES), F32),
                   jax.ShapeDtypeStruct((8, LANES), F32)],
        scratch_shapes=[pltpu.VMEM((1, LANES), F32), pltpu.VMEM((d, LANES), BF16), pltpu.VMEM((d, LANES), BF16)],
        compiler_params=_cparams(("arbitrary",)),
        name="moe_router",
    )(h, sc, sh, rw_pad, rb_pad)


def _dispatch_body(pad_ref, dest_ref, h_ref, sc_ref, sh_ref, xs_hbm, buf_ref, zero_ref, sem, zsem):
    tm, d = h_ref.shape
    half = d // 2
    pitch = _packed_pitch(d)
    i = pl.program_id(0)
    last = pl.num_programs(0) - 1
    slot = i % 2
    n_pad = pad_ref.shape[0]

    def wait_slot(s):
        for _ in range(TOP_K):
            pltpu.make_async_copy(buf_ref.at[s], xs_hbm.at[pl.ds(0, tm * pitch)], sem.at[s]).wait()

    @pl.when(i == 0)
    def _():
        zero_ref[...] = jnp.zeros_like(zero_ref)
        buf_ref[...] = jnp.zeros_like(buf_ref)

        def zfill(q2, carry):
            for pr in range(2):
                dst = pad_ref[q2 * 2 + pr] * pitch
                pltpu.make_async_copy(zero_ref, xs_hbm.at[pl.ds(dst, pitch)], zsem.at[0]).start(priority=pr)
            return carry

        lax.fori_loop(0, n_pad // 2, zfill, 0)

    @pl.when(i >= 2)
    def _():
        wait_slot(slot)

    xin = h_ref[...] * (1.0 + sc_ref[0]) + sh_ref[0]
    lo = lax.bitcast_convert_type(xin[:, :half].astype(BF16).astype(F32), U32)
    hi = lax.bitcast_convert_type(xin[:, half:].astype(BF16).astype(F32), U32)
    word = lax.shift_right_logical(lo, jnp.uint32(16)) | (hi & jnp.uint32(0xFFFF0000))
    for s in range(half // LANES):
        buf_ref[slot, pl.ds(s, tm, stride=pitch), :] = word[:, s * LANES:(s + 1) * LANES]

    def issue(g, carry):
        for rr in range(DMA_GROUP):
            r = g * DMA_GROUP + rr
            src = buf_ref.at[slot, pl.ds(r * pitch, pitch)]
            for k in range(TOP_K):
                dst = dest_ref[0, 0, r * TOP_K + k] * pitch
                pltpu.make_async_copy(src, xs_hbm.at[pl.ds(dst, pitch)], sem.at[slot]).start(priority=k % 2)
        return carry

    lax.fori_loop(0, tm // DMA_GROUP, issue, 0)

    @pl.when(i == last)
    def _():
        wait_slot(slot)

        @pl.when(i >= 1)
        def _():
            wait_slot(1 - slot)

        for _ in range(n_pad // tm):
            pltpu.make_async_copy(buf_ref.at[0], xs_hbm.at[pl.ds(0, tm * pitch)], zsem.at[0]).wait()


def moe_dispatch(h, sc, sh, dest, pad_dest, n_rows, seq):
    n, d = h.shape
    tm = min(256, seq)
    per_b = seq // tm
    pitch = _packed_pitch(d)
    assert pad_dest.shape[0] % tm == 0
    grid_spec = pltpu.PrefetchScalarGridSpec(
        num_scalar_prefetch=1,
        grid=(n // tm,),
        in_specs=[pl.BlockSpec((1, 1, tm * TOP_K), lambda i, p: (i, 0, 0), memory_space=pltpu.SMEM),
                  pl.BlockSpec((tm, d), lambda i, p: (i, 0)),
                  pl.BlockSpec((1, 1, d), lambda i, p: (i // per_b, 0, 0)),
                  pl.BlockSpec((1, 1, d), lambda i, p: (i // per_b, 0, 0))],
        out_specs=pl.BlockSpec(memory_space=pl.ANY),
        scratch_shapes=[pltpu.VMEM((2, tm * pitch, LANES), U32), pltpu.VMEM((pitch, LANES), U32),
                        pltpu.SemaphoreType.DMA((2,)), pltpu.SemaphoreType.DMA((1,))],
    )
    return pl.pallas_call(
        _dispatch_body,
        grid_spec=grid_spec,
        out_shape=jax.ShapeDtypeStruct((n_rows * pitch, LANES), U32),
        compiler_params=_cparams(("arbitrary",)),
        name="moe_dispatch",
    )(pad_dest, dest.reshape(n // tm, 1, tm * TOP_K), h, sc, sh)


SCHED_MODE, SCHED_BLOCK, SCHED_EXPERT, SCHED_SLOT, SCHED_DL_EXPERT, SCHED_DL_C1, SCHED_DL_C2, SCHED_DL_SLOT = range(8)
MODE_IDLE, MODE_COMPUTE, MODE_ZERO = 0, 1, 2


def _expert_body(sched_ref, xs_ref, w1c_ref, w2c_ref, b1_ref, b2_ref, y_ref, wb1a_ref, wb1b_ref, wb2a_ref, wb2b_ref):
    j = pl.program_id(0)
    nstep = pl.num_programs(0)
    d = wb1a_ref.shape[0]
    ff = wb2a_ref.shape[0]
    rt = d // LANES
    pitch = _row_pitch(d)
    xpitch = _packed_pitch(d)
    blk = xs_ref.shape[0] // xpitch
    cache = ((wb1a_ref, wb2a_ref), (wb1b_ref, wb2b_ref))

    def sched(field):
        return sched_ref[field * nstep + j]

    def cast_chunks(wb1_ref, wb2_ref):
        r1 = pl.multiple_of(sched(SCHED_DL_C1) * W_CHUNK, W_CHUNK)
        r2 = pl.multiple_of(sched(SCHED_DL_C2) * W_CHUNK, W_CHUNK)
        wb1_ref[pl.ds(r1, W_CHUNK), :] = w1c_ref[0].astype(BF16)
        wb2_ref[pl.ds(r2, W_CHUNK), :] = w2c_ref[0].astype(BF16)

    def compute(wb1_ref, wb2_ref):
        sub = blk // EXPERT_SUB_BLOCKS
        for r0 in range(0, blk, sub):
            words = [xs_ref[pl.ds(r0 * xpitch + s, sub, stride=xpitch), :] for s in range(rt // 2)]
            lo = [lax.bitcast_convert_type(lax.shift_left(w, jnp.uint32(16)), F32).astype(BF16) for w in words]
            hi = [lax.bitcast_convert_type(w & jnp.uint32(0xFFFF0000), F32).astype(BF16) for w in words]
            x = jnp.concatenate(lo + hi, axis=1)
            hid = jnp.dot(x, wb1_ref[...], preferred_element_type=F32) + b1_ref[0]
            gate = jnp.minimum(hid[:, :ff], SWIGLU_LIMIT)
            lin = jnp.clip(hid[:, ff:], -SWIGLU_LIMIT, SWIGLU_LIMIT)
            act = gate * jax.nn.sigmoid(SWIGLU_ALPHA * gate) * (lin + 1.0)
            y = jnp.dot(act.astype(BF16), wb2_ref[...], preferred_element_type=F32) + b2_ref[0]
            for s in range(rt):
                y_ref[pl.ds(r0 * pitch + s, sub, stride=pitch), :] = y[:, s * LANES:(s + 1) * LANES]
            for s in range(rt, pitch):
                y_ref[pl.ds(r0 * pitch + s, sub, stride=pitch), :] = jnp.zeros((sub, LANES), F32)

    mode = sched(SCHED_MODE)
    slot = sched(SCHED_SLOT)
    dl_slot = sched(SCHED_DL_SLOT)
    for par in range(2):
        @pl.when((mode == MODE_COMPUTE) & (slot == par))
        def _(par=par):
            cast_chunks(*cache[1 - par])
            compute(*cache[par])

        @pl.when((mode != MODE_COMPUTE) & (dl_slot == par))
        def _(par=par):
            cast_chunks(*cache[par])

    @pl.when(mode == MODE_ZERO)
    def _():
        y_ref[...] = jnp.zeros_like(y_ref)


def expert_schedule(padded, n_used, nblk, layer, ne, need, need2):
    nb = padded // MOE_BLK
    seg_len = jnp.concatenate([jnp.full((1,), need, I32), jnp.maximum(nb[:-1], need), nb[-1:]])
    seg_end = jnp.cumsum(seg_len)
    seg_start = seg_end - seg_len
    nstep = nblk + (ne + 1) * need
    j = jnp.arange(nstep, dtype=I32)
    seg = jnp.sum(seg_end[None, :] <= j[:, None], axis=-1).astype(I32)
    in_seg = seg <= ne
    segc = jnp.minimum(seg, ne)
    loc = j - seg_start[segc]
    expert = jnp.clip(segc - 1, 0, ne - 1)
    compute = in_seg & (segc >= 1) & (loc < nb[expert])
    done = jnp.cumsum(compute.astype(I32)) - compute.astype(I32)
    tail = j - seg_end[-1]
    ntail = nblk - n_used
    zero = jnp.logical_not(in_seg) & (tail < ntail)
    parked = jnp.where(jnp.logical_not(in_seg) & (ntail > 0), nblk - 1, jnp.maximum(done - 1, 0))
    block = jnp.where(compute, done, jnp.where(zero, n_used + tail, parked))
    mode = jnp.where(compute, MODE_COMPUTE, jnp.where(zero, MODE_ZERO, MODE_IDLE))
    deliver = in_seg & (segc <= ne - 1) & (loc < need)
    last = lax.cummax(jnp.where(deliver, j, -1), axis=0)
    dl_expert = segc[last]
    dl_c1 = loc[last]
    dl_c2 = jnp.minimum(loc[last], need2 - 1)
    table = jnp.stack([mode, block, expert + layer * ne, expert % 2,
                       dl_expert + layer * ne, dl_c1, dl_c2, dl_expert % 2])
    return table.astype(I32).reshape(-1), nstep


def moe_experts(xs, sched, nstep, w1, b1, w2, b2):
    ne, d, ff2 = w1.shape
    pitch = _row_pitch(d)
    xpitch = _packed_pitch(d)
    n_rows = xs.shape[0] // xpitch
    ff = ff2 // 2

    def fld(s, j, field):
        return s[field * nstep + j]

    grid_spec = pltpu.PrefetchScalarGridSpec(
        num_scalar_prefetch=1,
        grid=(nstep,),
        in_specs=[pl.BlockSpec((MOE_BLK * xpitch, LANES), lambda j, s: (fld(s, j, SCHED_BLOCK), 0)),
                  pl.BlockSpec((1, W_CHUNK, ff2),
                               lambda j, s: (fld(s, j, SCHED_DL_EXPERT), fld(s, j, SCHED_DL_C1), 0)),
                  pl.BlockSpec((1, W_CHUNK, d),
                               lambda j, s: (fld(s, j, SCHED_DL_EXPERT), fld(s, j, SCHED_DL_C2), 0)),
                  pl.BlockSpec((1, 1, ff2), lambda j, s: (fld(s, j, SCHED_EXPERT), 0, 0)),
                  pl.BlockSpec((1, 1, d), lambda j, s: (fld(s, j, SCHED_EXPERT), 0, 0))],
        out_specs=pl.BlockSpec((MOE_BLK * pitch, LANES), lambda j, s: (fld(s, j, SCHED_BLOCK), 0)),
        scratch_shapes=[pltpu.VMEM((d, ff2), BF16), pltpu.VMEM((d, ff2), BF16),
                        pltpu.VMEM((ff, d), BF16), pltpu.VMEM((ff, d), BF16)],
    )
    return pl.pallas_call(
        _expert_body,
        grid_spec=grid_spec,
        out_shape=jax.ShapeDtypeStruct((n_rows * pitch, LANES), F32),
        compiler_params=_cparams(("arbitrary",)),
        name="moe_experts",
    )(sched, xs, w1, w2, b1.reshape(ne, 1, ff2), b2.reshape(ne, 1, d))


def _combine_body(dc_ref, dn_ref, meta_ref, h_ref, gf_ref, lg_ref, lb_ref, y_hbm, o_ref, buf_ref, sem):
    tm, d = h_ref.shape
    rt = d // LANES
    pitch = _row_pitch(d)
    grp = COMBINE_GROUP
    i = pl.program_id(0)
    n = pl.num_programs(0)
    slot = i % 2

    def issue_rows(dref, s, r0):
        for rr in range(grp):
            r = r0 + rr
            for k in range(TOP_K):
                src = dref[0, 0, r * TOP_K + k] * pitch
                pltpu.make_async_copy(y_hbm.at[pl.ds(src, rt)], buf_ref.at[s, k, pl.ds(r * pitch, rt)],
                                      sem.at[s]).start(priority=k % 2)

    @pl.when(i == 0)
    def _():
        def body(g, carry):
            issue_rows(dc_ref, 0, g * grp)
            return carry
        lax.fori_loop(0, tm // grp, body, 0)

    for k in range(TOP_K):
        pltpu.make_async_copy(y_hbm.at[pl.ds(0, tm * rt)], buf_ref.at[slot, k, pl.ds(0, tm * rt)],
                              sem.at[slot]).wait()

    def group(g, carry, prefetch):
        r0 = pl.multiple_of(g * grp, grp)
        if prefetch:
            issue_rows(dn_ref, 1 - slot, r0)
        meta = meta_ref[pl.ds(r0, grp), :]
        gates = [jnp.broadcast_to(meta[:, TOP_K + k:TOP_K + k + 1], (grp, LANES)) for k in range(TOP_K)]
        for s in range(rt):
            a = gates[0] * buf_ref[slot, 0, pl.ds(r0 * pitch + s, grp, stride=pitch), :]
            for k in range(1, TOP_K):
                a = a + gates[k] * buf_ref[slot, k, pl.ds(r0 * pitch + s, grp, stride=pitch), :]
            cs = slice(s * LANES, (s + 1) * LANES)
            o_ref[pl.ds(r0, grp), cs] = DN_ALPHA * h_ref[pl.ds(r0, grp), cs] + (1.0 + gf_ref[0, :, cs]) * a
        return carry

    @pl.when(i + 1 < n)
    def _():
        lax.fori_loop(0, tm // grp, functools.partial(group, prefetch=True), 0)

    @pl.when(i + 1 == n)
    def _():
        lax.fori_loop(0, tm // grp, functools.partial(group, prefetch=False), 0)

    o_ref[...] = _layer_norm(o_ref[...], lg_ref[...], lb_ref[...])


def moe_combine(y, dest, meta, h, gf, ln_g, ln_b, seq):
    n, d = h.shape
    tm = min(256, seq)
    per_b = seq // tm
    nblk = n // tm
    dest3 = dest.reshape(nblk, 1, tm * TOP_K)
    row = lambda i: (i, 0)
    const = lambda i: (0, 0)
    return pl.pallas_call(
        _combine_body,
        grid=(nblk,),
        in_specs=[pl.BlockSpec((1, 1, tm * TOP_K), lambda i: (i, 0, 0), memory_space=pltpu.SMEM),
                  pl.BlockSpec((1, 1, tm * TOP_K), lambda i: (jnp.minimum(i + 1, nblk - 1), 0, 0),
                               memory_space=pltpu.SMEM),
                  pl.BlockSpec((tm, LANES), row),
                  pl.BlockSpec((tm, d), row),
                  pl.BlockSpec((1, 1, d), lambda i: (i // per_b, 0, 0)),
                  pl.BlockSpec((1, d), const),
                  pl.BlockSpec((1, d), const),
                  pl.BlockSpec(memory_space=pl.ANY)],
        out_specs=pl.BlockSpec((tm, d), row),
        out_shape=jax.ShapeDtypeStruct((n, d), F32),
        scratch_shapes=[pltpu.VMEM((2, TOP_K, tm * _row_pitch(d), LANES), F32), pltpu.SemaphoreType.DMA((2,))],
        compiler_params=_cparams(("arbitrary",)),
        name="moe_combine",
    )(dest3, dest3, meta, h, gf, ln_g.reshape(1, d), ln_b.reshape(1, d), y)


def moe_layer(h, sc, sh, gf, ln_g, ln_b, router_w, router_b, layer, w1, b1, w2, b2, seq):
    n, d = h.shape
    ne = router_w.shape[1]
    rw_pad = jnp.pad(router_w, ((0, 0), (0, LANES - ne)))
    rb_pad = jnp.pad(router_b, (0, LANES - ne), constant_values=NEG).reshape(1, LANES)
    meta, cnt = moe_router(h, sc, sh, rw_pad, rb_pad, seq)
    idx4 = meta[:, 0:TOP_K].astype(I32)
    rank4 = meta[:, 2 * TOP_K:3 * TOP_K].astype(I32)
    counts = cnt[0, :ne].astype(I32)
    padded = (counts + MOE_BLK - 1) // MOE_BLK * MOE_BLK
    end = jnp.cumsum(padded)
    start = end - padded
    onehot = idx4[:, :, None] == jnp.arange(ne, dtype=I32)[None, None, :]
    dest = jnp.sum(jnp.where(onehot, start[None, None, :], 0), axis=-1) + rank4
    n_rows = -(-(n * TOP_K + ne * (MOE_BLK - 1)) // MOE_BLK) * MOE_BLK
    nblk = n_rows // MOE_BLK
    blk0 = jnp.arange(nblk, dtype=I32) * MOE_BLK
    n_used = (end[-1] // MOE_BLK).astype(I32)
    n_pad = n_rows - n * TOP_K
    seg_cnt = jnp.concatenate([padded - counts, (n_rows - end[-1])[None]])
    seg_base = jnp.concatenate([start + counts, end[-1:]])
    seg_end = jnp.cumsum(seg_cnt)
    q = jnp.arange(n_pad, dtype=I32)
    seg = jnp.sum(seg_end[None, :] <= q[:, None], axis=-1)
    pad_dest = (seg_base[seg] + q - (seg_end - seg_cnt)[seg]).astype(I32)
    xs = moe_dispatch(h, sc, sh, dest.reshape(-1), pad_dest, n_rows, seq)
    sched, nstep = expert_schedule(padded, n_used, nblk, layer, ne, w1.shape[1] // W_CHUNK, w2.shape[1] // W_CHUNK)
    y = moe_experts(xs, sched, nstep, w1, b1, w2, b2)
    return moe_combine(y, dest.reshape(-1), meta, h, gf, ln_g, ln_b, seq)


def _inproj_c_body(x_ref, sc_ref, sh_ref, w_ref, o_ref, acc_ref):
    tm = x_ref.shape[0]
    ncol = w_ref.shape[1] // LANES
    sub = min(tm, ROW_SUB)
    for r0 in range(0, tm, sub):
        rs = slice(r0, r0 + sub)
        xb = (x_ref[rs, :] * (1.0 + sc_ref[0]) + sh_ref[0]).astype(BF16)
        acc = jnp.dot(xb, w_ref[...], preferred_element_type=F32)
        for c in range(ncol):
            acc_ref[c, rs, :] = acc[:, c * LANES:(c + 1) * LANES]
        ch = slice(r0 // SSM_T, (r0 + sub) // SSM_T)
        for t in range(SSM_T):
            for c in range(ncol):
                o_ref[t, ch, c * LANES:(c + 1) * LANES] = acc_ref[c, pl.ds(r0 + t, sub // SSM_T, stride=SSM_T), :]


def inproj_c(h, sc, sh, w_bf, seq):
    n, d = h.shape
    width = w_bf.shape[1]
    tm = min(1024, seq)
    per_b = seq // tm
    return pl.pallas_call(
        _inproj_c_body,
        grid=(n // tm,),
        in_specs=[pl.BlockSpec((tm, d), lambda i: (i, 0)),
                  pl.BlockSpec((1, 1, d), lambda i: (i // per_b, 0, 0)),
                  pl.BlockSpec((1, 1, d), lambda i: (i // per_b, 0, 0)),
                  _resident((d, width), lambda i: (0, 0))],
        out_specs=pl.BlockSpec((SSM_T, tm // SSM_T, width), lambda i: (0, i, 0)),
        out_shape=jax.ShapeDtypeStruct((SSM_T, n // SSM_T, width), F32),
        scratch_shapes=[pltpu.VMEM((width // LANES, tm, LANES), F32)],
        compiler_params=_cparams(("arbitrary",)),
        name="inproj_c",
    )(h, sc, sh, w_bf)


def _s5_body(u_ref, wt_ref, wa_ref, wc_ref, lam_ref, d_ref, o_ref, st_ref, v_ref, xs_ref):
    nb, cb = u_ref.shape[1], u_ref.shape[2]
    ns = st_ref.shape[1] // 2
    kdim = SSM_T * LANES

    @pl.when(pl.program_id(1) == 0)
    def _():
        st_ref[...] = jnp.zeros_like(st_ref)

    xcat = jnp.concatenate(
        [jnp.concatenate([u_ref[t, b].astype(BF16) for t in range(SSM_T)], axis=1) for b in range(nb)], axis=0)
    v = jnp.dot(xcat, wa_ref[0], preferred_element_type=F32)
    nslab = 2 * ns // LANES
    for c in range(nslab):
        v_ref[c] = v[:, c * LANES:(c + 1) * LANES]
    lam = lam_ref[0]
    a_re, a_im = lam[:, :ns], lam[:, ns:]

    def step(r, st):
        for c in range(nslab):
            xs_ref[c, pl.ds(r, nb, stride=cb), :] = st[:, c * LANES:(c + 1) * LANES]
        v = jnp.concatenate([v_ref[c, pl.ds(r, nb, stride=cb), :] for c in range(nslab)], axis=1)
        s_re, s_im = st[:, :ns], st[:, ns:]
        n_re = a_re * s_re - a_im * s_im + v[:, :ns]
        n_im = a_re * s_im + a_im * s_re + v[:, ns:]
        return jnp.concatenate([n_re, n_im], axis=1)

    st_ref[...] = lax.fori_loop(0, cb, step, st_ref[...])
    xs_bf = jnp.concatenate([xs_ref[c].astype(BF16) for c in range(nslab)], axis=1)
    dvec = d_ref[0]
    for q in range(S5_COL_BLOCKS):
        c0, c1 = q * kdim // S5_COL_BLOCKS, (q + 1) * kdim // S5_COL_BLOCKS
        y = jnp.dot(xcat[:, :c1], wt_ref[0, :c1, c0:c1], preferred_element_type=F32)
        y = y + jnp.dot(xs_bf, wc_ref[0, :, c0:c1], preferred_element_type=F32)
        for tt in range((c1 - c0) // LANES):
            t = c0 // LANES + tt
            for b in range(nb):
                yt = y[b * cb:(b + 1) * cb, tt * LANES:(tt + 1) * LANES] + dvec * u_ref[t, b]
                gel = 0.5 * yt * (1.0 + jnp.tanh(0.7978845608028654 * (yt + 0.044715 * yt * yt * yt)))
                o_ref[t, b] = gel.astype(o_ref.dtype)


def s5_mixer(u2, wt, wa, wc, lam_t, dpack, bsz, seq):
    _, nchunk, width = u2.shape
    npack = width // LANES
    per_b = seq // SSM_T
    cb = min(256, per_b)
    nblk = per_b // cb
    ns2 = wa.shape[2]
    kdim = SSM_T * LANES
    pk = lambda j, i: (j, 0, 0)
    u4 = u2.reshape(SSM_T, bsz, per_b, width)
    y4 = pl.pallas_call(
        _s5_body,
        grid=(npack, nblk),
        in_specs=[pl.BlockSpec((SSM_T, bsz, cb, LANES), lambda j, i: (0, 0, i, j)),
                  _resident((1, kdim, kdim), pk),
                  _resident((1, kdim, ns2), pk),
                  _resident((1, ns2, kdim), pk),
                  pl.BlockSpec((1, 1, ns2), pk),
                  pl.BlockSpec((1, 1, LANES), pk)],
        out_specs=pl.BlockSpec((SSM_T, bsz, cb, LANES), lambda j, i: (0, 0, i, j)),
        out_shape=jax.ShapeDtypeStruct((SSM_T, bsz, per_b, width), F32),
        scratch_shapes=[pltpu.VMEM((bsz, ns2), F32), pltpu.VMEM((ns2 // LANES, bsz * cb, LANES), F32),
                        pltpu.VMEM((ns2 // LANES, bsz * cb, LANES), F32)],
        compiler_params=_cparams(("arbitrary", "arbitrary")),
        name="s5_mixer",
    )(u4, wt, wa, wc, lam_t, dpack)
    return y4.reshape(SSM_T, nchunk, width)


def s5_operators(a_re, a_im, log_dt, b_re, b_im, c_re, c_im, dskip):
    g, p = a_re.shape
    cg = b_re.shape[2]
    npack = g // SSM_PACK
    ns2 = SSM_PACK * 2 * p
    t = SSM_T
    lam = lax.complex(a_re, a_im)
    dt = jnp.exp(log_dt)[:, None]
    lam_bar = jnp.exp(lam * dt)
    b_bar = ((lam_bar - 1.0) / lam)[..., None] * lax.complex(b_re, b_im)
    c_mat = lax.complex(c_re, c_im)
    kpow = jnp.arange(t + 1, dtype=F32)[:, None, None]
    pows = jnp.exp((lam * dt)[None] * kpow)
    lane_c = jnp.arange(LANES) % cg
    lane_g = jnp.arange(LANES) // cg
    tile = (lane_c[None, :] == jnp.arange(cg)[:, None]).astype(F32)
    diag = lane_g[:, None] == lane_g[None, :]
    kern = jnp.real(jnp.einsum('gcp,kgp,gpd->kgdc', c_mat, pows[:t], b_bar, precision=HIGHEST))
    kfull = jnp.where(diag, jnp.dot(kern.reshape(t, npack, LANES, cg), tile, precision=HIGHEST), 0.0)
    kfull = kfull.astype(BF16)
    kcat = jnp.concatenate([kfull[tau] for tau in range(t)], axis=-1)
    wt = jnp.stack([jnp.pad(kcat[:, :, :(t - s) * LANES], ((0, 0), (0, 0), (s * LANES, 0)))
                    for s in range(t)], axis=1)
    wt = wt.reshape(npack, t * LANES, t * LANES)
    ns = SSM_PACK * p
    col_g = jnp.arange(ns) // p
    tile_p = (jnp.arange(ns)[None, :] % p == jnp.arange(p)[:, None]).astype(F32)
    wa_c = pows[t - 1 - jnp.arange(t)][:, :, :, None] * b_bar[None]
    wa_c = wa_c.reshape(t, npack, SSM_PACK, p, cg).transpose(1, 0, 2, 4, 3)
    wa_c = wa_c.reshape(npack, t * LANES, p)
    own_a = jnp.tile(lane_g, t)[:, None] == col_g[None, :]
    zero_bf = jnp.zeros((), BF16)
    wa = jnp.concatenate([jnp.where(own_a, jnp.dot(part.astype(BF16), tile_p.astype(BF16),
                                                  preferred_element_type=BF16), zero_bf)
                          for part in (jnp.real(wa_c), jnp.imag(wa_c))], axis=-1)
    wc_c = c_mat[None] * pows[1:t + 1][:, :, None, :]
    wc_c = wc_c.transpose(0, 1, 3, 2).reshape(t, npack, ns, cg)
    cc = jnp.concatenate([jnp.real(wc_c), -jnp.imag(wc_c)], axis=2)
    own_c = (jnp.arange(ns2) % ns // p)[:, None] == lane_g[None, :]
    wcf = jnp.where(own_c, jnp.dot(cc.astype(BF16), tile.astype(BF16), preferred_element_type=BF16),
                    zero_bf)
    wc = jnp.concatenate([wcf[tt] for tt in range(t)], axis=-1)
    lam_t = pows[t].reshape(npack, 1, ns)
    lam_t = jnp.concatenate([jnp.real(lam_t), jnp.imag(lam_t)], axis=-1)
    dpack = dskip.reshape(npack, 1, LANES)
    return wt, wa, wc, lam_t, dpack


def _glu_out_body(y_ref, wg_ref, bg_ref, wo_ref, h_ref, gm_ref, lg_ref, lb_ref, o_ref, ytm_ref):
    cb = y_ref.shape[1]
    ncol = y_ref.shape[2] // LANES
    for t in range(SSM_T):
        for c in range(ncol):
            ytm_ref[c, pl.ds(t, cb, stride=SSM_T), :] = y_ref[t, :, c * LANES:(c + 1) * LANES]
    tm = h_ref.shape[0]
    sub = min(tm, ROW_SUB)
    for r0 in range(0, tm, sub):
        rs = slice(r0, r0 + sub)
        y = jnp.concatenate([ytm_ref[c, rs, :] for c in range(ncol)], axis=1)
        gate = jnp.dot(y.astype(BF16), wg_ref[...], preferred_element_type=F32) + bg_ref[...]
        z = (y * jax.nn.sigmoid(gate)).astype(BF16)
        out = jnp.dot(z, wo_ref[...], preferred_element_type=F32)
        zz = DN_ALPHA * h_ref[rs, :] + (1.0 + gm_ref[0]) * out
        o_ref[rs, :] = _layer_norm(zz, lg_ref[...], lb_ref[...])


def glu_out(y2, wg_bf, bg, wo_bf, h, gm, ln_g, ln_b, seq):
    n, d = h.shape
    width = y2.shape[2]
    tm = min(512, seq)
    cb = tm // SSM_T
    per_b = seq // tm
    row = lambda i: (i, 0)
    const = lambda i: (0, 0)
    return pl.pallas_call(
        _glu_out_body,
        grid=(n // tm,),
        in_specs=[pl.BlockSpec((SSM_T, cb, width), lambda i: (0, i, 0)),
                  _resident((width, width), const),
                  pl.BlockSpec((1, width), const),
                  _resident((width, d), const),
                  pl.BlockSpec((tm, d), row),
                  pl.BlockSpec((1, 1, d), lambda i: (i // per_b, 0, 0)),
                  pl.BlockSpec((1, d), const),
                  pl.BlockSpec((1, d), const)],
        out_specs=pl.BlockSpec((tm, d), row),
        out_shape=jax.ShapeDtypeStruct((n, d), F32),
        scratch_shapes=[pltpu.VMEM((width // LANES, tm, LANES), F32)],
        compiler_params=_cparams(("arbitrary",)),
        name="glu_out",
    )(y2, wg_bf, bg.reshape(1, width), wo_bf, h, gm, ln_g.reshape(1, d), ln_b.reshape(1, d))


def kernel(x, c, positions, ada_w, ada_b, ln_g, ln_b, ab_in_w, ab_in_b, ab_sinks, ab_gnorm_w, ab_out_w, ab_out_b, hgrn_lb_logits, c_in_w, c_A_re, c_A_im, c_log_dt, c_B_re, c_B_im, c_C_re, c_C_im, c_D, c_glu_w, c_glu_b, c_out_w, router_w, router_b, exp_w1, exp_b1, exp_w2, exp_b2):
    bsz, seq, d = x.shape
    n = bsz * seq
    depth = ada_w.shape[0]
    mod = ada_mod(c, ada_w, ada_b)
    lb_all = jnp.cumsum(jax.nn.softmax(hgrn_lb_logits.astype(F32), axis=0), axis=0)
    half = A_HEAD_DIM // 2
    inv_freq = ROPE_THETA ** (-jnp.arange(half, dtype=F32) / half)
    invf = jnp.tile(inv_freq, LANES // half).reshape(1, LANES)
    pos_col = positions.reshape(n, 1)
    ne_all = depth * exp_w1.shape[1]
    w1_all = exp_w1.reshape(ne_all, d, exp_w1.shape[3])
    w2_all = exp_w2.reshape(ne_all, exp_w2.shape[2], d)
    b1_all = exp_b1.reshape(ne_all, exp_b1.shape[2])
    b2_all = exp_b2.reshape(ne_all, d)
    h = x.reshape(n, d)
    for layer in range(depth):
        m6 = mod[layer].reshape(bsz, 6, 1, d)
        sh_m, sc_m, g_m, sh_f, sc_f, g_f = [m6[:, k] for k in range(6)]
        i = layer // 2
        if layer % 2 == 0:
            proj = inproj_ab(h, sc_m, sh_m, ab_in_w[i].astype(BF16), ab_in_b[i], seq)
            oa = swa_attention(proj, pos_col, invf, ab_sinks[i], bsz, seq)
            ob = hgrn2(proj, lb_all[layer], ab_gnorm_w[i], bsz, seq)
            h = outproj_ab(oa, ob, ab_out_w[i].astype(BF16), ab_out_b[i], h, g_m,
                           ln_g[layer, 0], ln_b[layer, 0], seq)
        else:
            wt, wa, wc, lam_t, dpack = s5_operators(c_A_re[i], c_A_im[i], c_log_dt[i], c_B_re[i], c_B_im[i],
                                                    c_C_re[i], c_C_im[i], c_D[i])
            u2 = inproj_c(h, sc_m, sh_m, c_in_w[i].astype(BF16), seq)
            y2 = s5_mixer(u2, wt, wa, wc, lam_t, dpack, bsz, seq)
            h = glu_out(y2, c_glu_w[i].astype(BF16), c_glu_b[i], c_out_w[i].astype(BF16), h, g_m,
                        ln_g[layer, 0], ln_b[layer, 0], seq)
        h = moe_layer(h, sc_f, sh_f, g_f, ln_g[layer, 1], ln_b[layer, 1], router_w[layer], router_b[layer],
                      layer, w1_all, b1_all, w2_all, b2_all, seq)
    return h.reshape(bsz, seq, d)
```

```python
import functools

import jax
import jax.numpy as jnp
from jax import lax
from jax.experimental import pallas as pl
from jax.experimental.pallas import tpu as pltpu

F32 = jnp.float32
BF16 = jnp.bfloat16
I32 = jnp.int32
U32 = jnp.uint32
HIGHEST = lax.Precision.HIGHEST

DEPTH = 2
A_HEADS = 16
A_KV_HEADS = 2
A_HEAD_DIM = 64
WINDOW = 128
ROPE_THETA = 10000.0
B_HEADS = 8
B_DIM = 128
GLA_CHUNK = 64
C_GROUP = 16
C_STATE = 64
N_EXPERTS = 32
TOP_K = 4
SWIGLU_LIMIT = 7.0
SWIGLU_ALPHA = 1.702
DN_ALPHA = (2 * DEPTH) ** 0.25
LN_EPS = 1e-5
RMS_EPS = 1e-6

LANES = 128
VMEM_LIMIT = 56 * 1024 * 1024

ATT_BLK = 128
HGRN_T = 512
HGRN_HEADS_PER_STEP = 2
MOE_BLK = 256
SSM_T = 16
SSM_PACK = LANES // C_GROUP
S5_COL_BLOCKS = 4
DMA_GROUP = 8
COMBINE_GROUP = 32
ROW_SUB = 256
W_CHUNK = 256
EXPERT_SUB_BLOCKS = 1
NEG = -1e30


def _packed_pitch(d):
    return d // (2 * LANES) + 1


def _row_pitch(d):
    return d // LANES + 1


def _cparams(sem):
    return pltpu.CompilerParams(dimension_semantics=sem, vmem_limit_bytes=VMEM_LIMIT)


def _resident(block_shape, index_map):
    return pl.BlockSpec(block_shape, index_map, pipeline_mode=pl.Buffered(1))


def _layer_norm(z, g, b):
    mu = jnp.mean(z, axis=-1, keepdims=True)
    zc = z - mu
    var = jnp.mean(zc * zc, axis=-1, keepdims=True)
    return zc * lax.rsqrt(var + LN_EPS) * g + b


def _ada_body(ct_ref, w_ref, b_ref, o_ref):
    k = pl.program_id(1)
    ct = ct_ref[...]
    cond = ct * jax.nn.sigmoid(ct)
    w = w_ref[0]
    rows = [jnp.sum(w * cond[:, b:b + 1], axis=0, keepdims=True) for b in range(ct.shape[1])]
    part = jnp.concatenate(rows, axis=0)

    @pl.when(k == 0)
    def _():
        o_ref[0] = part + b_ref[0]

    @pl.when(k > 0)
    def _():
        o_ref[0] += part


def ada_mod(c, ada_w, ada_b):
    depth, d, n6 = ada_w.shape
    bsz = c.shape[0]
    tk = 256
    return pl.pallas_call(
        _ada_body,
        grid=(depth, d // tk),
        in_specs=[pl.BlockSpec((tk, bsz), lambda l, k: (k, 0)),
                  pl.BlockSpec((1, tk, n6), lambda l, k: (l, k, 0)),
                  pl.BlockSpec((1, 1, n6), lambda l, k: (l, 0, 0))],
        out_specs=pl.BlockSpec((1, bsz, n6), lambda l, k: (l, 0, 0)),
        out_shape=jax.ShapeDtypeStruct((depth, bsz, n6), F32),
        compiler_params=_cparams(("arbitrary", "arbitrary")),
        name="ada_mod",
    )(c.T, ada_w, ada_b.reshape(depth, 1, n6))


def _inproj_ab_body(x_ref, sc_ref, sh_ref, w_ref, b_ref, o_ref, xb_ref):
    @pl.when(pl.program_id(1) == 0)
    def _():
        xb_ref[...] = (x_ref[...] * (1.0 + sc_ref[0]) + sh_ref[0]).astype(BF16)

    tm = x_ref.shape[0]
    sub = min(tm, ROW_SUB)
    for r0 in range(0, tm, sub):
        rs = slice(r0, r0 + sub)
        o_ref[rs, :] = jnp.dot(xb_ref[rs, :], w_ref[...], preferred_element_type=F32) + b_ref[...]


def inproj_ab(h, sc, sh, w_bf, b, seq):
    n, d = h.shape
    nc = w_bf.shape[1]
    tm = min(1024, seq)
    tn = 1792
    per_b = seq // tm
    return pl.pallas_call(
        _inproj_ab_body,
        grid=(n // tm, nc // tn),
        in_specs=[pl.BlockSpec((tm, d), lambda i, j: (i, 0)),
                  pl.BlockSpec((1, 1, d), lambda i, j: (i // per_b, 0, 0)),
                  pl.BlockSpec((1, 1, d), lambda i, j: (i // per_b, 0, 0)),
                  pl.BlockSpec((d, tn), lambda i, j: (0, j)),
                  pl.BlockSpec((1, tn), lambda i, j: (0, j))],
        out_specs=pl.BlockSpec((tm, tn), lambda i, j: (i, j)),
        out_shape=jax.ShapeDtypeStruct((n, nc), F32),
        scratch_shapes=[pltpu.VMEM((tm, d), BF16)],
        compiler_params=_cparams(("arbitrary", "arbitrary")),
        name="inproj_ab",
    )(h, sc, sh, w_bf, b.reshape(1, nc))


def _attn_body(sink_ref, q_ref, kc_ref, vc_ref, vp_ref, pc_ref, invf_ref, o_ref, kprev_ref):
    blk = ATT_BLK
    half = A_HEAD_DIM // 2
    grp = A_HEADS // A_KV_HEADS
    i = pl.program_id(1)
    lane = lax.broadcasted_iota(I32, (blk, LANES), 1)
    first_half = (lane % A_HEAD_DIM) < half
    low_head = lane < A_HEAD_DIM
    invf = invf_ref[...]

    def cos_sin(pos_col):
        ang = pos_col.astype(F32) * invf
        s = jnp.sin(ang)
        return jnp.cos(ang), jnp.where(first_half, -s, s)

    def rope(x, cs):
        partner = jnp.where(first_half, pltpu.roll(x, LANES - half, 1), pltpu.roll(x, half, 1))
        return x * cs[0] + partner * cs[1]

    @pl.when(i == 0)
    def _():
        kprev_ref[...] = jnp.zeros_like(kprev_ref)

    cs_c = cos_sin(pc_ref[...])
    k_cur = rope(kc_ref[...], cs_c)
    k2 = jnp.concatenate([kprev_ref[...], k_cur], axis=0)
    kprev_ref[...] = k_cur
    v2 = jnp.concatenate([vp_ref[...], vc_ref[...]], axis=0)
    lane2 = lax.broadcasted_iota(I32, (2 * blk, LANES), 1)
    k2r = pltpu.roll(k2, A_HEAD_DIM, 1)
    v2r = pltpu.roll(v2, A_HEAD_DIM, 1)

    rows = lax.broadcasted_iota(I32, (grp * blk, 2 * blk), 0) % blk
    cols = lax.broadcasted_iota(I32, (grp * blk, 2 * blk), 1)
    valid = (cols > rows) & (cols <= rows + WINDOW) & ((cols >= blk) | (i > 0))

    q_chunks = [rope(q_ref[:, c * LANES:(c + 1) * LANES], cs_c) * (A_HEAD_DIM ** -0.5)
                for c in range(A_HEADS * A_HEAD_DIM // LANES)]

    for g in range(A_KV_HEADS):
        own = (lane2 < A_HEAD_DIM) if g == 0 else (lane2 >= A_HEAD_DIM)
        kg = jnp.where(own, k2, k2r).astype(BF16)
        vg = jnp.where(own, v2, v2r).astype(BF16)
        pieces = []
        sinks = []
        for r in range(grp):
            hd = g * grp + r
            qc = q_chunks[hd // 2]
            keep = low_head if hd % 2 == 0 else jnp.logical_not(low_head)
            pieces.append(jnp.where(keep, qc, 0.0).astype(BF16))
            sinks.append(jnp.full((blk, 1), sink_ref[hd], F32))
        qs = jnp.concatenate(pieces, axis=0)
        sink = jnp.concatenate(sinks, axis=0)
        s = lax.dot_general(qs, kg, (((1,), (1,)), ((), ())), preferred_element_type=F32)
        s = jnp.where(valid, s, NEG)
        m = jnp.maximum(jnp.max(s, axis=-1, keepdims=True), sink)
        p = jnp.exp(s - m)
        den = jnp.sum(p, axis=-1, keepdims=True) + jnp.exp(sink - m)
        o = jnp.dot(p.astype(BF16), vg, preferred_element_type=F32) / den
        for r2 in range(grp // 2):
            c = (g * grp) // 2 + r2
            ev = o[(2 * r2) * blk:(2 * r2 + 1) * blk]
            od = o[(2 * r2 + 1) * blk:(2 * r2 + 2) * blk]
            o_ref[:, c * LANES:(c + 1) * LANES] = jnp.where(low_head, ev, od).astype(o_ref.dtype)


def swa_attention(proj, pos_col, invf, sinks, bsz, seq):
    n = proj.shape[0]
    blk = ATT_BLK
    nb = seq // blk
    aq = A_HEADS * A_HEAD_DIM
    kcol = aq // LANES
    vcol = kcol + 1

    def cur(b, i, s):
        return b * nb + i

    def prev(b, i, s):
        return b * nb + jnp.maximum(i - 1, 0)

    grid_spec = pltpu.PrefetchScalarGridSpec(
        num_scalar_prefetch=1,
        grid=(bsz, nb),
        in_specs=[pl.BlockSpec((blk, aq), lambda b, i, s: (cur(b, i, s), 0)),
                  pl.BlockSpec((blk, LANES), lambda b, i, s: (cur(b, i, s), kcol)),
                  pl.BlockSpec((blk, LANES), lambda b, i, s: (cur(b, i, s), vcol)),
                  pl.BlockSpec((blk, LANES), lambda b, i, s: (prev(b, i, s), vcol)),
                  pl.BlockSpec((blk, 1), lambda b, i, s: (cur(b, i, s), 0)),
                  pl.BlockSpec((1, LANES), lambda b, i, s: (0, 0))],
        out_specs=pl.BlockSpec((blk, aq), lambda b, i, s: (cur(b, i, s), 0)),
        scratch_shapes=[pltpu.VMEM((blk, LANES), F32)],
    )
    return pl.pallas_call(
        _attn_body,
        grid_spec=grid_spec,
        out_shape=jax.ShapeDtypeStruct((n, aq), BF16),
        compiler_params=_cparams(("arbitrary", "arbitrary")),
        name="swa_attention",
    )(sinks, proj, proj, proj, proj, pos_col, invf)


def _hgrn_body(q_ref, f_ref, i_ref, g_ref, lb_ref, gw_ref, o_ref, st_ref):
    L = GLA_CHUNK

    @pl.when(pl.program_id(2) == 0)
    def _():
        st_ref[...] = jnp.zeros_like(st_ref)

    T = q_ref.shape[0]
    nch = T // L
    nt = (((1,), (1,)), ((), ()))
    pos = lax.broadcasted_iota(I32, (T, B_DIM), 0) % L
    r = lax.broadcasted_iota(I32, (T, T), 0)
    c = lax.broadcasted_iota(I32, (T, T), 1)
    same_chunk_causal = (r >= c) & (r // L == c // L)

    for hh in range(q_ref.shape[1] // B_DIM):
        hs = slice(hh * B_DIM, (hh + 1) * B_DIM)
        lb = lb_ref[hh]
        gw = gw_ref[hh]
        f = lb + (1.0 - lb) * jax.nn.sigmoid(f_ref[:, hs])
        kk = 1.0 - f
        qb = q_ref[:, hs]
        q = qb * jax.nn.sigmoid(qb)
        v = i_ref[:, hs]
        b = jnp.log(f)
        sh = 1
        while sh < L:
            b = b + jnp.where(pos >= sh, pltpu.roll(b, sh, 0), 0.0)
            sh *= 2

        def per_chunk(row, b=b):
            return jnp.concatenate(
                [jnp.broadcast_to(b[ci * L + row:ci * L + row + 1], (L, B_DIM)) for ci in range(nch)], axis=0)

        b_mid = per_chunk(L // 2)
        b_last = per_chunk(L - 1)
        qa = (q * jnp.exp(b - b_mid)).astype(BF16)
        ka = (kk * jnp.exp(b_mid - b)).astype(BF16)
        qe = (q * jnp.exp(b)).astype(BF16)
        kd = (kk * jnp.exp(b_last - b)).astype(BF16)
        att = lax.dot_general(qa, ka, nt, preferred_element_type=F32)
        att = jnp.where(same_chunk_causal, att, 0.0)
        o_intra = jnp.dot(att.astype(BF16), v.astype(BF16), preferred_element_type=F32)
        vt = v.T.astype(BF16)
        st = st_ref[hh]
        o_inter = []
        for ci in range(nch):
            sl = slice(ci * L, (ci + 1) * L)
            o_inter.append(lax.dot_general(qe[sl], st.astype(BF16), nt, preferred_element_type=F32))
            u_t = jnp.dot(vt[:, sl], kd[sl], preferred_element_type=F32)
            st = jnp.exp(b[ci * L + L - 1:ci * L + L]) * st + u_t
        st_ref[hh] = st
        o = o_intra + jnp.concatenate(o_inter, axis=0)
        o = o * lax.rsqrt(jnp.mean(o * o, axis=-1, keepdims=True) + RMS_EPS) * gw
        gb = g_ref[:, hs]
        o_ref[:, hs] = (o * (gb * jax.nn.sigmoid(gb))).astype(o_ref.dtype)


def hgrn2(proj, lb, gnorm_w, bsz, seq):
    n = proj.shape[0]
    t = min(HGRN_T, seq)
    nt = seq // t
    hp = HGRN_HEADS_PER_STEP
    width = hp * B_DIM
    base = (A_HEADS + 2 * A_KV_HEADS) * A_HEAD_DIM // width
    h = B_HEADS
    assert h % hp == 0 and (A_HEADS + 2 * A_KV_HEADS) * A_HEAD_DIM % width == 0

    def col(k):
        return lambda b, hd, i: (b * nt + i, base + k * (h // hp) + hd)

    return pl.pallas_call(
        _hgrn_body,
        grid=(bsz, h // hp, nt),
        in_specs=[pl.BlockSpec((t, width), col(0)),
                  pl.BlockSpec((t, width), col(1)),
                  pl.BlockSpec((t, width), col(2)),
                  pl.BlockSpec((t, width), col(3)),
                  pl.BlockSpec((hp, 1, B_DIM), lambda b, hd, i: (hd, 0, 0)),
                  pl.BlockSpec((hp, 1, B_DIM), lambda b, hd, i: (hd, 0, 0))],
        out_specs=pl.BlockSpec((t, width), lambda b, hd, i: (b * nt + i, hd)),
        out_shape=jax.ShapeDtypeStruct((n, h * B_DIM), BF16),
        scratch_shapes=[pltpu.VMEM((hp, B_DIM, B_DIM), F32)],
        compiler_params=_cparams(("arbitrary", "arbitrary", "arbitrary")),
        name="hgrn2",
    )(proj, proj, proj, proj, lb.reshape(h, 1, B_DIM), gnorm_w.reshape(h, 1, B_DIM))


def _outproj_ab_body(oa_ref, ob_ref, w_ref, b_ref, h_ref, gm_ref, lg_ref, lb_ref, o_ref):
    ka = oa_ref.shape[1]
    tm = oa_ref.shape[0]
    sub = min(tm, ROW_SUB)
    for r0 in range(0, tm, sub):
        rs = slice(r0, r0 + sub)
        y = jnp.dot(oa_ref[rs, :], w_ref[:ka, :], preferred_element_type=F32)
        y = y + jnp.dot(ob_ref[rs, :], w_ref[ka:, :], preferred_element_type=F32) + b_ref[...]
        z = DN_ALPHA * h_ref[rs, :] + (1.0 + gm_ref[0]) * y
        o_ref[rs, :] = _layer_norm(z, lg_ref[...], lb_ref[...])


def outproj_ab(oa, ob, w_bf, b, h, gm, ln_g, ln_b, seq):
    n, d = h.shape
    tm = min(512, seq)
    per_b = seq // tm
    ka, kb = oa.shape[1], ob.shape[1]
    row = lambda i: (i, 0)
    const = lambda i: (0, 0)
    return pl.pallas_call(
        _outproj_ab_body,
        grid=(n // tm,),
        in_specs=[pl.BlockSpec((tm, ka), row),
                  pl.BlockSpec((tm, kb), row),
                  _resident((ka + kb, d), const),
                  pl.BlockSpec((1, d), const),
                  pl.BlockSpec((tm, d), row),
                  pl.BlockSpec((1, 1, d), lambda i: (i // per_b, 0, 0)),
                  pl.BlockSpec((1, d), const),
                  pl.BlockSpec((1, d), const)],
        out_specs=pl.BlockSpec((tm, d), row),
        out_shape=jax.ShapeDtypeStruct((n, d), F32),
        compiler_params=_cparams(("arbitrary",)),
        name="outproj_ab",
    )(oa, ob, w_bf, b.reshape(1, d), h, gm, ln_g.reshape(1, d), ln_b.reshape(1, d))


def _router_body(h_ref, sc_ref, sh_ref, rw_ref, rb_ref, meta_ref, cnt_ref, carry_ref, rwh_ref, rwl_ref):
    tm = h_ref.shape[0]

    @pl.when(pl.program_id(0) == 0)
    def _():
        carry_ref[...] = jnp.zeros_like(carry_ref)
        w = rw_ref[...]
        w_hi = w.astype(BF16)
        rwh_ref[...] = w_hi
        rwl_ref[...] = (w - w_hi.astype(F32)).astype(BF16)

    xin = h_ref[...] * (1.0 + sc_ref[0]) + sh_ref[0]
    x_hi = xin.astype(BF16)
    x_lo = (xin - x_hi.astype(F32)).astype(BF16)
    logits = (jnp.dot(x_hi, rwh_ref[...], preferred_element_type=F32)
              + (jnp.dot(x_hi, rwl_ref[...], preferred_element_type=F32)
                 + jnp.dot(x_lo, rwh_ref[...], preferred_element_type=F32))) + rb_ref[...]
    lane = lax.broadcasted_iota(I32, (tm, LANES), 1)
    l = logits
    vals, idxs = [], []
    sel = jnp.zeros((tm, LANES), F32)
    for _ in range(TOP_K):
        m = jnp.max(l, axis=-1, keepdims=True)
        idx = jnp.min(jnp.where(l == m, lane, LANES), axis=-1, keepdims=True)
        hit = lane == idx
        vals.append(m)
        idxs.append(idx)
        sel = jnp.where(hit, 1.0, sel)
        l = jnp.where(hit, -jnp.inf, l)
    es = [jnp.exp(v - vals[0]) for v in vals]
    den = es[0] + es[1] + es[2] + es[3]
    r = lax.broadcasted_iota(I32, (tm, tm), 0)
    c = lax.broadcasted_iota(I32, (tm, tm), 1)
    tril = (r >= c).astype(BF16)
    incl = jnp.dot(tril, sel.astype(BF16), preferred_element_type=F32)
    carry = carry_ref[...]
    excl = incl - sel + carry
    meta = jnp.zeros((tm, LANES), F32)
    for k in range(TOP_K):
        rank = jnp.sum(jnp.where(lane == idxs[k], excl, 0.0), axis=-1, keepdims=True)
        meta = jnp.where(lane == k, idxs[k].astype(F32), meta)
        meta = jnp.where(lane == TOP_K + k, es[k] / den, meta)
        meta = jnp.where(lane == 2 * TOP_K + k, rank, meta)
    meta_ref[...] = meta
    carry = carry + incl[tm - 1:tm, :]
    carry_ref[...] = carry
    cnt_ref[...] = jnp.broadcast_to(carry, cnt_ref.shape)


def moe_router(h, sc, sh, rw_pad, rb_pad, seq):
    n, d = h.shape
    tm = min(512, seq)
    per_b = seq // tm
    return pl.pallas_call(
        _router_body,
        grid=(n // tm,),
        in_specs=[pl.BlockSpec((tm, d), lambda i: (i, 0)),
                  pl.BlockSpec((1, 1, d), lambda i: (i // per_b, 0, 0)),
                  pl.BlockSpec((1, 1, d), lambda i: (i // per_b, 0, 0)),
                  pl.BlockSpec((d, LANES), lambda i: (0, 0)),
                  pl.BlockSpec((1, LANES), lambda i: (0, 0))],
        out_specs=[pl.BlockSpec((tm, LANES), lambda i: (i, 0)),
                   pl.BlockSpec((8, LANES), lambda i: (0, 0))],
        out_shape=[jax.ShapeDtypeStruct((n, LANES), F32),
                   jax.ShapeDtypeStruct((8, LANES), F32)],
        scratch_shapes=[pltpu.VMEM((1, LANES), F32), pltpu.VMEM((d, LANES), BF16), pltpu.VMEM((d, LANES), BF16)],
        compiler_params=_cparams(("arbitrary",)),
        name="moe_router",
    )(h, sc, sh, rw_pad, rb_pad)


def _dispatch_body(pad_ref, dest_ref, h_ref, sc_ref, sh_ref, xs_hbm, buf_ref, zero_ref, sem, zsem):
    tm, d = h_ref.shape
    half = d // 2
    pitch = _packed_pitch(d)
    i = pl.program_id(0)
    last = pl.num_programs(0) - 1
    slot = i % 2
    n_pad = pad_ref.shape[0]

    def wait_slot(s):
        for _ in range(TOP_K):
            pltpu.make_async_copy(buf_ref.at[s], xs_hbm.at[pl.ds(0, tm * pitch)], sem.at[s]).wait()

    @pl.when(i == 0)
    def _():
        zero_ref[...] = jnp.zeros_like(zero_ref)
        buf_ref[...] = jnp.zeros_like(buf_ref)

        def zfill(q2, carry):
            for pr in range(2):
                dst = pad_ref[q2 * 2 + pr] * pitch
                pltpu.make_async_copy(zero_ref, xs_hbm.at[pl.ds(dst, pitch)], zsem.at[0]).start(priority=pr)
            return carry

        lax.fori_loop(0, n_pad // 2, zfill, 0)

    @pl.when(i >= 2)
    def _():
        wait_slot(slot)

    xin = h_ref[...] * (1.0 + sc_ref[0]) + sh_ref[0]
    lo = lax.bitcast_convert_type(xin[:, :half].astype(BF16).astype(F32), U32)
    hi = lax.bitcast_convert_type(xin[:, half:].astype(BF16).astype(F32), U32)
    word = lax.shift_right_logical(lo, jnp.uint32(16)) | (hi & jnp.uint32(0xFFFF0000))
    for s in range(half // LANES):
        buf_ref[slot, pl.ds(s, tm, stride=pitch), :] = word[:, s * LANES:(s + 1) * LANES]

    def issue(g, carry):
        for rr in range(DMA_GROUP):
            r = g * DMA_GROUP + rr
            src = buf_ref.at[slot, pl.ds(r * pitch, pitch)]
            for k in range(TOP_K):
                dst = dest_ref[0, 0, r * TOP_K + k] * pitch
                pltpu.make_async_copy(src, xs_hbm.at[pl.ds(dst, pitch)], sem.at[slot]).start(priority=k % 2)
        return carry

    lax.fori_loop(0, tm // DMA_GROUP, issue, 0)

    @pl.when(i == last)
    def _():
        wait_slot(slot)

        @pl.when(i >= 1)
        def _():
            wait_slot(1 - slot)

        for _ in range(n_pad // tm):
            pltpu.make_async_copy(buf_ref.at[0], xs_hbm.at[pl.ds(0, tm * pitch)], zsem.at[0]).wait()


def moe_dispatch(h, sc, sh, dest, pad_dest, n_rows, seq):
    n, d = h.shape
    tm = min(256, seq)
    per_b = seq // tm
    pitch = _packed_pitch(d)
    assert pad_dest.shape[0] % tm == 0
    grid_spec = pltpu.PrefetchScalarGridSpec(
        num_scalar_prefetch=1,
        grid=(n // tm,),
        in_specs=[pl.BlockSpec((1, 1, tm * TOP_K), lambda i, p: (i, 0, 0), memory_space=pltpu.SMEM),
                  pl.BlockSpec((tm, d), lambda i, p: (i, 0)),
                  pl.BlockSpec((1, 1, d), lambda i, p: (i // per_b, 0, 0)),
                  pl.BlockSpec((1, 1, d), lambda i, p: (i // per_b, 0, 0))],
        out_specs=pl.BlockSpec(memory_space=pl.ANY),
        scratch_shapes=[pltpu.VMEM((2, tm * pitch, LANES), U32), pltpu.VMEM((pitch, LANES), U32),
                        pltpu.SemaphoreType.DMA((2,)), pltpu.SemaphoreType.DMA((1,))],
    )
    return pl.pallas_call(
        _dispatch_body,
        grid_spec=grid_spec,
        out_shape=jax.ShapeDtypeStruct((n_rows * pitch, LANES), U32),
        compiler_params=_cparams(("arbitrary",)),
        name="moe_dispatch",
    )(pad_dest, dest.reshape(n // tm, 1, tm * TOP_K), h, sc, sh)


SCHED_MODE, SCHED_BLOCK, SCHED_EXPERT, SCHED_SLOT, SCHED_DL_EXPERT, SCHED_DL_C1, SCHED_DL_C2, SCHED_DL_SLOT = range(8)
MODE_IDLE, MODE_COMPUTE, MODE_ZERO = 0, 1, 2


def _expert_body(sched_ref, xs_ref, w1c_ref, w2c_ref, b1_ref, b2_ref, y_ref, wb1a_ref, wb1b_ref, wb2a_ref, wb2b_ref):
    j = pl.program_id(0)
    nstep = pl.num_programs(0)
    d = wb1a_ref.shape[0]
    ff = wb2a_ref.shape[0]
    rt = d // LANES
    pitch = _row_pitch(d)
    xpitch = _packed_pitch(d)
    blk = xs_ref.shape[0] // xpitch
    cache = ((wb1a_ref, wb2a_ref), (wb1b_ref, wb2b_ref))

    def sched(field):
        return sched_ref[field * nstep + j]

    def cast_chunks(wb1_ref, wb2_ref):
        r1 = pl.multiple_of(sched(SCHED_DL_C1) * W_CHUNK, W_CHUNK)
        r2 = pl.multiple_of(sched(SCHED_DL_C2) * W_CHUNK, W_CHUNK)
        wb1_ref[pl.ds(r1, W_CHUNK), :] = w1c_ref[0].astype(BF16)
        wb2_ref[pl.ds(r2, W_CHUNK), :] = w2c_ref[0].astype(BF16)

    def compute(wb1_ref, wb2_ref):
        sub = blk // EXPERT_SUB_BLOCKS
        for r0 in range(0, blk, sub):
            words = [xs_ref[pl.ds(r0 * xpitch + s, sub, stride=xpitch), :] for s in range(rt // 2)]
            lo = [lax.bitcast_convert_type(lax.shift_left(w, jnp.uint32(16)), F32).astype(BF16) for w in words]
            hi = [lax.bitcast_convert_type(w & jnp.uint32(0xFFFF0000), F32).astype(BF16) for w in words]
            x = jnp.concatenate(lo + hi, axis=1)
            hid = jnp.dot(x, wb1_ref[...], preferred_element_type=F32) + b1_ref[0]
            gate = jnp.minimum(hid[:, :ff], SWIGLU_LIMIT)
            lin = jnp.clip(hid[:, ff:], -SWIGLU_LIMIT, SWIGLU_LIMIT)
            act = gate * jax.nn.sigmoid(SWIGLU_ALPHA * gate) * (lin + 1.0)
            y = jnp.dot(act.astype(BF16), wb2_ref[...], preferred_element_type=F32) + b2_ref[0]
            for s in range(rt):
                y_ref[pl.ds(r0 * pitch + s, sub, stride=pitch), :] = y[:, s * LANES:(s + 1) * LANES]
            for s in range(rt, pitch):
                y_ref[pl.ds(r0 * pitch + s, sub, stride=pitch), :] = jnp.zeros((sub, LANES), F32)

    mode = sched(SCHED_MODE)
    slot = sched(SCHED_SLOT)
    dl_slot = sched(SCHED_DL_SLOT)
    for par in range(2):
        @pl.when((mode == MODE_COMPUTE) & (slot == par))
        def _(par=par):
            cast_chunks(*cache[1 - par])
            compute(*cache[par])

        @pl.when((mode != MODE_COMPUTE) & (dl_slot == par))
        def _(par=par):
            cast_chunks(*cache[par])

    @pl.when(mode == MODE_ZERO)
    def _():
        y_ref[...] = jnp.zeros_like(y_ref)


def expert_schedule(padded, n_used, nblk, layer, ne, need, need2):
    nb = padded // MOE_BLK
    seg_len = jnp.concatenate([jnp.full((1,), need, I32), jnp.maximum(nb[:-1], need), nb[-1:]])
    seg_end = jnp.cumsum(seg_len)
    seg_start = seg_end - seg_len
    nstep = nblk + (ne + 1) * need
    j = jnp.arange(nstep, dtype=I32)
    seg = jnp.sum(seg_end[None, :] <= j[:, None], axis=-1).astype(I32)
    in_seg = seg <= ne
    segc = jnp.minimum(seg, ne)
    loc = j - seg_start[segc]
    expert = jnp.clip(segc - 1, 0, ne - 1)
    compute = in_seg & (segc >= 1) & (loc < nb[expert])
    done = jnp.cumsum(compute.astype(I32)) - compute.astype(I32)
    tail = j - seg_end[-1]
    ntail = nblk - n_used
    zero = jnp.logical_not(in_seg) & (tail < ntail)
    parked = jnp.where(jnp.logical_not(in_seg) & (ntail > 0), nblk - 1, jnp.maximum(done - 1, 0))
    block = jnp.where(compute, done, jnp.where(zero, n_used + tail, parked))
    mode = jnp.where(compute, MODE_COMPUTE, jnp.where(zero, MODE_ZERO, MODE_IDLE))
    deliver = in_seg & (segc <= ne - 1) & (loc < need)
    last = lax.cummax(jnp.where(deliver, j, -1), axis=0)
    dl_expert = segc[last]
    dl_c1 = loc[last]
    dl_c2 = jnp.minimum(loc[last], need2 - 1)
    table = jnp.stack([mode, block, expert + layer * ne, expert % 2,
                       dl_expert + layer * ne, dl_c1, dl_c2, dl_expert % 2])
    return table.astype(I32).reshape(-1), nstep


def moe_experts(xs, sched, nstep, w1, b1, w2, b2):
    ne, d, ff2 = w1.shape
    pitch = _row_pitch(d)
    xpitch = _packed_pitch(d)
    n_rows = xs.shape[0] // xpitch
    ff = ff2 // 2

    def fld(s, j, field):
        return s[field * nstep + j]

    grid_spec = pltpu.PrefetchScalarGridSpec(
        num_scalar_prefetch=1,
        grid=(nstep,),
        in_specs=[pl.BlockSpec((MOE_BLK * xpitch, LANES), lambda j, s: (fld(s, j, SCHED_BLOCK), 0)),
                  pl.BlockSpec((1, W_CHUNK, ff2),
                               lambda j, s: (fld(s, j, SCHED_DL_EXPERT), fld(s, j, SCHED_DL_C1), 0)),
                  pl.BlockSpec((1, W_CHUNK, d),
                               lambda j, s: (fld(s, j, SCHED_DL_EXPERT), fld(s, j, SCHED_DL_C2), 0)),
                  pl.BlockSpec((1, 1, ff2), lambda j, s: (fld(s, j, SCHED_EXPERT), 0, 0)),
                  pl.BlockSpec((1, 1, d), lambda j, s: (fld(s, j, SCHED_EXPERT), 0, 0))],
        out_specs=pl.BlockSpec((MOE_BLK * pitch, LANES), lambda j, s: (fld(s, j, SCHED_BLOCK), 0)),
        scratch_shapes=[pltpu.VMEM((d, ff2), BF16), pltpu.VMEM((d, ff2), BF16),
                        pltpu.VMEM((ff, d), BF16), pltpu.VMEM((ff, d), BF16)],
    )
    return pl.pallas_call(
        _expert_body,
        grid_spec=grid_spec,
        out_shape=jax.ShapeDtypeStruct((n_rows * pitch, LANES), F32),
        compiler_params=_cparams(("arbitrary",)),
        name="moe_experts",
    )(sched, xs, w1, w2, b1.reshape(ne, 1, ff2), b2.reshape(ne, 1, d))


def _combine_body(dc_ref, dn_ref, meta_ref, h_ref, gf_ref, lg_ref, lb_ref, y_hbm, o_ref, buf_ref, sem):
    tm, d = h_ref.shape
    rt = d // LANES
    pitch = _row_pitch(d)
    grp = COMBINE_GROUP
    i = pl.program_id(0)
    n = pl.num_programs(0)
    slot = i % 2

    def issue_rows(dref, s, r0):
        for rr in range(grp):
            r = r0 + rr
            for k in range(TOP_K):
                src = dref[0, 0, r * TOP_K + k] * pitch
                pltpu.make_async_copy(y_hbm.at[pl.ds(src, rt)], buf_ref.at[s, k, pl.ds(r * pitch, rt)],
                                      sem.at[s]).start(priority=k % 2)

    @pl.when(i == 0)
    def _():
        def body(g, carry):
            issue_rows(dc_ref, 0, g * grp)
            return carry
        lax.fori_loop(0, tm // grp, body, 0)

    for k in range(TOP_K):
        pltpu.make_async_copy(y_hbm.at[pl.ds(0, tm * rt)], buf_ref.at[slot, k, pl.ds(0, tm * rt)],
                              sem.at[slot]).wait()

    def group(g, carry, prefetch):
        r0 = pl.multiple_of(g * grp, grp)
        if prefetch:
            issue_rows(dn_ref, 1 - slot, r0)
        meta = meta_ref[pl.ds(r0, grp), :]
        gates = [jnp.broadcast_to(meta[:, TOP_K + k:TOP_K + k + 1], (grp, LANES)) for k in range(TOP_K)]
        for s in range(rt):
            a = gates[0] * buf_ref[slot, 0, pl.ds(r0 * pitch + s, grp, stride=pitch), :]
            for k in range(1, TOP_K):
                a = a + gates[k] * buf_ref[slot, k, pl.ds(r0 * pitch + s, grp, stride=pitch), :]
            cs = slice(s * LANES, (s + 1) * LANES)
            o_ref[pl.ds(r0, grp), cs] = DN_ALPHA * h_ref[pl.ds(r0, grp), cs] + (1.0 + gf_ref[0, :, cs]) * a
        return carry

    @pl.when(i + 1 < n)
    def _():
        lax.fori_loop(0, tm // grp, functools.partial(group, prefetch=True), 0)

    @pl.when(i + 1 == n)
    def _():
        lax.fori_loop(0, tm // grp, functools.partial(group, prefetch=False), 0)

    o_ref[...] = _layer_norm(o_ref[...], lg_ref[...], lb_ref[...])


def moe_combine(y, dest, meta, h, gf, ln_g, ln_b, seq):
    n, d = h.shape
    tm = min(256, seq)
    per_b = seq // tm
    nblk = n // tm
    dest3 = dest.reshape(nblk, 1, tm * TOP_K)
    row = lambda i: (i, 0)
    const = lambda i: (0, 0)
    return pl.pallas_call(
        _combine_body,
        grid=(nblk,),
        in_specs=[pl.BlockSpec((1, 1, tm * TOP_K), lambda i: (i, 0, 0), memory_space=pltpu.SMEM),
                  pl.BlockSpec((1, 1, tm * TOP_K), lambda i: (jnp.minimum(i + 1, nblk - 1), 0, 0),
                               memory_space=pltpu.SMEM),
                  pl.BlockSpec((tm, LANES), row),
                  pl.BlockSpec((tm, d), row),
                  pl.BlockSpec((1, 1, d), lambda i: (i // per_b, 0, 0)),
                  pl.BlockSpec((1, d), const),
                  pl.BlockSpec((1, d), const),
                  pl.BlockSpec(memory_space=pl.ANY)],
        out_specs=pl.BlockSpec((tm, d), row),
        out_shape=jax.ShapeDtypeStruct((n, d), F32),
        scratch_shapes=[pltpu.VMEM((2, TOP_K, tm * _row_pitch(d), LANES), F32), pltpu.SemaphoreType.DMA((2,))],
        compiler_params=_cparams(("arbitrary",)),
        name="moe_combine",
    )(dest3, dest3, meta, h, gf, ln_g.reshape(1, d), ln_b.reshape(1, d), y)


def moe_layer(h, sc, sh, gf, ln_g, ln_b, router_w, router_b, layer, w1, b1, w2, b2, seq):
    n, d = h.shape
    ne = router_w.shape[1]
    rw_pad = jnp.pad(router_w, ((0, 0), (0, LANES - ne)))
    rb_pad = jnp.pad(router_b, (0, LANES - ne), constant_values=NEG).reshape(1, LANES)
    meta, cnt = moe_router(h, sc, sh, rw_pad, rb_pad, seq)
    idx4 = meta[:, 0:TOP_K].astype(I32)
    rank4 = meta[:, 2 * TOP_K:3 * TOP_K].astype(I32)
    counts = cnt[0, :ne].astype(I32)
    padded = (counts + MOE_BLK - 1) // MOE_BLK * MOE_BLK
    end = jnp.cumsum(padded)
    start = end - padded
    onehot = idx4[:, :, None] == jnp.arange(ne, dtype=I32)[None, None, :]
    dest = jnp.sum(jnp.where(onehot, start[None, None, :], 0), axis=-1) + rank4
    n_rows = -(-(n * TOP_K + ne * (MOE_BLK - 1)) // MOE_BLK) * MOE_BLK
    nblk = n_rows // MOE_BLK
    blk0 = jnp.arange(nblk, dtype=I32) * MOE_BLK
    n_used = (end[-1] // MOE_BLK).astype(I32)
    n_pad = n_rows - n * TOP_K
    seg_cnt = jnp.concatenate([padded - counts, (n_rows - end[-1])[None]])
    seg_base = jnp.concatenate([start + counts, end[-1:]])
    seg_end = jnp.cumsum(seg_cnt)
    q = jnp.arange(n_pad, dtype=I32)
    seg = jnp.sum(seg_end[None, :] <= q[:, None], axis=-1)
    pad_dest = (seg_base[seg] + q - (seg_end - seg_cnt)[seg]).astype(I32)
    xs = moe_dispatch(h, sc, sh, dest.reshape(-1), pad_dest, n_rows, seq)
    sched, nstep = expert_schedule(padded, n_used, nblk, layer, ne, w1.shape[1] // W_CHUNK, w2.shape[1] // W_CHUNK)
    y = moe_experts(xs, sched, nstep, w1, b1, w2, b2)
    return moe_combine(y, dest.reshape(-1), meta, h, gf, ln_g, ln_b, seq)


def _inproj_c_body(x_ref, sc_ref, sh_ref, w_ref, o_ref, acc_ref):
    tm = x_ref.shape[0]
    ncol = w_ref.shape[1] // LANES
    sub = min(tm, ROW_SUB)
    for r0 in range(0, tm, sub):
        rs = slice(r0, r0 + sub)
        xb = (x_ref[rs, :] * (1.0 + sc_ref[0]) + sh_ref[0]).astype(BF16)
        acc = jnp.dot(xb, w_ref[...], preferred_element_type=F32)
        for c in range(ncol):
            acc_ref[c, rs, :] = acc[:, c * LANES:(c + 1) * LANES]
        ch = slice(r0 // SSM_T, (r0 + sub) // SSM_T)
        for t in range(SSM_T):
            for c in range(ncol):
                o_ref[t, ch, c * LANES:(c + 1) * LANES] = acc_ref[c, pl.ds(r0 + t, sub // SSM_T, stride=SSM_T), :]


def inproj_c(h, sc, sh, w_bf, seq):
    n, d = h.shape
    width = w_bf.shape[1]
    tm = min(1024, seq)
    per_b = seq // tm
    return pl.pallas_call(
        _inproj_c_body,
        grid=(n // tm,),
        in_specs=[pl.BlockSpec((tm, d), lambda i: (i, 0)),
                  pl.BlockSpec((1, 1, d), lambda i: (i // per_b, 0, 0)),
                  pl.BlockSpec((1, 1, d), lambda i: (i // per_b, 0, 0)),
                  _resident((d, width), lambda i: (0, 0))],
        out_specs=pl.BlockSpec((SSM_T, tm // SSM_T, width), lambda i: (0, i, 0)),
        out_shape=jax.ShapeDtypeStruct((SSM_T, n // SSM_T, width), F32),
        scratch_shapes=[pltpu.VMEM((width // LANES, tm, LANES), F32)],
        compiler_params=_cparams(("arbitrary",)),
        name="inproj_c",
    )(h, sc, sh, w_bf)


def _s5_body(u_ref, wt_ref, wa_ref, wc_ref, lam_ref, d_ref, o_ref, st_ref, v_ref, xs_ref):
    nb, cb = u_ref.shape[1], u_ref.shape[2]
    ns = st_ref.shape[1] // 2
    kdim = SSM_T * LANES

    @pl.when(pl.program_id(1) == 0)
    def _():
        st_ref[...] = jnp.zeros_like(st_ref)

    xcat = jnp.concatenate(
        [jnp.concatenate([u_ref[t, b].astype(BF16) for t in range(SSM_T)], axis=1) for b in range(nb)], axis=0)
    v = jnp.dot(xcat, wa_ref[0], preferred_element_type=F32)
    nslab = 2 * ns // LANES
    for c in range(nslab):
        v_ref[c] = v[:, c * LANES:(c + 1) * LANES]
    lam = lam_ref[0]
    a_re, a_im = lam[:, :ns], lam[:, ns:]

    def step(r, st):
        for c in range(nslab):
            xs_ref[c, pl.ds(r, nb, stride=cb), :] = st[:, c * LANES:(c + 1) * LANES]
        v = jnp.concatenate([v_ref[c, pl.ds(r, nb, stride=cb), :] for c in range(nslab)], axis=1)
        s_re, s_im = st[:, :ns], st[:, ns:]
        n_re = a_re * s_re - a_im * s_im + v[:, :ns]
        n_im = a_re * s_im + a_im * s_re + v[:, ns:]
        return jnp.concatenate([n_re, n_im], axis=1)

    st_ref[...] = lax.fori_loop(0, cb, step, st_ref[...])
    xs_bf = jnp.concatenate([xs_ref[c].astype(BF16) for c in range(nslab)], axis=1)
    dvec = d_ref[0]
    for q in range(S5_COL_BLOCKS):
        c0, c1 = q * kdim // S5_COL_BLOCKS, (q + 1) * kdim // S5_COL_BLOCKS
        y = jnp.dot(xcat[:, :c1], wt_ref[0, :c1, c0:c1], preferred_element_type=F32)
        y = y + jnp.dot(xs_bf, wc_ref[0, :, c0:c1], preferred_element_type=F32)
        for tt in range((c1 - c0) // LANES):
            t = c0 // LANES + tt
            for b in range(nb):
                yt = y[b * cb:(b + 1) * cb, tt * LANES:(tt + 1) * LANES] + dvec * u_ref[t, b]
                gel = 0.5 * yt * (1.0 + jnp.tanh(0.7978845608028654 * (yt + 0.044715 * yt * yt * yt)))
                o_ref[t, b] = gel.astype(o_ref.dtype)


def s5_mixer(u2, wt, wa, wc, lam_t, dpack, bsz, seq):
    _, nchunk, width = u2.shape
    npack = width // LANES
    per_b = seq // SSM_T
    cb = min(256, per_b)
    nblk = per_b // cb
    ns2 = wa.shape[2]
    kdim = SSM_T * LANES
    pk = lambda j, i: (j, 0, 0)
    u4 = u2.reshape(SSM_T, bsz, per_b, width)
    y4 = pl.pallas_call(
        _s5_body,
        grid=(npack, nblk),
        in_specs=[pl.BlockSpec((SSM_T, bsz, cb, LANES), lambda j, i: (0, 0, i, j)),
                  _resident((1, kdim, kdim), pk),
                  _resident((1, kdim, ns2), pk),
                  _resident((1, ns2, kdim), pk),
                  pl.BlockSpec((1, 1, ns2), pk),
                  pl.BlockSpec((1, 1, LANES), pk)],
        out_specs=pl.BlockSpec((SSM_T, bsz, cb, LANES), lambda j, i: (0, 0, i, j)),
        out_shape=jax.ShapeDtypeStruct((SSM_T, bsz, per_b, width), F32),
        scratch_shapes=[pltpu.VMEM((bsz, ns2), F32), pltpu.VMEM((ns2 // LANES, bsz * cb, LANES), F32),
                        pltpu.VMEM((ns2 // LANES, bsz * cb, LANES), F32)],
        compiler_params=_cparams(("arbitrary", "arbitrary")),
        name="s5_mixer",
    )(u4, wt, wa, wc, lam_t, dpack)
    return y4.reshape(SSM_T, nchunk, width)


def s5_operators(a_re, a_im, log_dt, b_re, b_im, c_re, c_im, dskip):
    g, p = a_re.shape
    cg = b_re.shape[2]
    npack = g // SSM_PACK
    ns2 = SSM_PACK * 2 * p
    t = SSM_T
    lam = lax.complex(a_re, a_im)
    dt = jnp.exp(log_dt)[:, None]
    lam_bar = jnp.exp(lam * dt)
    b_bar = ((lam_bar - 1.0) / lam)[..., None] * lax.complex(b_re, b_im)
    c_mat = lax.complex(c_re, c_im)
    kpow = jnp.arange(t + 1, dtype=F32)[:, None, None]
    pows = jnp.exp((lam * dt)[None] * kpow)
    lane_c = jnp.arange(LANES) % cg
    lane_g = jnp.arange(LANES) // cg
    tile = (lane_c[None, :] == jnp.arange(cg)[:, None]).astype(F32)
    diag = lane_g[:, None] == lane_g[None, :]
    kern = jnp.real(jnp.einsum('gcp,kgp,gpd->kgdc', c_mat, pows[:t], b_bar, precision=HIGHEST))
    kfull = jnp.where(diag, jnp.dot(kern.reshape(t, npack, LANES, cg), tile, precision=HIGHEST), 0.0)
    kfull = kfull.astype(BF16)
    kcat = jnp.concatenate([kfull[tau] for tau in range(t)], axis=-1)
    wt = jnp.stack([jnp.pad(kcat[:, :, :(t - s) * LANES], ((0, 0), (0, 0), (s * LANES, 0)))
                    for s in range(t)], axis=1)
    wt = wt.reshape(npack, t * LANES, t * LANES)
    ns = SSM_PACK * p
    col_g = jnp.arange(ns) // p
    tile_p = (jnp.arange(ns)[None, :] % p == jnp.arange(p)[:, None]).astype(F32)
    wa_c = pows[t - 1 - jnp.arange(t)][:, :, :, None] * b_bar[None]
    wa_c = wa_c.reshape(t, npack, SSM_PACK, p, cg).transpose(1, 0, 2, 4, 3)
    wa_c = wa_c.reshape(npack, t * LANES, p)
    own_a = jnp.tile(lane_g, t)[:, None] == col_g[None, :]
    zero_bf = jnp.zeros((), BF16)
    wa = jnp.concatenate([jnp.where(own_a, jnp.dot(part.astype(BF16), tile_p.astype(BF16),
                                                  preferred_element_type=BF16), zero_bf)
                          for part in (jnp.real(wa_c), jnp.imag(wa_c))], axis=-1)
    wc_c = c_mat[None] * pows[1:t + 1][:, :, None, :]
    wc_c = wc_c.transpose(0, 1, 3, 2).reshape(t, npack, ns, cg)
    cc = jnp.concatenate([jnp.real(wc_c), -jnp.imag(wc_c)], axis=2)
    own_c = (jnp.arange(ns2) % ns // p)[:, None] == lane_g[None, :]
    wcf = jnp.where(own_c, jnp.dot(cc.astype(BF16), tile.astype(BF16), preferred_element_type=BF16),
                    zero_bf)
    wc = jnp.concatenate([wcf[tt] for tt in range(t)], axis=-1)
    lam_t = pows[t].reshape(npack, 1, ns)
    lam_t = jnp.concatenate([jnp.real(lam_t), jnp.imag(lam_t)], axis=-1)
    dpack = dskip.reshape(npack, 1, LANES)
    return wt, wa, wc, lam_t, dpack


def _glu_out_body(y_ref, wg_ref, bg_ref, wo_ref, h_ref, gm_ref, lg_ref, lb_ref, o_ref, ytm_ref):
    cb = y_ref.shape[1]
    ncol = y_ref.shape[2] // LANES
    for t in range(SSM_T):
        for c in range(ncol):
            ytm_ref[c, pl.ds(t, cb, stride=SSM_T), :] = y_ref[t, :, c * LANES:(c + 1) * LANES]
    tm = h_ref.shape[0]
    sub = min(tm, ROW_SUB)
    for r0 in range(0, tm, sub):
        rs = slice(r0, r0 + sub)
        y = jnp.concatenate([ytm_ref[c, rs, :] for c in range(ncol)], axis=1)
        gate = jnp.dot(y.astype(BF16), wg_ref[...], preferred_element_type=F32) + bg_ref[...]
        z = (y * jax.nn.sigmoid(gate)).astype(BF16)
        out = jnp.dot(z, wo_ref[...], preferred_element_type=F32)
        zz = DN_ALPHA * h_ref[rs, :] + (1.0 + gm_ref[0]) * out
        o_ref[rs, :] = _layer_norm(zz, lg_ref[...], lb_ref[...])


def glu_out(y2, wg_bf, bg, wo_bf, h, gm, ln_g, ln_b, seq):
    n, d = h.shape
    width = y2.shape[2]
    tm = min(512, seq)
    cb = tm // SSM_T
    per_b = seq // tm
    row = lambda i: (i, 0)
    const = lambda i: (0, 0)
    return pl.pallas_call(
        _glu_out_body,
        grid=(n // tm,),
        in_specs=[pl.BlockSpec((SSM_T, cb, width), lambda i: (0, i, 0)),
                  _resident((width, width), const),
                  pl.BlockSpec((1, width), const),
                  _resident((width, d), const),
                  pl.BlockSpec((tm, d), row),
                  pl.BlockSpec((1, 1, d), lambda i: (i // per_b, 0, 0)),
                  pl.BlockSpec((1, d), const),
                  pl.BlockSpec((1, d), const)],
        out_specs=pl.BlockSpec((tm, d), row),
        out_shape=jax.ShapeDtypeStruct((n, d), F32),
        scratch_shapes=[pltpu.VMEM((width // LANES, tm, LANES), F32)],
        compiler_params=_cparams(("arbitrary",)),
        name="glu_out",
    )(y2, wg_bf, bg.reshape(1, width), wo_bf, h, gm, ln_g.reshape(1, d), ln_b.reshape(1, d))


def kernel(x, c, positions, ada_w, ada_b, ln_g, ln_b, ab_in_w, ab_in_b, ab_sinks, ab_gnorm_w, ab_out_w, ab_out_b, hgrn_lb_logits, c_in_w, c_A_re, c_A_im, c_log_dt, c_B_re, c_B_im, c_C_re, c_C_im, c_D, c_glu_w, c_glu_b, c_out_w, router_w, router_b, exp_w1, exp_b1, exp_w2, exp_b2):
    bsz, seq, d = x.shape
    n = bsz * seq
    depth = ada_w.shape[0]
    mod = ada_mod(c, ada_w, ada_b)
    lb_all = jnp.cumsum(jax.nn.softmax(hgrn_lb_logits.astype(F32), axis=0), axis=0)
    half = A_HEAD_DIM // 2
    inv_freq = ROPE_THETA ** (-jnp.arange(half, dtype=F32) / half)
    invf = jnp.tile(inv_freq, LANES // half).reshape(1, LANES)
    pos_col = positions.reshape(n, 1)
    ne_all = depth * exp_w1.shape[1]
    w1_all = exp_w1.reshape(ne_all, d, exp_w1.shape[3])
    w2_all = exp_w2.reshape(ne_all, exp_w2.shape[2], d)
    b1_all = exp_b1.reshape(ne_all, exp_b1.shape[2])
    b2_all = exp_b2.reshape(ne_all, d)
    h = x.reshape(n, d)
    for layer in range(depth):
        m6 = mod[layer].reshape(bsz, 6, 1, d)
        sh_m, sc_m, g_m, sh_f, sc_f, g_f = [m6[:, k] for k in range(6)]
        i = layer // 2
        if layer % 2 == 0:
            proj = inproj_ab(h, sc_m, sh_m, ab_in_w[i].astype(BF16), ab_in_b[i], seq)
            oa = swa_attention(proj, pos_col, invf, ab_sinks[i], bsz, seq)
            ob = hgrn2(proj, lb_all[layer], ab_gnorm_w[i], bsz, seq)
            h = outproj_ab(oa, ob, ab_out_w[i].astype(BF16), ab_out_b[i], h, g_m,
                           ln_g[layer, 0], ln_b[layer, 0], seq)
        else:
            wt, wa, wc, lam_t, dpack = s5_operators(c_A_re[i], c_A_im[i], c_log_dt[i], c_B_re[i], c_B_im[i],
                                                    c_C_re[i], c_C_im[i], c_D[i])
            u2 = inproj_c(h, sc_m, sh_m, c_in_w[i].astype(BF16), seq)
            y2 = s5_mixer(u2, wt, wa, wc, lam_t, dpack, bsz, seq)
            h = glu_out(y2, c_glu_w[i].astype(BF16), c_glu_b[i], c_out_w[i].astype(BF16), h, g_m,
                        ln_g[layer, 0], ln_b[layer, 0], seq)
        h = moe_layer(h, sc_f, sh_f, g_f, ln_g[layer, 1], ln_b[layer, 1], router_w[layer], router_b[layer],
                      layer, w1_all, b1_all, w2_all, b2_all, seq)
    return h.reshape(bsz, seq, d)
```

```python
import functools

import jax
import jax.numpy as jnp
from jax import lax
from jax.experimental import pallas as pl
from jax.experimental.pallas import tpu as pltpu

F32 = jnp.float32
BF16 = jnp.bfloat16
I32 = jnp.int32
U32 = jnp.uint32
HIGHEST = lax.Precision.HIGHEST

DEPTH = 2
A_HEADS = 16
A_KV_HEADS = 2
A_HEAD_DIM = 64
WINDOW = 128
ROPE_THETA = 10000.0
B_HEADS = 8
B_DIM = 128
GLA_CHUNK = 64
C_GROUP = 16
C_STATE = 64
N_EXPERTS = 32
TOP_K = 4
SWIGLU_LIMIT = 7.0
SWIGLU_ALPHA = 1.702
DN_ALPHA = (2 * DEPTH) ** 0.25
LN_EPS = 1e-5
RMS_EPS = 1e-6

LANES = 128
VMEM_LIMIT = 56 * 1024 * 1024

ATT_BLK = 128
HGRN_T = 512
HGRN_HEADS_PER_STEP = 2
MOE_BLK = 256
SSM_T = 16
SSM_PACK = LANES // C_GROUP
S5_COL_BLOCKS = 8
DMA_GROUP = 8
COMBINE_GROUP = 32
ROW_SUB = 256
W_CHUNK = 256
EXPERT_SUB_BLOCKS = 1
NEG = -1e30


def _packed_pitch(d):
    return d // (2 * LANES) + 1


def _row_pitch(d):
    return d // LANES + 1


def _cparams(sem):
    return pltpu.CompilerParams(dimension_semantics=sem, vmem_limit_bytes=VMEM_LIMIT)


def _resident(block_shape, index_map):
    return pl.BlockSpec(block_shape, index_map, pipeline_mode=pl.Buffered(1))


def _layer_norm(z, g, b):
    mu = jnp.mean(z, axis=-1, keepdims=True)
    zc = z - mu
    var = jnp.mean(zc * zc, axis=-1, keepdims=True)
    return zc * lax.rsqrt(var + LN_EPS) * g + b


def _ada_body(ct_ref, w_ref, b_ref, o_ref):
    ct = ct_ref[...]
    cond = ct * jax.nn.sigmoid(ct)
    w = w_ref[0]
    rows = [jnp.sum(w * cond[:, b:b + 1], axis=0, keepdims=True) for b in range(ct.shape[1])]
    o_ref[0] = jnp.concatenate(rows, axis=0) + b_ref[0]


def ada_mod(c, ada_w, ada_b):
    depth, d, n6 = ada_w.shape
    bsz = c.shape[0]
    tn = 1536
    return pl.pallas_call(
        _ada_body,
        grid=(depth, n6 // tn),
        in_specs=[pl.BlockSpec((d, bsz), lambda l, j: (0, 0)),
                  pl.BlockSpec((1, d, tn), lambda l, j: (l, 0, j)),
                  pl.BlockSpec((1, 1, tn), lambda l, j: (l, 0, j))],
        out_specs=pl.BlockSpec((1, bsz, tn), lambda l, j: (l, 0, j)),
        out_shape=jax.ShapeDtypeStruct((depth, bsz, n6), F32),
        compiler_params=_cparams(("arbitrary", "arbitrary")),
        name="ada_mod",
    )(c.T, ada_w, ada_b.reshape(depth, 1, n6))


def _inproj_ab_body(x_ref, sc_ref, sh_ref, w_ref, b_ref, o_ref, xb_ref):
    @pl.when(pl.program_id(1) == 0)
    def _():
        xb_ref[...] = (x_ref[...] * (1.0 + sc_ref[0]) + sh_ref[0]).astype(BF16)

    tm = x_ref.shape[0]
    sub = min(tm, ROW_SUB)
    for r0 in range(0, tm, sub):
        rs = slice(r0, r0 + sub)
        o_ref[rs, :] = jnp.dot(xb_ref[rs, :], w_ref[...], preferred_element_type=F32) + b_ref[...]


def inproj_ab(h, sc, sh, w_bf, b, seq):
    n, d = h.shape
    nc = w_bf.shape[1]
    tm = min(1024, seq)
    tn = 1792
    per_b = seq // tm
    return pl.pallas_call(
        _inproj_ab_body,
        grid=(n // tm, nc // tn),
        in_specs=[pl.BlockSpec((tm, d), lambda i, j: (i, 0)),
                  pl.BlockSpec((1, 1, d), lambda i, j: (i // per_b, 0, 0)),
                  pl.BlockSpec((1, 1, d), lambda i, j: (i // per_b, 0, 0)),
                  pl.BlockSpec((d, tn), lambda i, j: (0, j)),
                  pl.BlockSpec((1, tn), lambda i, j: (0, j))],
        out_specs=pl.BlockSpec((tm, tn), lambda i, j: (i, j)),
        out_shape=jax.ShapeDtypeStruct((n, nc), F32),
        scratch_shapes=[pltpu.VMEM((tm, d), BF16)],
        compiler_params=_cparams(("arbitrary", "arbitrary")),
        name="inproj_ab",
    )(h, sc, sh, w_bf, b.reshape(1, nc))


def _attn_body(sink_ref, q_ref, kc_ref, vc_ref, vp_ref, pc_ref, invf_ref, o_ref, kprev_ref):
    blk = ATT_BLK
    half = A_HEAD_DIM // 2
    grp = A_HEADS // A_KV_HEADS
    i = pl.program_id(1)
    lane = lax.broadcasted_iota(I32, (blk, LANES), 1)
    first_half = (lane % A_HEAD_DIM) < half
    low_head = lane < A_HEAD_DIM
    invf = invf_ref[...]

    def cos_sin(pos_col):
        ang = pos_col.astype(F32) * invf
        s = jnp.sin(ang)
        return jnp.cos(ang), jnp.where(first_half, -s, s)

    def rope(x, cs):
        partner = jnp.where(first_half, pltpu.roll(x, LANES - half, 1), pltpu.roll(x, half, 1))
        return x * cs[0] + partner * cs[1]

    @pl.when(i == 0)
    def _():
        kprev_ref[...] = jnp.zeros_like(kprev_ref)

    cs_c = cos_sin(pc_ref[...])
    k_cur = rope(kc_ref[...], cs_c)
    k2 = jnp.concatenate([kprev_ref[...], k_cur], axis=0)
    kprev_ref[...] = k_cur
    v2 = jnp.concatenate([vp_ref[...], vc_ref[...]], axis=0)
    lane2 = lax.broadcasted_iota(I32, (2 * blk, LANES), 1)
    k2r = pltpu.roll(k2, A_HEAD_DIM, 1)
    v2r = pltpu.roll(v2, A_HEAD_DIM, 1)

    rows = lax.broadcasted_iota(I32, (grp * blk, 2 * blk), 0) % blk
    cols = lax.broadcasted_iota(I32, (grp * blk, 2 * blk), 1)
    valid = (cols > rows) & (cols <= rows + WINDOW) & ((cols >= blk) | (i > 0))

    q_chunks = [rope(q_ref[:, c * LANES:(c + 1) * LANES], cs_c) * (A_HEAD_DIM ** -0.5)
                for c in range(A_HEADS * A_HEAD_DIM // LANES)]

    for g in range(A_KV_HEADS):
        own = (lane2 < A_HEAD_DIM) if g == 0 else (lane2 >= A_HEAD_DIM)
        kg = jnp.where(own, k2, k2r).astype(BF16)
        vg = jnp.where(own, v2, v2r).astype(BF16)
        pieces = []
        sinks = []
        for r in range(grp):
            hd = g * grp + r
            qc = q_chunks[hd // 2]
            keep = low_head if hd % 2 == 0 else jnp.logical_not(low_head)
            pieces.append(jnp.where(keep, qc, 0.0).astype(BF16))
            sinks.append(jnp.full((blk, 1), sink_ref[hd], F32))
        qs = jnp.concatenate(pieces, axis=0)
        sink = jnp.concatenate(sinks, axis=0)
        s = lax.dot_general(qs, kg, (((1,), (1,)), ((), ())), preferred_element_type=F32)
        s = jnp.where(valid, s, NEG)
        m = jnp.maximum(jnp.max(s, axis=-1, keepdims=True), sink)
        p = jnp.exp(s - m)
        den = jnp.sum(p, axis=-1, keepdims=True) + jnp.exp(sink - m)
        o = jnp.dot(p.astype(BF16), vg, preferred_element_type=F32) / den
        for r2 in range(grp // 2):
            c = (g * grp) // 2 + r2
            ev = o[(2 * r2) * blk:(2 * r2 + 1) * blk]
            od = o[(2 * r2 + 1) * blk:(2 * r2 + 2) * blk]
            o_ref[:, c * LANES:(c + 1) * LANES] = jnp.where(low_head, ev, od).astype(o_ref.dtype)


def swa_attention(proj, pos_col, invf, sinks, bsz, seq):
    n = proj.shape[0]
    blk = ATT_BLK
    nb = seq // blk
    aq = A_HEADS * A_HEAD_DIM
    kcol = aq // LANES
    vcol = kcol + 1

    def cur(b, i, s):
        return b * nb + i

    def prev(b, i, s):
        return b * nb + jnp.maximum(i - 1, 0)

    grid_spec = pltpu.PrefetchScalarGridSpec(
        num_scalar_prefetch=1,
        grid=(bsz, nb),
        in_specs=[pl.BlockSpec((blk, aq), lambda b, i, s: (cur(b, i, s), 0)),
                  pl.BlockSpec((blk, LANES), lambda b, i, s: (cur(b, i, s), kcol)),
                  pl.BlockSpec((blk, LANES), lambda b, i, s: (cur(b, i, s), vcol)),
                  pl.BlockSpec((blk, LANES), lambda b, i, s: (prev(b, i, s), vcol)),
                  pl.BlockSpec((blk, 1), lambda b, i, s: (cur(b, i, s), 0)),
                  pl.BlockSpec((1, LANES), lambda b, i, s: (0, 0))],
        out_specs=pl.BlockSpec((blk, aq), lambda b, i, s: (cur(b, i, s), 0)),
        scratch_shapes=[pltpu.VMEM((blk, LANES), F32)],
    )
    return pl.pallas_call(
        _attn_body,
        grid_spec=grid_spec,
        out_shape=jax.ShapeDtypeStruct((n, aq), BF16),
        compiler_params=_cparams(("arbitrary", "arbitrary")),
        name="swa_attention",
    )(sinks, proj, proj, proj, proj, pos_col, invf)


def _hgrn_body(q_ref, f_ref, i_ref, g_ref, lb_ref, gw_ref, o_ref, st_ref):
    L = GLA_CHUNK

    @pl.when(pl.program_id(2) == 0)
    def _():
        st_ref[...] = jnp.zeros_like(st_ref)

    T = q_ref.shape[0]
    nch = T // L
    nt = (((1,), (1,)), ((), ()))
    pos = lax.broadcasted_iota(I32, (T, B_DIM), 0) % L
    r = lax.broadcasted_iota(I32, (T, T), 0)
    c = lax.broadcasted_iota(I32, (T, T), 1)
    same_chunk_causal = (r >= c) & (r // L == c // L)

    for hh in range(q_ref.shape[1] // B_DIM):
        hs = slice(hh * B_DIM, (hh + 1) * B_DIM)
        lb = lb_ref[hh]
        gw = gw_ref[hh]
        f = lb + (1.0 - lb) * jax.nn.sigmoid(f_ref[:, hs])
        kk = 1.0 - f
        qb = q_ref[:, hs]
        q = qb * jax.nn.sigmoid(qb)
        v = i_ref[:, hs]
        b = jnp.log(f)
        sh = 1
        while sh < L:
            b = b + jnp.where(pos >= sh, pltpu.roll(b, sh, 0), 0.0)
            sh *= 2

        def per_chunk(row, b=b):
            return jnp.concatenate(
                [jnp.broadcast_to(b[ci * L + row:ci * L + row + 1], (L, B_DIM)) for ci in range(nch)], axis=0)

        b_mid = per_chunk(L // 2)
        b_last = per_chunk(L - 1)
        qa = (q * jnp.exp(b - b_mid)).astype(BF16)
        ka = (kk * jnp.exp(b_mid - b)).astype(BF16)
        qe = (q * jnp.exp(b)).astype(BF16)
        kd = (kk * jnp.exp(b_last - b)).astype(BF16)
        att = lax.dot_general(qa, ka, nt, preferred_element_type=F32)
        att = jnp.where(same_chunk_causal, att, 0.0)
        o_intra = jnp.dot(att.astype(BF16), v.astype(BF16), preferred_element_type=F32)
        vt = v.T.astype(BF16)
        st = st_ref[hh]
        o_inter = []
        for ci in range(nch):
            sl = slice(ci * L, (ci + 1) * L)
            o_inter.append(lax.dot_general(qe[sl], st.astype(BF16), nt, preferred_element_type=F32))
            u_t = jnp.dot(vt[:, sl], kd[sl], preferred_element_type=F32)
            st = jnp.exp(b[ci * L + L - 1:ci * L + L]) * st + u_t
        st_ref[hh] = st
        o = o_intra + jnp.concatenate(o_inter, axis=0)
        o = o * lax.rsqrt(jnp.mean(o * o, axis=-1, keepdims=True) + RMS_EPS) * gw
        gb = g_ref[:, hs]
        o_ref[:, hs] = (o * (gb * jax.nn.sigmoid(gb))).astype(o_ref.dtype)


def hgrn2(proj, lb, gnorm_w, bsz, seq):
    n = proj.shape[0]
    t = min(HGRN_T, seq)
    nt = seq // t
    hp = HGRN_HEADS_PER_STEP
    width = hp * B_DIM
    base = (A_HEADS + 2 * A_KV_HEADS) * A_HEAD_DIM // width
    h = B_HEADS
    assert h % hp == 0 and (A_HEADS + 2 * A_KV_HEADS) * A_HEAD_DIM % width == 0

    def col(k):
        return lambda b, hd, i: (b * nt + i, base + k * (h // hp) + hd)

    return pl.pallas_call(
        _hgrn_body,
        grid=(bsz, h // hp, nt),
        in_specs=[pl.BlockSpec((t, width), col(0)),
                  pl.BlockSpec((t, width), col(1)),
                  pl.BlockSpec((t, width), col(2)),
                  pl.BlockSpec((t, width), col(3)),
                  pl.BlockSpec((hp, 1, B_DIM), lambda b, hd, i: (hd, 0, 0)),
                  pl.BlockSpec((hp, 1, B_DIM), lambda b, hd, i: (hd, 0, 0))],
        out_specs=pl.BlockSpec((t, width), lambda b, hd, i: (b * nt + i, hd)),
        out_shape=jax.ShapeDtypeStruct((n, h * B_DIM), BF16),
        scratch_shapes=[pltpu.VMEM((hp, B_DIM, B_DIM), F32)],
        compiler_params=_cparams(("arbitrary", "arbitrary", "arbitrary")),
        name="hgrn2",
    )(proj, proj, proj, proj, lb.reshape(h, 1, B_DIM), gnorm_w.reshape(h, 1, B_DIM))


def _outproj_ab_body(oa_ref, ob_ref, w_ref, b_ref, h_ref, gm_ref, lg_ref, lb_ref, o_ref):
    ka = oa_ref.shape[1]
    tm = oa_ref.shape[0]
    sub = min(tm, ROW_SUB)
    for r0 in range(0, tm, sub):
        rs = slice(r0, r0 + sub)
        y = jnp.dot(oa_ref[rs, :], w_ref[:ka, :], preferred_element_type=F32)
        y = y + jnp.dot(ob_ref[rs, :], w_ref[ka:, :], preferred_element_type=F32) + b_ref[...]
        z = DN_ALPHA * h_ref[rs, :] + (1.0 + gm_ref[0]) * y
        o_ref[rs, :] = _layer_norm(z, lg_ref[...], lb_ref[...])


def outproj_ab(oa, ob, w_bf, b, h, gm, ln_g, ln_b, seq):
    n, d = h.shape
    tm = min(512, seq)
    per_b = seq // tm
    ka, kb = oa.shape[1], ob.shape[1]
    row = lambda i: (i, 0)
    const = lambda i: (0, 0)
    return pl.pallas_call(
        _outproj_ab_body,
        grid=(n // tm,),
        in_specs=[pl.BlockSpec((tm, ka), row),
                  pl.BlockSpec((tm, kb), row),
                  _resident((ka + kb, d), const),
                  pl.BlockSpec((1, d), const),
                  pl.BlockSpec((tm, d), row),
                  pl.BlockSpec((1, 1, d), lambda i: (i // per_b, 0, 0)),
                  pl.BlockSpec((1, d), const),
                  pl.BlockSpec((1, d), const)],
        out_specs=pl.BlockSpec((tm, d), row),
        out_shape=jax.ShapeDtypeStruct((n, d), F32),
        compiler_params=_cparams(("arbitrary",)),
        name="outproj_ab",
    )(oa, ob, w_bf, b.reshape(1, d), h, gm, ln_g.reshape(1, d), ln_b.reshape(1, d))


def _router_body(h_ref, sc_ref, sh_ref, rw_ref, rb_ref, meta_ref, cnt_ref, carry_ref, rwh_ref, rwl_ref):
    tm = h_ref.shape[0]

    @pl.when(pl.program_id(0) == 0)
    def _():
        carry_ref[...] = jnp.zeros_like(carry_ref)
        w = rw_ref[...]
        w_hi = w.astype(BF16)
        rwh_ref[...] = w_hi
        rwl_ref[...] = (w - w_hi.astype(F32)).astype(BF16)

    xin = h_ref[...] * (1.0 + sc_ref[0]) + sh_ref[0]
    x_hi = xin.astype(BF16)
    x_lo = (xin - x_hi.astype(F32)).astype(BF16)
    logits = (jnp.dot(x_hi, rwh_ref[...], preferred_element_type=F32)
              + (jnp.dot(x_hi, rwl_ref[...], preferred_element_type=F32)
                 + jnp.dot(x_lo, rwh_ref[...], preferred_element_type=F32))) + rb_ref[...]
    lane = lax.broadcasted_iota(I32, (tm, LANES), 1)
    l = logits
    vals, idxs = [], []
    sel = jnp.zeros((tm, LANES), F32)
    for _ in range(TOP_K):
        m = jnp.max(l, axis=-1, keepdims=True)
        idx = jnp.min(jnp.where(l == m, lane, LANES), axis=-1, keepdims=True)
        hit = lane == idx
        vals.append(m)
        idxs.append(idx)
        sel = jnp.where(hit, 1.0, sel)
        l = jnp.where(hit, -jnp.inf, l)
    es = [jnp.exp(v - vals[0]) for v in vals]
    den = es[0] + es[1] + es[2] + es[3]
    r = lax.broadcasted_iota(I32, (tm, tm), 0)
    c = lax.broadcasted_iota(I32, (tm, tm), 1)
    tril = (r >= c).astype(BF16)
    incl = jnp.dot(tril, sel.astype(BF16), preferred_element_type=F32)
    carry = carry_ref[...]
    excl = incl - sel + carry
    meta = jnp.zeros((tm, LANES), F32)
    for k in range(TOP_K):
        rank = jnp.sum(jnp.where(lane == idxs[k], excl, 0.0), axis=-1, keepdims=True)
        meta = jnp.where(lane == k, idxs[k].astype(F32), meta)
        meta = jnp.where(lane == TOP_K + k, es[k] / den, meta)
        meta = jnp.where(lane == 2 * TOP_K + k, rank, meta)
    meta_ref[...] = meta
    carry = carry + incl[tm - 1:tm, :]
    carry_ref[...] = carry
    cnt_ref[...] = jnp.broadcast_to(carry, cnt_ref.shape)


def moe_router(h, sc, sh, rw_pad, rb_pad, seq):
    n, d = h.shape
    tm = min(512, seq)
    per_b = seq // tm
    return pl.pallas_call(
        _router_body,
        grid=(n // tm,),
        in_specs=[pl.BlockSpec((tm, d), lambda i: (i, 0)),
                  pl.BlockSpec((1, 1, d), lambda i: (i // per_b, 0, 0)),
                  pl.BlockSpec((1, 1, d), lambda i: (i // per_b, 0, 0)),
                  pl.BlockSpec((d, LANES), lambda i: (0, 0)),
                  pl.BlockSpec((1, LANES), lambda i: (0, 0))],
        out_specs=[pl.BlockSpec((tm, LANES), lambda i: (i, 0)),
                   pl.BlockSpec((8, LANES), lambda i: (0, 0))],
        out_shape=[jax.ShapeDtypeStruct((n, LANES), F32),
                   jax.ShapeDtypeStruct((8, LANES), F32)],
        scratch_shapes=[pltpu.VMEM((1, LANES), F32), pltpu.VMEM((d, LANES), BF16), pltpu.VMEM((d, LANES), BF16)],
        compiler_params=_cparams(("arbitrary",)),
        name="moe_router",
    )(h, sc, sh, rw_pad, rb_pad)


def _dispatch_body(pad_ref, dest_ref, h_ref, sc_ref, sh_ref, xs_hbm, buf_ref, zero_ref, sem, zsem):
    tm, d = h_ref.shape
    half = d // 2
    pitch = _packed_pitch(d)
    i = pl.program_id(0)
    last = pl.num_programs(0) - 1
    slot = i % 2
    n_pad = pad_ref.shape[0]

    def wait_slot(s):
        for _ in range(TOP_K):
            pltpu.make_async_copy(buf_ref.at[s], xs_hbm.at[pl.ds(0, tm * pitch)], sem.at[s]).wait()

    @pl.when(i == 0)
    def _():
        zero_ref[...] = jnp.zeros_like(zero_ref)
        buf_ref[...] = jnp.zeros_like(buf_ref)

        def zfill(q2, carry):
            for pr in range(2):
                dst = pad_ref[q2 * 2 + pr] * pitch
                pltpu.make_async_copy(zero_ref, xs_hbm.at[pl.ds(dst, pitch)], zsem.at[0]).start(priority=pr)
            return carry

        lax.fori_loop(0, n_pad // 2, zfill, 0)

    @pl.when(i >= 2)
    def _():
        wait_slot(slot)

    xin = h_ref[...] * (1.0 + sc_ref[0]) + sh_ref[0]
    lo = lax.bitcast_convert_type(xin[:, :half].astype(BF16).astype(F32), U32)
    hi = lax.bitcast_convert_type(xin[:, half:].astype(BF16).astype(F32), U32)
    word = lax.shift_right_logical(lo, jnp.uint32(16)) | (hi & jnp.uint32(0xFFFF0000))
    for s in range(half // LANES):
        buf_ref[slot, pl.ds(s, tm, stride=pitch), :] = word[:, s * LANES:(s + 1) * LANES]

    def issue(g, carry):
        for rr in range(DMA_GROUP):
            r = g * DMA_GROUP + rr
            src = buf_ref.at[slot, pl.ds(r * pitch, pitch)]
            for k in range(TOP_K):
                dst = dest_ref[0, 0, r * TOP_K + k] * pitch
                pltpu.make_async_copy(src, xs_hbm.at[pl.ds(dst, pitch)], sem.at[slot]).start(priority=k % 2)
        return carry

    lax.fori_loop(0, tm // DMA_GROUP, issue, 0)

    @pl.when(i == last)
    def _():
        wait_slot(slot)

        @pl.when(i >= 1)
        def _():
            wait_slot(1 - slot)

        for _ in range(n_pad // tm):
            pltpu.make_async_copy(buf_ref.at[0], xs_hbm.at[pl.ds(0, tm * pitch)], zsem.at[0]).wait()


def moe_dispatch(h, sc, sh, dest, pad_dest, n_rows, seq):
    n, d = h.shape
    tm = min(256, seq)
    per_b = seq // tm
    pitch = _packed_pitch(d)
    assert pad_dest.shape[0] % tm == 0
    grid_spec = pltpu.PrefetchScalarGridSpec(
        num_scalar_prefetch=1,
        grid=(n // tm,),
        in_specs=[pl.BlockSpec((1, 1, tm * TOP_K), lambda i, p: (i, 0, 0), memory_space=pltpu.SMEM),
                  pl.BlockSpec((tm, d), lambda i, p: (i, 0)),
                  pl.BlockSpec((1, 1, d), lambda i, p: (i // per_b, 0, 0)),
                  pl.BlockSpec((1, 1, d), lambda i, p: (i // per_b, 0, 0))],
        out_specs=pl.BlockSpec(memory_space=pl.ANY),
        scratch_shapes=[pltpu.VMEM((2, tm * pitch, LANES), U32), pltpu.VMEM((pitch, LANES), U32),
                        pltpu.SemaphoreType.DMA((2,)), pltpu.SemaphoreType.DMA((1,))],
    )
    return pl.pallas_call(
        _dispatch_body,
        grid_spec=grid_spec,
        out_shape=jax.ShapeDtypeStruct((n_rows * pitch, LANES), U32),
        compiler_params=_cparams(("arbitrary",)),
        name="moe_dispatch",
    )(pad_dest, dest.reshape(n // tm, 1, tm * TOP_K), h, sc, sh)


(SCHED_MODE, SCHED_BLOCK, SCHED_EXPERT, SCHED_SLOT, SCHED_DL_EXPERT, SCHED_DL_C1, SCHED_DL_C2, SCHED_DL_SLOT,
 SCHED_DELIVER) = range(9)
MODE_IDLE, MODE_COMPUTE, MODE_ZERO = 0, 1, 2


def _expert_body(sched_ref, xs_ref, w1c_ref, w2c_ref, b1_ref, b2_ref, y_ref, wb1a_ref, wb1b_ref, wb2a_ref, wb2b_ref):
    j = pl.program_id(0)
    nstep = pl.num_programs(0)
    d = wb1a_ref.shape[0]
    ff = wb2a_ref.shape[0]
    rt = d // LANES
    pitch = _row_pitch(d)
    xpitch = _packed_pitch(d)
    blk = xs_ref.shape[0] // xpitch
    cache = ((wb1a_ref, wb2a_ref), (wb1b_ref, wb2b_ref))

    def sched(field):
        return sched_ref[field * nstep + j]

    def cast_chunks(wb1_ref, wb2_ref):
        r1 = pl.multiple_of(sched(SCHED_DL_C1) * W_CHUNK, W_CHUNK)
        r2 = pl.multiple_of(sched(SCHED_DL_C2) * W_CHUNK, W_CHUNK)
        wb1_ref[pl.ds(r1, W_CHUNK), :] = w1c_ref[0].astype(BF16)
        wb2_ref[pl.ds(r2, W_CHUNK), :] = w2c_ref[0].astype(BF16)

    def compute(wb1_ref, wb2_ref):
        sub = blk // EXPERT_SUB_BLOCKS
        for r0 in range(0, blk, sub):
            words = [xs_ref[pl.ds(r0 * xpitch + s, sub, stride=xpitch), :] for s in range(rt // 2)]
            lo = [lax.bitcast_convert_type(lax.shift_left(w, jnp.uint32(16)), F32).astype(BF16) for w in words]
            hi = [lax.bitcast_convert_type(w & jnp.uint32(0xFFFF0000), F32).astype(BF16) for w in words]
            x = jnp.concatenate(lo + hi, axis=1)
            hid = jnp.dot(x, wb1_ref[...], preferred_element_type=F32) + b1_ref[0]
            gate = jnp.minimum(hid[:, :ff], SWIGLU_LIMIT)
            lin = jnp.clip(hid[:, ff:], -SWIGLU_LIMIT, SWIGLU_LIMIT)
            act = gate * jax.nn.sigmoid(SWIGLU_ALPHA * gate) * (lin + 1.0)
            y = jnp.dot(act.astype(BF16), wb2_ref[...], preferred_element_type=F32) + b2_ref[0]
            for s in range(rt):
                y_ref[pl.ds(r0 * pitch + s, sub, stride=pitch), :] = y[:, s * LANES:(s + 1) * LANES]
            for s in range(rt, pitch):
                y_ref[pl.ds(r0 * pitch + s, sub, stride=pitch), :] = jnp.zeros((sub, LANES), F32)

    mode = sched(SCHED_MODE)
    slot = sched(SCHED_SLOT)
    dl_slot = sched(SCHED_DL_SLOT)
    for par in range(2):
        @pl.when((mode == MODE_COMPUTE) & (slot == par))
        def _(par=par):
            cast_chunks(*cache[1 - par])
            compute(*cache[par])

        @pl.when((mode != MODE_COMPUTE) & (dl_slot == par) & (sched(SCHED_DELIVER) == 1))
        def _(par=par):
            cast_chunks(*cache[par])

    @pl.when(mode == MODE_ZERO)
    def _():
        y_ref[...] = jnp.zeros_like(y_ref)


def expert_schedule(padded, n_used, nblk, layer, ne, need, need2):
    nb = padded // MOE_BLK
    seg_len = jnp.concatenate([jnp.full((1,), need, I32), jnp.maximum(nb[:-1], need), nb[-1:]])
    seg_end = jnp.cumsum(seg_len)
    seg_start = seg_end - seg_len
    nstep = nblk + (ne + 1) * need
    j = jnp.arange(nstep, dtype=I32)
    seg = jnp.sum(seg_end[None, :] <= j[:, None], axis=-1).astype(I32)
    in_seg = seg <= ne
    segc = jnp.minimum(seg, ne)
    loc = j - seg_start[segc]
    expert = jnp.clip(segc - 1, 0, ne - 1)
    compute = in_seg & (segc >= 1) & (loc < nb[expert])
    done = jnp.cumsum(compute.astype(I32)) - compute.astype(I32)
    tail = j - seg_end[-1]
    ntail = nblk - n_used
    zero = jnp.logical_not(in_seg) & (tail < ntail)
    parked = jnp.where(jnp.logical_not(in_seg) & (ntail > 0), nblk - 1, jnp.maximum(done - 1, 0))
    block = jnp.where(compute, done, jnp.where(zero, n_used + tail, parked))
    mode = jnp.where(compute, MODE_COMPUTE, jnp.where(zero, MODE_ZERO, MODE_IDLE))
    deliver = in_seg & (segc <= ne - 1) & (loc < need)
    last = lax.cummax(jnp.where(deliver, j, -1), axis=0)
    dl_expert = segc[last]
    dl_c1 = loc[last]
    dl_c2 = jnp.minimum(loc[last], need2 - 1)
    table = jnp.stack([mode, block, expert + layer * ne, expert % 2,
                       dl_expert + layer * ne, dl_c1, dl_c2, dl_expert % 2, deliver.astype(I32)])
    return table.astype(I32).reshape(-1), nstep


def moe_experts(xs, sched, nstep, w1, b1, w2, b2):
    ne, d, ff2 = w1.shape
    pitch = _row_pitch(d)
    xpitch = _packed_pitch(d)
    n_rows = xs.shape[0] // xpitch
    ff = ff2 // 2

    def fld(s, j, field):
        return s[field * nstep + j]

    grid_spec = pltpu.PrefetchScalarGridSpec(
        num_scalar_prefetch=1,
        grid=(nstep,),
        in_specs=[pl.BlockSpec((MOE_BLK * xpitch, LANES), lambda j, s: (fld(s, j, SCHED_BLOCK), 0)),
                  pl.BlockSpec((1, W_CHUNK, ff2),
                               lambda j, s: (fld(s, j, SCHED_DL_EXPERT), fld(s, j, SCHED_DL_C1), 0)),
                  pl.BlockSpec((1, W_CHUNK, d),
                               lambda j, s: (fld(s, j, SCHED_DL_EXPERT), fld(s, j, SCHED_DL_C2), 0)),
                  pl.BlockSpec((1, 1, ff2), lambda j, s: (fld(s, j, SCHED_EXPERT), 0, 0)),
                  pl.BlockSpec((1, 1, d), lambda j, s: (fld(s, j, SCHED_EXPERT), 0, 0))],
        out_specs=pl.BlockSpec((MOE_BLK * pitch, LANES), lambda j, s: (fld(s, j, SCHED_BLOCK), 0)),
        scratch_shapes=[pltpu.VMEM((d, ff2), BF16), pltpu.VMEM((d, ff2), BF16),
                        pltpu.VMEM((ff, d), BF16), pltpu.VMEM((ff, d), BF16)],
    )
    return pl.pallas_call(
        _expert_body,
        grid_spec=grid_spec,
        out_shape=jax.ShapeDtypeStruct((n_rows * pitch, LANES), F32),
        compiler_params=_cparams(("arbitrary",)),
        name="moe_experts",
    )(sched, xs, w1, w2, b1.reshape(ne, 1, ff2), b2.reshape(ne, 1, d))


def _combine_body(dc_ref, dn_ref, meta_ref, h_ref, gf_ref, lg_ref, lb_ref, y_hbm, o_ref, buf_ref, sem):
    tm, d = h_ref.shape
    rt = d // LANES
    pitch = _row_pitch(d)
    grp = COMBINE_GROUP
    i = pl.program_id(0)
    n = pl.num_programs(0)
    slot = i % 2

    def issue_rows(dref, s, r0):
        for rr in range(grp):
            r = r0 + rr
            for k in range(TOP_K):
                src = dref[0, 0, r * TOP_K + k] * pitch
                pltpu.make_async_copy(y_hbm.at[pl.ds(src, rt)], buf_ref.at[s, k, pl.ds(r * pitch, rt)],
                                      sem.at[s]).start(priority=k % 2)

    @pl.when(i == 0)
    def _():
        def body(g, carry):
            issue_rows(dc_ref, 0, g * grp)
            return carry
        lax.fori_loop(0, tm // grp, body, 0)

    for k in range(TOP_K):
        pltpu.make_async_copy(y_hbm.at[pl.ds(0, tm * rt)], buf_ref.at[slot, k, pl.ds(0, tm * rt)],
                              sem.at[slot]).wait()

    def group(g, carry, prefetch):
        r0 = pl.multiple_of(g * grp, grp)
        if prefetch:
            issue_rows(dn_ref, 1 - slot, r0)
        meta = meta_ref[pl.ds(r0, grp), :]
        gates = [jnp.broadcast_to(meta[:, TOP_K + k:TOP_K + k + 1], (grp, LANES)) for k in range(TOP_K)]
        for s in range(rt):
            a = gates[0] * buf_ref[slot, 0, pl.ds(r0 * pitch + s, grp, stride=pitch), :]
            for k in range(1, TOP_K):
                a = a + gates[k] * buf_ref[slot, k, pl.ds(r0 * pitch + s, grp, stride=pitch), :]
            cs = slice(s * LANES, (s + 1) * LANES)
            o_ref[pl.ds(r0, grp), cs] = DN_ALPHA * h_ref[pl.ds(r0, grp), cs] + (1.0 + gf_ref[0, :, cs]) * a
        return carry

    @pl.when(i + 1 < n)
    def _():
        lax.fori_loop(0, tm // grp, functools.partial(group, prefetch=True), 0)

    @pl.when(i + 1 == n)
    def _():
        lax.fori_loop(0, tm // grp, functools.partial(group, prefetch=False), 0)

    o_ref[...] = _layer_norm(o_ref[...], lg_ref[...], lb_ref[...])


def moe_combine(y, dest, meta, h, gf, ln_g, ln_b, seq):
    n, d = h.shape
    tm = min(256, seq)
    per_b = seq // tm
    nblk = n // tm
    dest3 = dest.reshape(nblk, 1, tm * TOP_K)
    row = lambda i: (i, 0)
    const = lambda i: (0, 0)
    return pl.pallas_call(
        _combine_body,
        grid=(nblk,),
        in_specs=[pl.BlockSpec((1, 1, tm * TOP_K), lambda i: (i, 0, 0), memory_space=pltpu.SMEM),
                  pl.BlockSpec((1, 1, tm * TOP_K), lambda i: (jnp.minimum(i + 1, nblk - 1), 0, 0),
                               memory_space=pltpu.SMEM),
                  pl.BlockSpec((tm, LANES), row),
                  pl.BlockSpec((tm, d), row),
                  pl.BlockSpec((1, 1, d), lambda i: (i // per_b, 0, 0)),
                  pl.BlockSpec((1, d), const),
                  pl.BlockSpec((1, d), const),
                  pl.BlockSpec(memory_space=pl.ANY)],
        out_specs=pl.BlockSpec((tm, d), row),
        out_shape=jax.ShapeDtypeStruct((n, d), F32),
        scratch_shapes=[pltpu.VMEM((2, TOP_K, tm * _row_pitch(d), LANES), F32), pltpu.SemaphoreType.DMA((2,))],
        compiler_params=_cparams(("arbitrary",)),
        name="moe_combine",
    )(dest3, dest3, meta, h, gf, ln_g.reshape(1, d), ln_b.reshape(1, d), y)


def moe_layer(h, sc, sh, gf, ln_g, ln_b, router_w, router_b, layer, w1, b1, w2, b2, seq):
    n, d = h.shape
    ne = router_w.shape[1]
    rw_pad = jnp.pad(router_w, ((0, 0), (0, LANES - ne)))
    rb_pad = jnp.pad(router_b, (0, LANES - ne), constant_values=NEG).reshape(1, LANES)
    meta, cnt = moe_router(h, sc, sh, rw_pad, rb_pad, seq)
    idx4 = meta[:, 0:TOP_K].astype(I32)
    rank4 = meta[:, 2 * TOP_K:3 * TOP_K].astype(I32)
    counts = cnt[0, :ne].astype(I32)
    padded = (counts + MOE_BLK - 1) // MOE_BLK * MOE_BLK
    end = jnp.cumsum(padded)
    start = end - padded
    onehot = idx4[:, :, None] == jnp.arange(ne, dtype=I32)[None, None, :]
    dest = jnp.sum(jnp.where(onehot, start[None, None, :], 0), axis=-1) + rank4
    n_rows = -(-(n * TOP_K + ne * (MOE_BLK - 1)) // MOE_BLK) * MOE_BLK
    nblk = n_rows // MOE_BLK
    blk0 = jnp.arange(nblk, dtype=I32) * MOE_BLK
    n_used = (end[-1] // MOE_BLK).astype(I32)
    n_pad = n_rows - n * TOP_K
    seg_cnt = jnp.concatenate([padded - counts, (n_rows - end[-1])[None]])
    seg_base = jnp.concatenate([start + counts, end[-1:]])
    seg_end = jnp.cumsum(seg_cnt)
    q = jnp.arange(n_pad, dtype=I32)
    seg = jnp.sum(seg_end[None, :] <= q[:, None], axis=-1)
    pad_dest = (seg_base[seg] + q - (seg_end - seg_cnt)[seg]).astype(I32)
    xs = moe_dispatch(h, sc, sh, dest.reshape(-1), pad_dest, n_rows, seq)
    sched, nstep = expert_schedule(padded, n_used, nblk, layer, ne, w1.shape[1] // W_CHUNK, w2.shape[1] // W_CHUNK)
    y = moe_experts(xs, sched, nstep, w1, b1, w2, b2)
    return moe_combine(y, dest.reshape(-1), meta, h, gf, ln_g, ln_b, seq)


def _inproj_c_body(x_ref, sc_ref, sh_ref, w_ref, o_ref, acc_ref):
    tm = x_ref.shape[0]
    ncol = w_ref.shape[1] // LANES
    sub = min(tm, ROW_SUB)
    for r0 in range(0, tm, sub):
        rs = slice(r0, r0 + sub)
        xb = (x_ref[rs, :] * (1.0 + sc_ref[0]) + sh_ref[0]).astype(BF16)
        acc = jnp.dot(xb, w_ref[...], preferred_element_type=F32)
        for c in range(ncol):
            acc_ref[c, rs, :] = acc[:, c * LANES:(c + 1) * LANES]
        ch = slice(r0 // SSM_T, (r0 + sub) // SSM_T)
        for t in range(SSM_T):
            for c in range(ncol):
                o_ref[t, ch, c * LANES:(c + 1) * LANES] = acc_ref[c, pl.ds(r0 + t, sub // SSM_T, stride=SSM_T), :]


def inproj_c(h, sc, sh, w_bf, seq):
    n, d = h.shape
    width = w_bf.shape[1]
    tm = min(1024, seq)
    per_b = seq // tm
    return pl.pallas_call(
        _inproj_c_body,
        grid=(n // tm,),
        in_specs=[pl.BlockSpec((tm, d), lambda i: (i, 0)),
                  pl.BlockSpec((1, 1, d), lambda i: (i // per_b, 0, 0)),
                  pl.BlockSpec((1, 1, d), lambda i: (i // per_b, 0, 0)),
                  _resident((d, width), lambda i: (0, 0))],
        out_specs=pl.BlockSpec((SSM_T, tm // SSM_T, width), lambda i: (0, i, 0)),
        out_shape=jax.ShapeDtypeStruct((SSM_T, n // SSM_T, width), F32),
        scratch_shapes=[pltpu.VMEM((width // LANES, tm, LANES), F32)],
        compiler_params=_cparams(("arbitrary",)),
        name="inproj_c",
    )(h, sc, sh, w_bf)


def _s5_body(u_ref, wt_ref, wa_ref, wc_ref, lam_ref, d_ref, o_ref, st_ref, v_ref, xs_ref):
    nb, cb = u_ref.shape[1], u_ref.shape[2]
    ns = st_ref.shape[1] // 2
    kdim = SSM_T * LANES

    @pl.when(pl.program_id(1) == 0)
    def _():
        st_ref[...] = jnp.zeros_like(st_ref)

    xcat = jnp.concatenate(
        [jnp.concatenate([u_ref[t, b].astype(BF16) for t in range(SSM_T)], axis=1) for b in range(nb)], axis=0)
    v = jnp.dot(xcat, wa_ref[0], preferred_element_type=F32)
    nslab = 2 * ns // LANES
    for c in range(nslab):
        v_ref[c] = v[:, c * LANES:(c + 1) * LANES]
    lam = lam_ref[0]
    a_re, a_im = lam[:, :ns], lam[:, ns:]

    def step(r, st):
        for c in range(nslab):
            xs_ref[c, pl.ds(r, nb, stride=cb), :] = st[:, c * LANES:(c + 1) * LANES]
        v = jnp.concatenate([v_ref[c, pl.ds(r, nb, stride=cb), :] for c in range(nslab)], axis=1)
        s_re, s_im = st[:, :ns], st[:, ns:]
        n_re = a_re * s_re - a_im * s_im + v[:, :ns]
        n_im = a_re * s_im + a_im * s_re + v[:, ns:]
        return jnp.concatenate([n_re, n_im], axis=1)

    st_ref[...] = lax.fori_loop(0, cb, step, st_ref[...])
    xs_bf = jnp.concatenate([xs_ref[c].astype(BF16) for c in range(nslab)], axis=1)
    dvec = d_ref[0]
    for q in range(S5_COL_BLOCKS):
        c0, c1 = q * kdim // S5_COL_BLOCKS, (q + 1) * kdim // S5_COL_BLOCKS
        y = jnp.dot(xcat[:, :c1], wt_ref[0, :c1, c0:c1], preferred_element_type=F32)
        y = y + jnp.dot(xs_bf, wc_ref[0, :, c0:c1], preferred_element_type=F32)
        for tt in range((c1 - c0) // LANES):
            t = c0 // LANES + tt
            for b in range(nb):
                yt = y[b * cb:(b + 1) * cb, tt * LANES:(tt + 1) * LANES] + dvec * u_ref[t, b]
                gel = 0.5 * yt * (1.0 + jnp.tanh(0.7978845608028654 * (yt + 0.044715 * yt * yt * yt)))
                o_ref[t, b] = gel.astype(o_ref.dtype)


def s5_mixer(u2, wt, wa, wc, lam_t, dpack, bsz, seq):
    _, nchunk, width = u2.shape
    npack = width // LANES
    per_b = seq // SSM_T
    cb = min(256, per_b)
    nblk = per_b // cb
    ns2 = wa.shape[2]
    kdim = SSM_T * LANES
    pk = lambda j, i: (j, 0, 0)
    u4 = u2.reshape(SSM_T, bsz, per_b, width)
    y4 = pl.pallas_call(
        _s5_body,
        grid=(npack, nblk),
        in_specs=[pl.BlockSpec((SSM_T, bsz, cb, LANES), lambda j, i: (0, 0, i, j)),
                  _resident((1, kdim, kdim), pk),
                  _resident((1, kdim, ns2), pk),
                  _resident((1, ns2, kdim), pk),
                  pl.BlockSpec((1, 1, ns2), pk),
                  pl.BlockSpec((1, 1, LANES), pk)],
        out_specs=pl.BlockSpec((SSM_T, bsz, cb, LANES), lambda j, i: (0, 0, i, j)),
        out_shape=jax.ShapeDtypeStruct((SSM_T, bsz, per_b, width), F32),
        scratch_shapes=[pltpu.VMEM((bsz, ns2), F32), pltpu.VMEM((ns2 // LANES, bsz * cb, LANES), F32),
                        pltpu.VMEM((ns2 // LANES, bsz * cb, LANES), F32)],
        compiler_params=_cparams(("arbitrary", "arbitrary")),
        name="s5_mixer",
    )(u4, wt, wa, wc, lam_t, dpack)
    return y4.reshape(SSM_T, nchunk, width)


def s5_operators(a_re, a_im, log_dt, b_re, b_im, c_re, c_im, dskip):
    g, p = a_re.shape
    cg = b_re.shape[2]
    npack = g // SSM_PACK
    ns2 = SSM_PACK * 2 * p
    t = SSM_T
    lam = lax.complex(a_re, a_im)
    dt = jnp.exp(log_dt)[:, None]
    lam_bar = jnp.exp(lam * dt)
    b_bar = ((lam_bar - 1.0) / lam)[..., None] * lax.complex(b_re, b_im)
    c_mat = lax.complex(c_re, c_im)
    kpow = jnp.arange(t + 1, dtype=F32)[:, None, None]
    pows = jnp.exp((lam * dt)[None] * kpow)
    lane_c = jnp.arange(LANES) % cg
    lane_g = jnp.arange(LANES) // cg
    tile = (lane_c[None, :] == jnp.arange(cg)[:, None]).astype(F32)
    diag = lane_g[:, None] == lane_g[None, :]
    kern = jnp.real(jnp.einsum('gcp,kgp,gpd->kgdc', c_mat, pows[:t], b_bar, precision=HIGHEST))
    kfull = jnp.where(diag, jnp.dot(kern.reshape(t, npack, LANES, cg), tile, precision=HIGHEST), 0.0)
    kfull = kfull.astype(BF16)
    kcat = jnp.concatenate([kfull[tau] for tau in range(t)], axis=-1)
    wt = jnp.stack([jnp.pad(kcat[:, :, :(t - s) * LANES], ((0, 0), (0, 0), (s * LANES, 0)))
                    for s in range(t)], axis=1)
    wt = wt.reshape(npack, t * LANES, t * LANES)
    ns = SSM_PACK * p
    col_g = jnp.arange(ns) // p
    tile_p = (jnp.arange(ns)[None, :] % p == jnp.arange(p)[:, None]).astype(F32)
    wa_c = pows[t - 1 - jnp.arange(t)][:, :, :, None] * b_bar[None]
    wa_c = wa_c.reshape(t, npack, SSM_PACK, p, cg).transpose(1, 0, 2, 4, 3)
    wa_c = wa_c.reshape(npack, t * LANES, p)
    own_a = jnp.tile(lane_g, t)[:, None] == col_g[None, :]
    zero_bf = jnp.zeros((), BF16)
    wa = jnp.concatenate([jnp.where(own_a, jnp.dot(part.astype(BF16), tile_p.astype(BF16),
                                                  preferred_element_type=BF16), zero_bf)
                          for part in (jnp.real(wa_c), jnp.imag(wa_c))], axis=-1)
    wc_c = c_mat[None] * pows[1:t + 1][:, :, None, :]
    wc_c = wc_c.transpose(0, 1, 3, 2).reshape(t, npack, ns, cg)
    cc = jnp.concatenate([jnp.real(wc_c), -jnp.imag(wc_c)], axis=2)
    own_c = (jnp.arange(ns2) % ns // p)[:, None] == lane_g[None, :]
    wcf = jnp.where(own_c, jnp.dot(cc.astype(BF16), tile.astype(BF16), preferred_element_type=BF16),
                    zero_bf)
    wc = jnp.concatenate([wcf[tt] for tt in range(t)], axis=-1)
    lam_t = pows[t].reshape(npack, 1, ns)
    lam_t = jnp.concatenate([jnp.real(lam_t), jnp.imag(lam_t)], axis=-1)
    dpack = dskip.reshape(npack, 1, LANES)
    return wt, wa, wc, lam_t, dpack


def _glu_out_body(y_ref, wg_ref, bg_ref, wo_ref, h_ref, gm_ref, lg_ref, lb_ref, o_ref, ytm_ref):
    cb = y_ref.shape[1]
    ncol = y_ref.shape[2] // LANES
    for t in range(SSM_T):
        for c in range(ncol):
            ytm_ref[c, pl.ds(t, cb, stride=SSM_T), :] = y_ref[t, :, c * LANES:(c + 1) * LANES]
    tm = h_ref.shape[0]
    sub = min(tm, ROW_SUB)
    for r0 in range(0, tm, sub):
        rs = slice(r0, r0 + sub)
        y = jnp.concatenate([ytm_ref[c, rs, :] for c in range(ncol)], axis=1)
        gate = jnp.dot(y.astype(BF16), wg_ref[...], preferred_element_type=F32) + bg_ref[...]
        z = (y * jax.nn.sigmoid(gate)).astype(BF16)
        out = jnp.dot(z, wo_ref[...], preferred_element_type=F32)
        zz = DN_ALPHA * h_ref[rs, :] + (1.0 + gm_ref[0]) * out
        o_ref[rs, :] = _layer_norm(zz, lg_ref[...], lb_ref[...])


def glu_out(y2, wg_bf, bg, wo_bf, h, gm, ln_g, ln_b, seq):
    n, d = h.shape
    width = y2.shape[2]
    tm = min(512, seq)
    cb = tm // SSM_T
    per_b = seq // tm
    row = lambda i: (i, 0)
    const = lambda i: (0, 0)
    return pl.pallas_call(
        _glu_out_body,
        grid=(n // tm,),
        in_specs=[pl.BlockSpec((SSM_T, cb, width), lambda i: (0, i, 0)),
                  _resident((width, width), const),
                  pl.BlockSpec((1, width), const),
                  _resident((width, d), const),
                  pl.BlockSpec((tm, d), row),
                  pl.BlockSpec((1, 1, d), lambda i: (i // per_b, 0, 0)),
                  pl.BlockSpec((1, d), const),
                  pl.BlockSpec((1, d), const)],
        out_specs=pl.BlockSpec((tm, d), row),
        out_shape=jax.ShapeDtypeStruct((n, d), F32),
        scratch_shapes=[pltpu.VMEM((width // LANES, tm, LANES), F32)],
        compiler_params=_cparams(("arbitrary",)),
        name="glu_out",
    )(y2, wg_bf, bg.reshape(1, width), wo_bf, h, gm, ln_g.reshape(1, d), ln_b.reshape(1, d))


def kernel(x, c, positions, ada_w, ada_b, ln_g, ln_b, ab_in_w, ab_in_b, ab_sinks, ab_gnorm_w, ab_out_w, ab_out_b, hgrn_lb_logits, c_in_w, c_A_re, c_A_im, c_log_dt, c_B_re, c_B_im, c_C_re, c_C_im, c_D, c_glu_w, c_glu_b, c_out_w, router_w, router_b, exp_w1, exp_b1, exp_w2, exp_b2):
    bsz, seq, d = x.shape
    n = bsz * seq
    depth = ada_w.shape[0]
    mod = ada_mod(c, ada_w, ada_b)
    lb_all = jnp.cumsum(jax.nn.softmax(hgrn_lb_logits.astype(F32), axis=0), axis=0)
    half = A_HEAD_DIM // 2
    inv_freq = ROPE_THETA ** (-jnp.arange(half, dtype=F32) / half)
    invf = jnp.tile(inv_freq, LANES // half).reshape(1, LANES)
    pos_col = positions.reshape(n, 1)
    ne_all = depth * exp_w1.shape[1]
    w1_all = exp_w1.reshape(ne_all, d, exp_w1.shape[3])
    w2_all = exp_w2.reshape(ne_all, exp_w2.shape[2], d)
    b1_all = exp_b1.reshape(ne_all, exp_b1.shape[2])
    b2_all = exp_b2.reshape(ne_all, d)
    h = x.reshape(n, d)
    for layer in range(depth):
        m6 = mod[layer].reshape(bsz, 6, 1, d)
        sh_m, sc_m, g_m, sh_f, sc_f, g_f = [m6[:, k] for k in range(6)]
        i = layer // 2
        if layer % 2 == 0:
            proj = inproj_ab(h, sc_m, sh_m, ab_in_w[i].astype(BF16), ab_in_b[i], seq)
            oa = swa_attention(proj, pos_col, invf, ab_sinks[i], bsz, seq)
            ob = hgrn2(proj, lb_all[layer], ab_gnorm_w[i], bsz, seq)
            h = outproj_ab(oa, ob, ab_out_w[i].astype(BF16), ab_out_b[i], h, g_m,
                           ln_g[layer, 0], ln_b[layer, 0], seq)
        else:
            wt, wa, wc, lam_t, dpack = s5_operators(c_A_re[i], c_A_im[i], c_log_dt[i], c_B_re[i], c_B_im[i],
                                                    c_C_re[i], c_C_im[i], c_D[i])
            u2 = inproj_c(h, sc_m, sh_m, c_in_w[i].astype(BF16), seq)
            y2 = s5_mixer(u2, wt, wa, wc, lam_t, dpack, bsz, seq)
            h = glu_out(y2, c_glu_w[i].astype(BF16), c_glu_b[i], c_out_w[i].astype(BF16), h, g_m,
                        ln_g[layer, 0], ln_b[layer, 0], seq)
        h = moe_layer(h, sc_f, sh_f, g_f, ln_g[layer, 1], ln_b[layer, 1], router_w[layer], router_b[layer],
                      layer, w1_all, b1_all, w2_all, b2_all, seq)
    return h.reshape(bsz, seq, d)
```
